```python
import math
import jax, jax.numpy as jnp
from jax import lax
import numpy as np

D_MODEL = 1024
BATCH = 2
SEQ = 8192
DEPTH = 1

CHUNK = 64
Q_BLOCK = 128
MIX_WIDTH = D_MODEL
RWKV_WIDTH = 512
RWKV_HEAD = 64
RWKV_HEADS = RWKV_WIDTH // RWKV_HEAD
DECAY_LORA = 64
AAA_LORA = 64
GATE_LORA = 128
DIFF_WIDTH = MIX_WIDTH - RWKV_WIDTH
DIFF_HALF = 64
DIFF_VDIM = 2 * DIFF_HALF
DIFF_HEADS = DIFF_WIDTH // DIFF_VDIM
D_FF = ((8 * D_MODEL // 3 + 255) // 256) * 256
NORM_EPS = 1e-6
GN_EPS = 1e-5 * RWKV_HEAD
SUBLN_EPS = 1e-5

RWKV_COLS = 3 * RWKV_WIDTH + DECAY_LORA + AAA_LORA + GATE_LORA
DIFF_COLS = 3 * DIFF_WIDTH
D_IN = RWKV_COLS + DIFF_COLS
RWKV_SPLITS = (RWKV_WIDTH, 2 * RWKV_WIDTH, 3 * RWKV_WIDTH,
               3 * RWKV_WIDTH + DECAY_LORA, 3 * RWKV_WIDTH + DECAY_LORA + AAA_LORA)

kernel_name = "hymba_rwkv7_diffattn_block"


def _rms_norm(x, w, eps=NORM_EPS):
    xf = x.astype(jnp.float32)
    y = xf * lax.rsqrt(jnp.mean(xf * xf, axis=-1, keepdims=True) + eps)
    return (y * w.astype(jnp.float32)).astype(x.dtype)


def _token_shift(p, mu):
    p_prev = jnp.pad(p, ((0, 0), (1, 0), (0, 0)))[:, :-1]
    return p + (p_prev - p) * mu


def _wkv7_scan(r, w, k, v, a, b):
    bsz, _, nh, n = r.shape

    def step(S, inp):
        r_t, w_t, k_t, v_t, a_t, b_t = inp
        sa = jnp.einsum('bhij,bhj->bhi', S, a_t)
        S = (S * w_t[:, :, None, :] + sa[..., None] * b_t[:, :, None, :]
             + v_t[..., None] * k_t[:, :, None, :])
        y_t = jnp.einsum('bhij,bhj->bhi', S, r_t)
        return S, y_t

    xs = tuple(jnp.moveaxis(t.astype(jnp.float32), 1, 0) for t in (r, w, k, v, a, b))
    S0 = jnp.zeros((bsz, nh, n, n), jnp.float32)
    _, y = lax.scan(step, S0, xs)
    return jnp.moveaxis(y, 0, 1)


def _rwkv7_mixer(p_rwkv, w0, w2, a0, a2, g2, k_k, k_a, r_k, ln_w, ln_b):
    bsz, t, _ = p_rwkv.shape
    dt = p_rwkv.dtype
    p_r, p_k, p_v, p_w, p_a, p_g = jnp.split(p_rwkv, RWKV_SPLITS, axis=-1)
    hs = (bsz, t, RWKV_HEADS, RWKV_HEAD)
    w = -jax.nn.softplus(-(w0 + jnp.tanh(p_w) @ w2)) - 0.5
    decay = jnp.exp(-jnp.exp(w.astype(jnp.float32)))
    a = jax.nn.sigmoid(a0 + p_a @ a2)
    g = jax.nn.sigmoid(p_g) @ g2
    kk = (p_k * k_k).reshape(hs).astype(jnp.float32)
    kk = kk / jnp.maximum(jnp.sqrt(jnp.sum(kk * kk, -1, keepdims=True)), 1e-12)
    k = p_k * (1.0 + (a - 1.0) * k_a)
    r4, k4, v4, a4 = (z.reshape(hs) for z in (p_r, k, p_v, a))
    y = _wkv7_scan(r4, decay.reshape(hs), k4, v4, -kk, kk * a4.astype(jnp.float32))
    mean = jnp.mean(y, -1, keepdims=True)
    var = jnp.mean(jnp.square(y - mean), -1, keepdims=True)
    y = ((y - mean) * lax.rsqrt(var + GN_EPS)).reshape(bsz, t, RWKV_WIDTH)
    y = (y * ln_w.astype(jnp.float32) + ln_b.astype(jnp.float32)).astype(dt)
    bonus = jnp.sum(r4 * k4 * r_k, -1, keepdims=True) * v4
    return (y + bonus.reshape(bsz, t, RWKV_WIDTH)) * g


def _diff_attention(p_diff, lq1, lk1, lq2, lk2, subln_w, lambda_init):
    bsz, t, _ = p_diff.shape
    q, k, v = jnp.split(p_diff, (DIFF_WIDTH, 2 * DIFF_WIDTH), axis=-1)
    q = q.reshape(bsz, t, 2 * DIFF_HEADS, DIFF_HALF)
    k = k.reshape(bsz, t, 2 * DIFF_HEADS, DIFF_HALF)
    v = v.reshape(bsz, t, DIFF_HEADS, DIFF_VDIM)
    f32 = jnp.float32
    lam = (jnp.exp(jnp.sum(lq1.astype(f32) * lk1.astype(f32)))
           - jnp.exp(jnp.sum(lq2.astype(f32) * lk2.astype(f32))) + lambda_init)
    scale = DIFF_HALF ** -0.5
    k_chunk = jnp.arange(t) // CHUNK

    def block(i):
        qb = lax.dynamic_slice_in_dim(q, i * Q_BLOCK, Q_BLOCK, axis=1)
        s = jnp.einsum('bqhd,bkhd->bhqk', qb, k).astype(f32) * scale
        q_chunk = (i * Q_BLOCK + jnp.arange(Q_BLOCK)) // CHUNK
        mask = k_chunk[None, :] <= q_chunk[:, None]
        s = jnp.where(mask[None, None], s, -jnp.inf)
        p = jax.nn.softmax(s, axis=-1).reshape(bsz, DIFF_HEADS, 2, Q_BLOCK, t)
        attn = p[:, :, 0] - lam * p[:, :, 1]
        return jnp.einsum('bhqk,bkhe->bqhe', attn.astype(v.dtype), v)

    out = lax.map(block, jnp.arange(t // Q_BLOCK))
    out = jnp.moveaxis(out, 0, 1).reshape(bsz, t, DIFF_HEADS, DIFF_VDIM)
    out = _rms_norm(out, subln_w, SUBLN_EPS) * (1.0 - lambda_init)
    return out.reshape(bsz, t, DIFF_WIDTH)


def setup_inputs(seed: int = 0) -> dict:
    key = jax.random.key(seed)
    ks = jax.random.split(key, 26)
    L = DEPTH

    def nrm(k, shape, scale):
        return jax.random.normal(k, shape, jnp.float32) * scale

    def gain(k, shape):
        return 1.0 + 0.02 * jax.random.normal(k, shape, jnp.float32)

    return {
        "x": jax.random.normal(ks[0], (BATCH, SEQ, D_MODEL), jnp.float32),
        "norm_mix_w": gain(ks[1], (L, D_MODEL)),
        "w_in": nrm(ks[2], (L, D_MODEL, D_IN), D_MODEL ** -0.5),
        "mu_shift": jax.random.uniform(ks[3], (L, RWKV_COLS), jnp.float32),
        "w0": jax.random.uniform(ks[4], (L, RWKV_WIDTH), jnp.float32, -6.0, -1.0),
        "w_lora_up": nrm(ks[5], (L, DECAY_LORA, RWKV_WIDTH), 0.1 * DECAY_LORA ** -0.5),
        "a0": nrm(ks[6], (L, RWKV_WIDTH), 0.1),
        "a_lora_up": nrm(ks[7], (L, AAA_LORA, RWKV_WIDTH), 0.5 * AAA_LORA ** -0.5),
        "g_lora_up": nrm(ks[8], (L, GATE_LORA, RWKV_WIDTH), GATE_LORA ** -0.5),
        "k_k": 0.85 + 0.05 * jax.random.normal(ks[9], (L, RWKV_WIDTH), jnp.float32),
        "k_a": gain(ks[10], (L, RWKV_WIDTH)),
        "r_k": nrm(ks[11], (L, RWKV_HEADS, RWKV_HEAD), 0.1),
        "ln_x_w": gain(ks[12], (L, RWKV_WIDTH)),
        "ln_x_b": nrm(ks[13], (L, RWKV_WIDTH), 0.01),
        "lambda_q1": nrm(ks[14], (L, DIFF_HALF), 0.1),
        "lambda_k1": nrm(ks[15], (L, DIFF_HALF), 0.1),
        "lambda_q2": nrm(ks[16], (L, DIFF_HALF), 0.1),
        "lambda_k2": nrm(ks[17], (L, DIFF_HALF), 0.1),
        "subln_w": gain(ks[18], (L, DIFF_VDIM)),
        "w_out": nrm(ks[19], (L, MIX_WIDTH, D_MODEL), MIX_WIDTH ** -0.5),
        "norm_ffn_w": gain(ks[20], (L, D_MODEL)),
        "w_gate": nrm(ks[21], (L, D_MODEL, D_FF), D_MODEL ** -0.5),
        "w_up": nrm(ks[22], (L, D_MODEL, D_FF), D_MODEL ** -0.5),
        "w_down": nrm(ks[23], (L, D_FF, D_MODEL), D_FF ** -0.5),
        "norm_final_w": gain(ks[24], (D_MODEL,)),
    }


def reference(x, norm_mix_w, w_in, mu_shift, w0, w_lora_up, a0, a_lora_up, g_lora_up,
              k_k, k_a, r_k, ln_x_w, ln_x_b, lambda_q1, lambda_k1, lambda_q2, lambda_k2,
              subln_w, w_out, norm_ffn_w, w_gate, w_up, w_down, norm_final_w):
    h = x
    for l in range(DEPTH):
        lambda_init = 0.8 - 0.6 * math.exp(-0.3 * l)
        u = _rms_norm(h, norm_mix_w[l])
        proj = u @ w_in[l]
        p_rwkv = _token_shift(proj[..., :RWKV_COLS], mu_shift[l])
        p_diff = proj[..., RWKV_COLS:]
        y_a = _rwkv7_mixer(p_rwkv, w0[l], w_lora_up[l], a0[l], a_lora_up[l], g_lora_up[l],
                           k_k[l], k_a[l], r_k[l], ln_x_w[l], ln_x_b[l])
        y_b = _diff_attention(p_diff, lambda_q1[l], lambda_k1[l], lambda_q2[l], lambda_k2[l],
                              subln_w[l], lambda_init)
        h = h + jnp.concatenate([y_a, y_b], axis=-1) @ w_out[l]
        u = _rms_norm(h, norm_ffn_w[l])
        h = h + (jax.nn.silu(u @ w_gate[l]) * (u @ w_up[l])) @ w_down[l]
    return _rms_norm(h, norm_final_w)
```

```python
import functools
import math

import jax
import jax.numpy as jnp
from jax import lax
from jax.experimental import pallas as pl
from jax.experimental.pallas import tpu as pltpu

F32 = jnp.float32
BF16 = jnp.bfloat16

ATTN_CHUNK = 64
RWKV_WIDTH = 512
RWKV_HEAD = 64
RWKV_HEADS = RWKV_WIDTH // RWKV_HEAD
DECAY_LORA = 64
AAA_LORA = 64
GATE_LORA = 128
DIFF_WIDTH = 512
DIFF_HALF = 64
DIFF_VDIM = 2 * DIFF_HALF
DIFF_HEADS = DIFF_WIDTH // DIFF_VDIM
NORM_EPS = 1e-6
GN_EPS = 1e-5 * RWKV_HEAD
SUBLN_EPS = 1e-5
RWKV_COLS = 3 * RWKV_WIDTH + DECAY_LORA + AAA_LORA + GATE_LORA
DIFF_COLS = 3 * DIFF_WIDTH

WKV_CHUNK = 64

VMEM_LIMIT_BYTES = 56 * 1024 * 1024


def _nt(a, b):
    return lax.dot_general(a, b, (((1,), (1,)), ((), ())), preferred_element_type=F32)


def _tn(a, b):
    return lax.dot_general(a, b, (((0,), (0,)), ((), ())), preferred_element_type=F32)


def _mm(a, b):
    return jnp.dot(a, b, preferred_element_type=F32)


def _rms(x, w, eps):
    return x * lax.rsqrt(jnp.mean(x * x, axis=-1, keepdims=True) + eps) * w


def _sigmoid(x):
    return 1.0 / (1.0 + jnp.exp(-x))


def _split_dot_left(mat_bf16, x, terms):
    acc = None
    rem = x
    for _ in range(terms):
        part = rem.astype(BF16)
        d = _mm(mat_bf16, part)
        acc = d if acc is None else acc + d
        rem = rem - part.astype(F32)
    return acc


def _split_dot_right(x, mat_bf16, terms):
    acc = None
    rem = x
    for _ in range(terms):
        part = rem.astype(BF16)
        d = _mm(part, mat_bf16)
        acc = d if acc is None else acc + d
        rem = rem - part.astype(F32)
    return acc


def _in_proj_kernel(x_ref, nw_ref, w_ref, prw_ref, qkv_ref, *, tn):
    ub = _rms(x_ref[...], nw_ref[...], NORM_EPS).astype(BF16)
    for n0 in range(0, RWKV_COLS + DIFF_COLS, tn):
        acc = _mm(ub, w_ref[:, n0:n0 + tn])
        if n0 < RWKV_COLS:
            prw_ref[:, n0:n0 + tn] = acc
        else:
            c0 = n0 - RWKV_COLS
            if c0 < DIFF_WIDTH:
                acc = acc * (DIFF_HALF ** -0.5)
            qkv_ref[:, c0:c0 + tn] = acc.astype(BF16)


def _in_proj(x2, norm_w, w_in_bf16, *, tm=512, tn=256):
    m, d = x2.shape
    const = dict(pipeline_mode=pl.Buffered(1))
    return pl.pallas_call(
        functools.partial(_in_proj_kernel, tn=tn),
        grid=(m // tm,),
        in_specs=[
            pl.BlockSpec((tm, d), lambda i: (i, 0)),
            pl.BlockSpec((1, d), lambda i: (0, 0), **const),
            pl.BlockSpec(w_in_bf16.shape, lambda i: (0, 0), **const),
        ],
        out_specs=[
            pl.BlockSpec((tm, RWKV_COLS), lambda i: (i, 0)),
            pl.BlockSpec((tm, DIFF_COLS), lambda i: (i, 0)),
        ],
        out_shape=[
            jax.ShapeDtypeStruct((m, RWKV_COLS), F32),
            jax.ShapeDtypeStruct((m, DIFF_COLS), BF16),
        ],
        compiler_params=pltpu.CompilerParams(
            dimension_semantics=("arbitrary",), vmem_limit_bytes=VMEM_LIMIT_BYTES),
        name="in_proj",
    )(x2, norm_w.reshape(1, d), w_in_bf16)


def _rwkv_prep_kernel(p_ref, prev_ref, mu_ref, w0_ref, w2_ref, a0_ref, a2_ref, g2_ref,
                      kk_ref, ka_ref, rk_ref,
                      rt_ref, kt_ref, bt_ref, at_ref, v_ref, gam_ref, bonus_ref, g_ref,
                      *, tm, tiles_per_seq):
    i = pl.program_id(0)
    seq_start = (i % tiles_per_seq) == 0

    def shifted(c0, c1):
        p = p_ref[:, c0:c1]
        carry = jnp.where(seq_start, 0.0, prev_ref[7:8, c0:c1])
        row = lax.broadcasted_iota(jnp.int32, p.shape, 0)
        p_prev = jnp.where(row == 0, carry, pltpu.roll(p, 1, axis=0))
        return p + (p_prev - p) * mu_ref[:, c0:c1]

    w = RWKV_WIDTH
    p_r = shifted(0, w)
    p_k = shifted(w, 2 * w)
    p_v = shifted(2 * w, 3 * w)
    p_wa = shifted(3 * w, 3 * w + DECAY_LORA + AAA_LORA)
    p_g = shifted(3 * w + DECAY_LORA + AAA_LORA, RWKV_COLS)

    z = w0_ref[...] + _mm(jnp.tanh(p_wa).astype(BF16), w2_ref[...])
    softplus_neg = jnp.maximum(-z, 0.0) + jnp.log(1.0 + jnp.exp(-jnp.abs(z)))
    log_decay = -jnp.exp(-softplus_neg - 0.5)
    a_sig = _sigmoid(a0_ref[...] + _mm(p_wa.astype(BF16), a2_ref[...]))
    gate = _mm(_sigmoid(p_g).astype(BF16), g2_ref[...])

    r_i = lax.broadcasted_iota(jnp.int32, (w, w), 0) // RWKV_HEAD
    c_i = lax.broadcasted_iota(jnp.int32, (w, w), 1) // RWKV_HEAD
    head_ones = (r_i == c_i).astype(BF16)

    kk = p_k * kk_ref[...]
    kk_norm = jnp.sqrt(_split_dot_right(kk * kk, head_ones, 2))
    kk = kk / jnp.maximum(kk_norm, 1e-12)
    k_fin = p_k * (1.0 + (a_sig - 1.0) * ka_ref[...])
    bonus = _split_dot_right(p_r * k_fin * rk_ref[...], head_ones, 2) * p_v

    t_r = lax.broadcasted_iota(jnp.int32, (tm, tm), 0)
    t_c = lax.broadcasted_iota(jnp.int32, (tm, tm), 1)
    tri = ((t_r // WKV_CHUNK == t_c // WKV_CHUNK) & (t_c <= t_r)).astype(BF16)
    lg = _split_dot_left(tri, log_decay, 3)

    gam = jnp.exp(lg)
    gam_inv = jnp.exp(-lg)
    gam_prev = jnp.exp(lg - log_decay)
    rt = p_r * gam
    kt = k_fin * gam_inv
    bt = kk * a_sig * gam_inv
    at = -kk * gam_prev

    for h in range(RWKV_HEADS):
        sl = slice(h * RWKV_HEAD, (h + 1) * RWKV_HEAD)
        rt_ref[h] = rt[:, sl].astype(BF16)
        kt_ref[h] = kt[:, sl].astype(BF16)
        bt_ref[h] = bt[:, sl].astype(BF16)
        at_ref[h] = at[:, sl].astype(BF16)
        v_ref[h] = p_v[:, sl].astype(BF16)
        gam_ref[h] = gam[:, sl]
        bonus_ref[h] = bonus[:, sl]
        g_ref[h] = gate[:, sl]


def _rwkv_prep(prw, mu, w0, w2pad, a0, a2pad, g2, k_k, k_a, r_k, *, seq, tm=512):
    m = prw.shape[0]
    w = RWKV_WIDTH
    const = dict(pipeline_mode=pl.Buffered(1))
    row = lambda a: a.reshape(1, -1)
    head_major = lambda dt: jax.ShapeDtypeStruct((RWKV_HEADS, m, RWKV_HEAD), dt)
    hm_spec = pl.BlockSpec((RWKV_HEADS, tm, RWKV_HEAD), lambda i: (0, i, 0))
    vec = lambda n: pl.BlockSpec((1, n), lambda i: (0, 0), **const)
    return pl.pallas_call(
        functools.partial(_rwkv_prep_kernel, tm=tm, tiles_per_seq=seq // tm),
        grid=(m // tm,),
        in_specs=[
            pl.BlockSpec((tm, RWKV_COLS), lambda i: (i, 0)),
            pl.BlockSpec((8, RWKV_COLS), lambda i: (jnp.maximum(i * (tm // 8) - 1, 0), 0)),
            vec(RWKV_COLS), vec(w),
            pl.BlockSpec(w2pad.shape, lambda i: (0, 0), **const),
            vec(w),
            pl.BlockSpec(a2pad.shape, lambda i: (0, 0), **const),
            pl.BlockSpec(g2.shape, lambda i: (0, 0), **const),
            vec(w), vec(w), vec(w),
        ],
        out_specs=[hm_spec] * 8,
        out_shape=[head_major(BF16)] * 5 + [head_major(F32)] * 3,
        compiler_params=pltpu.CompilerParams(
            dimension_semantics=("arbitrary",), vmem_limit_bytes=VMEM_LIMIT_BYTES),
        name="rwkv_prep",
    )(prw, prw, row(mu), row(w0), w2pad, row(a0), a2pad, g2, row(k_k), row(k_a), row(r_k))


def _wkv_kernel(rt_ref, kt_ref, bt_ref, at_ref, v_ref, gam_ref, y_ref, s_ref,
                *, tc, tiles_per_seq):
    ti = pl.program_id(1)

    @pl.when(ti % tiles_per_seq == 0)
    def _():
        s_ref[...] = jnp.zeros_like(s_ref)

    c = WKV_CHUNK
    row = lax.broadcasted_iota(jnp.int32, (c, c), 0)
    col = lax.broadcasted_iota(jnp.int32, (c, c), 1)
    strict = row > col
    incl = row >= col
    eye = (row == col).astype(F32)

    s = s_ref[...]
    for ci in range(tc // c):
        sl = slice(ci * c, (ci + 1) * c)
        a = at_ref[0, sl, :]
        r = rt_ref[0, sl, :]
        b = bt_ref[0, sl, :]
        k = kt_ref[0, sl, :]
        v = v_ref[0, sl, :]
        a_ab = jnp.where(strict, _nt(a, b), 0.0)
        a_ak = jnp.where(strict, _nt(a, k), 0.0)
        a_rb = jnp.where(incl, _nt(r, b), 0.0)
        a_rk = jnp.where(incl, _nt(r, k), 0.0)

        t_inv = eye + a_ab
        power = a_ab
        for _ in range(int(math.log2(c)) - 1):
            pb = power.astype(BF16)
            power = _mm(pb, pb)
            t_inv = _mm(t_inv.astype(BF16), (eye + power).astype(BF16))
        t_b = t_inv.astype(BF16)

        w_mat = _mm(t_b, a)
        u_free = _mm(t_b, _mm(a_ak.astype(BF16), v).astype(BF16))
        y_free = _mm(a_rk.astype(BF16), v)
        kv = _tn(v, k)

        s_b = s.astype(BF16)
        u = (_nt(w_mat.astype(BF16), s_b) + u_free).astype(BF16)
        y = _nt(r, s_b) + _mm(a_rb.astype(BF16), u) + y_free
        g_last = gam_ref[0, ci * c + c - 1:(ci + 1) * c, :]
        s = (s + _tn(u, b) + kv) * g_last

        mean = jnp.mean(y, axis=-1, keepdims=True)
        yc = y - mean
        var = jnp.mean(yc * yc, axis=-1, keepdims=True)
        y_ref[0, sl, :] = yc * lax.rsqrt(var + GN_EPS)
    s_ref[...] = s


def _wkv(rt, kt, bt, at, v, gam, *, seq, tc=256):
    heads, m, n = rt.shape
    spec = pl.BlockSpec((1, tc, n), lambda h, t: (h, t, 0))
    return pl.pallas_call(
        functools.partial(_wkv_kernel, tc=tc, tiles_per_seq=seq // tc),
        grid=(heads, m // tc),
        in_specs=[spec] * 6,
        out_specs=spec,
        out_shape=jax.ShapeDtypeStruct((heads, m, n), F32),
        scratch_shapes=[pltpu.VMEM((n, n), F32)],
        compiler_params=pltpu.CompilerParams(
            dimension_semantics=("arbitrary", "arbitrary"), vmem_limit_bytes=VMEM_LIMIT_BYTES),
        name="wkv",
    )(rt, kt, bt, at, v, gam)


def _diff_attn_kernel(q_ref, k_ref, v_ref, lq1_ref, lk1_ref, lq2_ref, lk2_ref, sw_ref, o_ref,
                      m_ref, l_ref, acc_ref, *, tq, lambda_init):
    qi = pl.program_id(2)
    q = q_ref[...]
    lane = lax.broadcasted_iota(jnp.int32, q.shape, 1)
    zero = jnp.zeros_like(q)
    q_half = (jnp.where(lane < DIFF_HALF, q, zero), jnp.where(lane >= DIFF_HALF, q, zero))

    m_ref[...] = jnp.full_like(m_ref, -jnp.inf)
    l_ref[...] = jnp.zeros_like(l_ref)
    acc_ref[...] = jnp.zeros_like(acc_ref)

    def block(j, diagonal):
        start = pl.multiple_of(j * tq, tq)
        kb = k_ref[pl.ds(start, tq), :]
        vb = v_ref[pl.ds(start, tq), :]
        for s in range(2):
            sc = _nt(q_half[s], kb)
            if diagonal:
                rq = lax.broadcasted_iota(jnp.int32, sc.shape, 0) // ATTN_CHUNK
                ck = lax.broadcasted_iota(jnp.int32, sc.shape, 1) // ATTN_CHUNK
                sc = jnp.where(ck <= rq, sc, -jnp.inf)
            m_prev = m_ref[s]
            m_new = jnp.maximum(m_prev, jnp.max(sc, axis=-1, keepdims=True))
            p = jnp.exp(sc - m_new)
            alpha = jnp.exp(m_prev - m_new)
            l_ref[s] = alpha * l_ref[s] + jnp.sum(p, axis=-1, keepdims=True)
            acc_ref[s] = alpha * acc_ref[s] + _mm(p.astype(BF16), vb)
            m_ref[s] = m_new

    def full_block(j, carry):
        block(j, False)
        return carry

    lax.fori_loop(0, qi, full_block, 0)
    block(qi, True)

    lam = (jnp.exp(jnp.sum(lq1_ref[...] * lk1_ref[...], axis=-1, keepdims=True))
           - jnp.exp(jnp.sum(lq2_ref[...] * lk2_ref[...], axis=-1, keepdims=True)) + lambda_init)
    out = acc_ref[0] / l_ref[0] - lam * (acc_ref[1] / l_ref[1])
    out = _rms(out, sw_ref[...], SUBLN_EPS) * (1.0 - lambda_init)
    o_ref[...] = out.astype(o_ref.dtype)


def _diff_attn(qkv, lq1, lk1, lq2, lk2, subln_w, *, batch, seq, lambda_init, tq=256):
    m = qkv.shape[0]
    nq = seq // tq
    const = dict(pipeline_mode=pl.Buffered(1))
    vec = lambda n: pl.BlockSpec((1, n), lambda b, h, i: (0, 0), **const)
    row = lambda a: a.reshape(1, -1)
    return pl.pallas_call(
        functools.partial(_diff_attn_kernel, tq=tq, lambda_init=lambda_init),
        grid=(batch, DIFF_HEADS, nq),
        in_specs=[
            pl.BlockSpec((tq, DIFF_VDIM), lambda b, h, i: (b * nq + i, h)),
            pl.BlockSpec((seq, DIFF_VDIM), lambda b, h, i: (b, DIFF_HEADS + h)),
            pl.BlockSpec((seq, DIFF_VDIM), lambda b, h, i: (b, 2 * DIFF_HEADS + h)),
            vec(DIFF_HALF), vec(DIFF_HALF), vec(DIFF_HALF), vec(DIFF_HALF), vec(DIFF_VDIM),
        ],
        out_specs=pl.BlockSpec((tq, DIFF_VDIM), lambda b, h, i: (b * nq + i, h)),
        out_shape=jax.ShapeDtypeStruct((m, DIFF_WIDTH), BF16),
        scratch_shapes=[
            pltpu.VMEM((2, tq, 1), F32),
            pltpu.VMEM((2, tq, 1), F32),
            pltpu.VMEM((2, tq, DIFF_VDIM), F32),
        ],
        compiler_params=pltpu.CompilerParams(
            dimension_semantics=("arbitrary", "arbitrary", "arbitrary"),
            vmem_limit_bytes=VMEM_LIMIT_BYTES),
        name="diff_attn",
    )(qkv, qkv, qkv, row(lq1), row(lk1), row(lq2), row(lk2), row(subln_w))


def _post_kernel(x_ref, yw_ref, bonus_ref, g_ref, yb_ref, lnw_ref, lnb_ref, woa_ref, wob_ref,
                 nffn_ref, wg_ref, wu_ref, wd_ref, nfin_ref, o_ref, *, tff, final_norm):
    mix = _mm(yb_ref[...], wob_ref[...])
    for h in range(RWKV_HEADS):
        ya = (yw_ref[h] * lnw_ref[h] + lnb_ref[h] + bonus_ref[h]) * g_ref[h]
        mix = mix + _mm(ya.astype(BF16), woa_ref[h])
    h1 = x_ref[...] + mix
    u = _rms(h1, nffn_ref[...], NORM_EPS).astype(BF16)
    acc = jnp.zeros_like(h1)
    for c0 in range(0, wg_ref.shape[1], tff):
        gt = _mm(u, wg_ref[:, c0:c0 + tff])
        up = _mm(u, wu_ref[:, c0:c0 + tff])
        act = (gt * _sigmoid(gt) * up).astype(BF16)
        acc = acc + _mm(act, wd_ref[c0:c0 + tff, :])
    h2 = h1 + acc
    if final_norm:
        h2 = _rms(h2, nfin_ref[...], NORM_EPS)
    o_ref[...] = h2


def _post(x2, yw, bonus, gate, yb, ln_w, ln_b, w_out_bf16, norm_ffn_w, wg, wu, wd, norm_final_w,
          *, final_norm, tm=256, tff=256):
    m, d = x2.shape
    const = dict(pipeline_mode=pl.Buffered(1))
    whole = lambda a: pl.BlockSpec(a.shape, lambda i: (0,) * a.ndim, **const)
    hm_spec = pl.BlockSpec((RWKV_HEADS, tm, RWKV_HEAD), lambda i: (0, i, 0))
    lnw = ln_w.reshape(RWKV_HEADS, 1, RWKV_HEAD)
    lnb = ln_b.reshape(RWKV_HEADS, 1, RWKV_HEAD)
    woa = w_out_bf16[:RWKV_WIDTH].reshape(RWKV_HEADS, RWKV_HEAD, d)
    wob = w_out_bf16[RWKV_WIDTH:]
    nffn = norm_ffn_w.reshape(1, d)
    nfin = norm_final_w.reshape(1, d)
    return pl.pallas_call(
        functools.partial(_post_kernel, tff=tff, final_norm=final_norm),
        grid=(m // tm,),
        in_specs=[
            pl.BlockSpec((tm, d), lambda i: (i, 0)),
            hm_spec, hm_spec, hm_spec,
            pl.BlockSpec((tm, DIFF_WIDTH), lambda i: (i, 0)),
            whole(lnw), whole(lnb), whole(woa), whole(wob), whole(nffn),
            whole(wg), whole(wu), whole(wd), whole(nfin),
        ],
        out_specs=pl.BlockSpec((tm, d), lambda i: (i, 0)),
        out_shape=jax.ShapeDtypeStruct((m, d), F32),
        compiler_params=pltpu.CompilerParams(
            dimension_semantics=("arbitrary",), vmem_limit_bytes=VMEM_LIMIT_BYTES),
        name="post",
    )(x2, yw, bonus, gate, yb, lnw, lnb, woa, wob, nffn, wg, wu, wd, nfin)


def kernel(x, norm_mix_w, w_in, mu_shift, w0, w_lora_up, a0, a_lora_up, g_lora_up, k_k, k_a, r_k,
           ln_x_w, ln_x_b, lambda_q1, lambda_k1, lambda_q2, lambda_k2, subln_w, w_out,
           norm_ffn_w, w_gate, w_up, w_down, norm_final_w):
    batch, seq, d = x.shape
    depth = w_in.shape[0]
    h = x.reshape(batch * seq, d)
    for l in range(depth):
        lambda_init = 0.8 - 0.6 * math.exp(-0.3 * l)
        w2pad = jnp.concatenate(
            [w_lora_up[l], jnp.zeros((AAA_LORA, RWKV_WIDTH), F32)], axis=0).astype(BF16)
        a2pad = jnp.concatenate(
            [jnp.zeros((DECAY_LORA, RWKV_WIDTH), F32), a_lora_up[l]], axis=0).astype(BF16)

        prw, qkv = _in_proj(h, norm_mix_w[l], w_in[l].astype(BF16))
        rt, kt, bt, at, v, gam, bonus, gate = _rwkv_prep(
            prw, mu_shift[l], w0[l], w2pad, a0[l], a2pad, g_lora_up[l].astype(BF16),
            k_k[l], k_a[l], r_k[l].reshape(-1), seq=seq)
        yw = _wkv(rt, kt, bt, at, v, gam, seq=seq)
        yb = _diff_attn(qkv, lambda_q1[l], lambda_k1[l], lambda_q2[l], lambda_k2[l], subln_w[l],
                        batch=batch, seq=seq, lambda_init=lambda_init)
        h = _post(h, yw, bonus, gate, yb, ln_x_w[l], ln_x_b[l], w_out[l].astype(BF16),
                  norm_ffn_w[l], w_gate[l].astype(BF16), w_up[l].astype(BF16),
                  w_down[l].astype(BF16), norm_final_w, final_norm=(l == depth - 1))
    return h.reshape(batch, seq, d)
```

```python
import functools
import math

import jax
import jax.numpy as jnp
from jax import lax
from jax.experimental import pallas as pl
from jax.experimental.pallas import tpu as pltpu

F32 = jnp.float32
BF16 = jnp.bfloat16

ATTN_CHUNK = 64
RWKV_WIDTH = 512
RWKV_HEAD = 64
RWKV_HEADS = RWKV_WIDTH // RWKV_HEAD
DECAY_LORA = 64
AAA_LORA = 64
GATE_LORA = 128
DIFF_WIDTH = 512
DIFF_HALF = 64
DIFF_VDIM = 2 * DIFF_HALF
DIFF_HEADS = DIFF_WIDTH // DIFF_VDIM
NORM_EPS = 1e-6
GN_EPS = 1e-5 * RWKV_HEAD
SUBLN_EPS = 1e-5
RWKV_COLS = 3 * RWKV_WIDTH + DECAY_LORA + AAA_LORA + GATE_LORA
DIFF_COLS = 3 * DIFF_WIDTH

WKV_CHUNK = 64
ATTN_TILE = 512

VMEM_LIMIT_BYTES = 56 * 1024 * 1024


def _nt(a, b):
    return lax.dot_general(a, b, (((1,), (1,)), ((), ())), preferred_element_type=F32)


def _tn(a, b):
    return lax.dot_general(a, b, (((0,), (0,)), ((), ())), preferred_element_type=F32)


def _mm(a, b):
    return jnp.dot(a, b, preferred_element_type=F32)


def _rms(x, w, eps):
    return x * lax.rsqrt(jnp.mean(x * x, axis=-1, keepdims=True) + eps) * w


def _sigmoid(x):
    return 1.0 / (1.0 + jnp.exp(-x))


def _split_dot_left(mat_bf16, x, terms):
    acc = None
    rem = x
    for _ in range(terms):
        part = rem.astype(BF16)
        d = _mm(mat_bf16, part)
        acc = d if acc is None else acc + d
        rem = rem - part.astype(F32)
    return acc


def _split_dot_right(x, mat_bf16, terms):
    acc = None
    rem = x
    for _ in range(terms):
        part = rem.astype(BF16)
        d = _mm(part, mat_bf16)
        acc = d if acc is None else acc + d
        rem = rem - part.astype(F32)
    return acc


def _in_proj_kernel(x_ref, nw_ref, w_ref, wvt_ref, prw_ref, qk_ref, vt_ref, *, tn):
    ub = _rms(x_ref[...], nw_ref[...], NORM_EPS).astype(BF16)
    for n0 in range(0, RWKV_COLS + 2 * DIFF_WIDTH, tn):
        acc = _mm(ub, w_ref[:, n0:n0 + tn])
        if n0 < RWKV_COLS:
            prw_ref[:, n0:n0 + tn] = acc
        else:
            c0 = n0 - RWKV_COLS
            if c0 < DIFF_WIDTH:
                acc = acc * (DIFF_HALF ** -0.5)
            qk_ref[:, c0:c0 + tn] = acc.astype(BF16)
    for n0 in range(0, DIFF_WIDTH, tn):
        vt_ref[0, n0:n0 + tn, :] = _nt(wvt_ref[n0:n0 + tn, :], ub).astype(BF16)


def _in_proj(x2, norm_w, w_in_bf16, *, tm, tn=256):
    m, d = x2.shape
    const = dict(pipeline_mode=pl.Buffered(1))
    w_main = w_in_bf16[:, :RWKV_COLS + 2 * DIFF_WIDTH]
    w_vt = w_in_bf16[:, RWKV_COLS + 2 * DIFF_WIDTH:].T
    return pl.pallas_call(
        functools.partial(_in_proj_kernel, tn=tn),
        grid=(m // tm,),
        in_specs=[
            pl.BlockSpec((tm, d), lambda i: (i, 0)),
            pl.BlockSpec((1, d), lambda i: (0, 0), **const),
            pl.BlockSpec(w_main.shape, lambda i: (0, 0), **const),
            pl.BlockSpec(w_vt.shape, lambda i: (0, 0), **const),
        ],
        out_specs=[
            pl.BlockSpec((tm, RWKV_COLS), lambda i: (i, 0)),
            pl.BlockSpec((tm, 2 * DIFF_WIDTH), lambda i: (i, 0)),
            pl.BlockSpec((1, DIFF_WIDTH, tm), lambda i: (i, 0, 0)),
        ],
        out_shape=[
            jax.ShapeDtypeStruct((m, RWKV_COLS), F32),
            jax.ShapeDtypeStruct((m, 2 * DIFF_WIDTH), BF16),
            jax.ShapeDtypeStruct((m // tm, DIFF_WIDTH, tm), BF16),
        ],
        compiler_params=pltpu.CompilerParams(
            dimension_semantics=("arbitrary",), vmem_limit_bytes=VMEM_LIMIT_BYTES),
        name="in_proj",
    )(x2, norm_w.reshape(1, d), w_main, w_vt)


def _rwkv_prep_kernel(p_ref, prev_ref, mu_ref, w0_ref, w2_ref, a0_ref, a2_ref, g2_ref,
                      kk_ref, ka_ref, rk_ref,
                      rt_ref, kt_ref, bt_ref, at_ref, v_ref, gam_ref, bonus_ref, g_ref,
                      *, tm, tiles_per_seq):
    i = pl.program_id(0)
    seq_start = (i % tiles_per_seq) == 0

    def shifted(c0, c1):
        p = p_ref[:, c0:c1]
        carry = jnp.where(seq_start, 0.0, prev_ref[7:8, c0:c1])
        row = lax.broadcasted_iota(jnp.int32, p.shape, 0)
        p_prev = jnp.where(row == 0, carry, pltpu.roll(p, 1, axis=0))
        return p + (p_prev - p) * mu_ref[:, c0:c1]

    w = RWKV_WIDTH
    p_r = shifted(0, w)
    p_k = shifted(w, 2 * w)
    p_v = shifted(2 * w, 3 * w)
    p_wa = shifted(3 * w, 3 * w + DECAY_LORA + AAA_LORA)
    p_g = shifted(3 * w + DECAY_LORA + AAA_LORA, RWKV_COLS)

    z = w0_ref[...] + _mm(jnp.tanh(p_wa).astype(BF16), w2_ref[...])
    softplus_neg = jnp.maximum(-z, 0.0) + jnp.log(1.0 + jnp.exp(-jnp.abs(z)))
    log_decay = -jnp.exp(-softplus_neg - 0.5)
    a_sig = _sigmoid(a0_ref[...] + _mm(p_wa.astype(BF16), a2_ref[...]))
    gate = _mm(_sigmoid(p_g).astype(BF16), g2_ref[...])

    r_i = lax.broadcasted_iota(jnp.int32, (w, w), 0) // RWKV_HEAD
    c_i = lax.broadcasted_iota(jnp.int32, (w, w), 1) // RWKV_HEAD
    head_ones = (r_i == c_i).astype(BF16)

    kk = p_k * kk_ref[...]
    kk_norm = jnp.sqrt(_split_dot_right(kk * kk, head_ones, 2))
    kk = kk / jnp.maximum(kk_norm, 1e-12)
    k_fin = p_k * (1.0 + (a_sig - 1.0) * ka_ref[...])
    bonus = _split_dot_right(p_r * k_fin * rk_ref[...], head_ones, 2) * p_v

    t_r = lax.broadcasted_iota(jnp.int32, (tm, tm), 0)
    t_c = lax.broadcasted_iota(jnp.int32, (tm, tm), 1)
    tri = ((t_r // WKV_CHUNK == t_c // WKV_CHUNK) & (t_c <= t_r)).astype(BF16)
    lg = _split_dot_left(tri, log_decay, 3)

    gam = jnp.exp(lg)
    gam_inv = jnp.exp(-lg)
    gam_prev = jnp.exp(lg - log_decay)
    rt = p_r * gam
    kt = k_fin * gam_inv
    bt = kk * a_sig * gam_inv
    at = -kk * gam_prev

    for h in range(RWKV_HEADS):
        sl = slice(h * RWKV_HEAD, (h + 1) * RWKV_HEAD)
        rt_ref[h] = rt[:, sl].astype(BF16)
        kt_ref[h] = kt[:, sl].astype(BF16)
        bt_ref[h] = bt[:, sl].astype(BF16)
        at_ref[h] = at[:, sl].astype(BF16)
        v_ref[h] = p_v[:, sl].astype(BF16)
        gam_ref[h] = gam[:, sl]
        bonus_ref[h] = bonus[:, sl]
        g_ref[h] = gate[:, sl]


def _rwkv_prep(prw, mu, w0, w2pad, a0, a2pad, g2, k_k, k_a, r_k, *, seq, tm=512):
    m = prw.shape[0]
    w = RWKV_WIDTH
    const = dict(pipeline_mode=pl.Buffered(1))
    row = lambda a: a.reshape(1, -1)
    head_major = lambda dt: jax.ShapeDtypeStruct((RWKV_HEADS, m, RWKV_HEAD), dt)
    hm_spec = pl.BlockSpec((RWKV_HEADS, tm, RWKV_HEAD), lambda i: (0, i, 0))
    vec = lambda n: pl.BlockSpec((1, n), lambda i: (0, 0), **const)
    return pl.pallas_call(
        functools.partial(_rwkv_prep_kernel, tm=tm, tiles_per_seq=seq // tm),
        grid=(m // tm,),
        in_specs=[
            pl.BlockSpec((tm, RWKV_COLS), lambda i: (i, 0)),
            pl.BlockSpec((8, RWKV_COLS), lambda i: (jnp.maximum(i * (tm // 8) - 1, 0), 0)),
            vec(RWKV_COLS), vec(w),
            pl.BlockSpec(w2pad.shape, lambda i: (0, 0), **const),
            vec(w),
            pl.BlockSpec(a2pad.shape, lambda i: (0, 0), **const),
            pl.BlockSpec(g2.shape, lambda i: (0, 0), **const),
            vec(w), vec(w), vec(w),
        ],
        out_specs=[hm_spec] * 8,
        out_shape=[head_major(BF16)] * 5 + [head_major(F32)] * 3,
        compiler_params=pltpu.CompilerParams(
            dimension_semantics=("arbitrary",), vmem_limit_bytes=VMEM_LIMIT_BYTES),
        name="rwkv_prep",
    )(prw, prw, row(mu), row(w0), w2pad, row(a0), a2pad, g2, row(k_k), row(k_a), row(r_k))


def _wkv_kernel(rt_ref, kt_ref, bt_ref, at_ref, v_ref, gam_ref, y_ref, s_ref, *, nc):
    @pl.when(pl.program_id(0) == 0)
    def _():
        s_ref[...] = jnp.zeros_like(s_ref)

    heads, batch = rt_ref.shape[0], rt_ref.shape[1]
    lanes = [(h, b) for b in range(batch) for h in range(heads)]
    n = len(lanes)
    c = WKV_CHUNK
    row = lax.broadcasted_iota(jnp.int32, (c, c), 0)
    col = lax.broadcasted_iota(jnp.int32, (c, c), 1)
    strict = row > col
    incl = row >= col
    eye = (row == col).astype(F32)
    bf = lambda xs: [x.astype(BF16) for x in xs]

    pre = []
    for ci in range(nc):
        sl = slice(ci * c, (ci + 1) * c)
        a = [at_ref[h, b, sl, :] for h, b in lanes]
        r = [rt_ref[h, b, sl, :] for h, b in lanes]
        bb = [bt_ref[h, b, sl, :] for h, b in lanes]
        k = [kt_ref[h, b, sl, :] for h, b in lanes]
        v = [v_ref[h, b, sl, :] for h, b in lanes]
        a_ab = [jnp.where(strict, _nt(a[i], bb[i]), 0.0) for i in range(n)]
        a_ak = bf([jnp.where(strict, _nt(a[i], k[i]), 0.0) for i in range(n)])
        a_rb = bf([jnp.where(incl, _nt(r[i], bb[i]), 0.0) for i in range(n)])
        a_rk = bf([jnp.where(incl, _nt(r[i], k[i]), 0.0) for i in range(n)])

        t_inv = [eye + x for x in a_ab]
        power = bf(a_ab)
        for _ in range(int(math.log2(c)) - 1):
            power = [_mm(p, p) for p in power]
            t_inv = [_mm(t.astype(BF16), (eye + p).astype(BF16)) for t, p in zip(t_inv, power)]
            power = bf(power)
        t_b = bf(t_inv)

        w_mat = bf([_mm(t_b[i], a[i]) for i in range(n)])
        av = bf([_mm(a_ak[i], v[i]) for i in range(n)])
        u_free = [_mm(t_b[i], av[i]) for i in range(n)]
        y_free = [_mm(a_rk[i], v[i]) for i in range(n)]
        kv = [_tn(v[i], k[i]) for i in range(n)]
        pre.append((sl, r, bb, a_rb, w_mat, u_free, y_free, kv))

    s = [s_ref[i] for i in range(n)]
    for ci in range(nc):
        sl, r, bb, a_rb, w_mat, u_free, y_free, kv = pre[ci]
        s_b = bf(s)
        u = bf([_nt(w_mat[i], s_b[i]) + u_free[i] for i in range(n)])
        s_new = [s[i] + _tn(u[i], bb[i]) + kv[i] for i in range(n)]
        y = [_nt(r[i], s_b[i]) + _mm(a_rb[i], u[i]) + y_free[i] for i in range(n)]
        for i, (h, b) in enumerate(lanes):
            g_last = gam_ref[h, b, ci * c + c - 1:(ci + 1) * c, :]
            s[i] = s_new[i] * g_last
            mean = jnp.mean(y[i], axis=-1, keepdims=True)
            yc = y[i] - mean
            var = jnp.mean(yc * yc, axis=-1, keepdims=True)
            y_ref[h, b, sl, :] = yc * lax.rsqrt(var + GN_EPS)
    for i in range(n):
        s_ref[i] = s[i]


def _wkv(rt, kt, bt, at, v, gam, *, batch, seq, tc=128):
    heads, m, n = rt.shape
    shape4 = (heads, batch, seq, n)
    spec = pl.BlockSpec((heads, batch, tc, n), lambda t: (0, 0, t, 0))
    out = pl.pallas_call(
        functools.partial(_wkv_kernel, nc=tc // WKV_CHUNK),
        grid=(seq // tc,),
        in_specs=[spec] * 6,
        out_specs=spec,
        out_shape=jax.ShapeDtypeStruct(shape4, F32),
        scratch_shapes=[pltpu.VMEM((heads * batch, n, n), F32)],
        compiler_params=pltpu.CompilerParams(
            dimension_semantics=("arbitrary",), vmem_limit_bytes=VMEM_LIMIT_BYTES),
        name="wkv",
    )(*[x.reshape(shape4) for x in (rt, kt, bt, at, v, gam)])
    return out.reshape(heads, m, n)


def _diff_attn_kernel(q_ref, k_ref, vt_ref, lq1_ref, lk1_ref, lq2_ref, lk2_ref, sw_ref, o_ref,
                      m_ref, l_ref, acc_ref, *, tq, lambda_init):
    qi = pl.program_id(2)
    q = q_ref[...]
    lane = lax.broadcasted_iota(jnp.int32, q.shape, 1)
    zero = jnp.zeros_like(q)
    q_half = (jnp.where(lane < DIFF_HALF, q, zero), jnp.where(lane >= DIFF_HALF, q, zero))

    m_ref[...] = jnp.full_like(m_ref, -jnp.inf)
    l_ref[...] = jnp.zeros_like(l_ref)
    acc_ref[...] = jnp.zeros_like(acc_ref)

    def block(j, diagonal):
        start = pl.multiple_of(j * tq, tq)
        kb = k_ref[pl.ds(start, tq), :]
        vtb = vt_ref[j]
        st = [_nt(kb, q_half[s]) for s in range(2)]
        if diagonal:
            ck = lax.broadcasted_iota(jnp.int32, st[0].shape, 0) // ATTN_CHUNK
            cq = lax.broadcasted_iota(jnp.int32, st[0].shape, 1) // ATTN_CHUNK
            st = [jnp.where(ck <= cq, x, -jnp.inf) for x in st]
        m_prev = [m_ref[s] for s in range(2)]
        m_new = [jnp.maximum(m_prev[s], jnp.max(st[s], axis=0, keepdims=True)) for s in range(2)]
        p = [jnp.exp(st[s] - m_new[s]) for s in range(2)]
        alpha = [jnp.exp(m_prev[s] - m_new[s]) for s in range(2)]
        for s in range(2):
            l_ref[s] = alpha[s] * l_ref[s] + jnp.sum(p[s], axis=0, keepdims=True)
            acc_ref[s] = alpha[s] * acc_ref[s] + _mm(vtb, p[s].astype(BF16))
            m_ref[s] = m_new[s]

    def full_block(j, carry):
        block(j, False)
        return carry

    lax.fori_loop(0, qi, full_block, 0)
    block(qi, True)

    lam = (jnp.exp(jnp.sum(lq1_ref[...] * lk1_ref[...], axis=-1, keepdims=True))
           - jnp.exp(jnp.sum(lq2_ref[...] * lk2_ref[...], axis=-1, keepdims=True)) + lambda_init)
    out_t = acc_ref[0] / l_ref[0] - lam * (acc_ref[1] / l_ref[1])
    out = _rms(out_t.T, sw_ref[...], SUBLN_EPS) * (1.0 - lambda_init)
    o_ref[...] = out.astype(o_ref.dtype)


def _diff_attn(qk, vt, lq1, lk1, lq2, lk2, subln_w, *, batch, seq, lambda_init):
    m = qk.shape[0]
    tq = vt.shape[2]
    nq = seq // tq
    const = dict(pipeline_mode=pl.Buffered(1))
    vec = lambda n: pl.BlockSpec((1, n), lambda b, h, i: (0, 0), **const)
    row = lambda a: a.reshape(1, -1)
    return pl.pallas_call(
        functools.partial(_diff_attn_kernel, tq=tq, lambda_init=lambda_init),
        grid=(batch, DIFF_HEADS, nq),
        in_specs=[
            pl.BlockSpec((tq, DIFF_VDIM), lambda b, h, i: (b * nq + i, h)),
            pl.BlockSpec((seq, DIFF_VDIM), lambda b, h, i: (b, DIFF_HEADS + h)),
            pl.BlockSpec((nq, DIFF_VDIM, tq), lambda b, h, i: (b, h, 0)),
            vec(DIFF_HALF), vec(DIFF_HALF), vec(DIFF_HALF), vec(DIFF_HALF), vec(DIFF_VDIM),
        ],
        out_specs=pl.BlockSpec((tq, DIFF_VDIM), lambda b, h, i: (b * nq + i, h)),
        out_shape=jax.ShapeDtypeStruct((m, DIFF_WIDTH), BF16),
        scratch_shapes=[
            pltpu.VMEM((2, 1, tq), F32),
            pltpu.VMEM((2, 1, tq), F32),
            pltpu.VMEM((2, DIFF_VDIM, tq), F32),
        ],
        compiler_params=pltpu.CompilerParams(
            dimension_semantics=("arbitrary", "arbitrary", "arbitrary"),
            vmem_limit_bytes=VMEM_LIMIT_BYTES),
        name="diff_attn",
    )(qk, qk, vt, row(lq1), row(lk1), row(lq2), row(lk2), row(subln_w))


def _post_kernel(x_ref, yw_ref, bonus_ref, g_ref, yb_ref, lnw_ref, lnb_ref, woa_ref, wob_ref,
                 nffn_ref, wg_ref, wu_ref, wd_ref, nfin_ref, o_ref, *, tff, final_norm):
    mix = _mm(yb_ref[...], wob_ref[...])
    for h in range(RWKV_HEADS):
        ya = (yw_ref[h] * lnw_ref[h] + lnb_ref[h] + bonus_ref[h]) * g_ref[h]
        mix = mix + _mm(ya.astype(BF16), woa_ref[h])
    h1 = x_ref[...] + mix
    u = _rms(h1, nffn_ref[...], NORM_EPS).astype(BF16)
    acc = jnp.zeros_like(h1)
    for c0 in range(0, wg_ref.shape[1], tff):
        gt = _mm(u, wg_ref[:, c0:c0 + tff])
        up = _mm(u, wu_ref[:, c0:c0 + tff])
        act = (gt * _sigmoid(gt) * up).astype(BF16)
        acc = acc + _mm(act, wd_ref[c0:c0 + tff, :])
    h2 = h1 + acc
    if final_norm:
        h2 = _rms(h2, nfin_ref[...], NORM_EPS)
    o_ref[...] = h2


def _post(x2, yw, bonus, gate, yb, ln_w, ln_b, w_out_bf16, norm_ffn_w, wg, wu, wd, norm_final_w,
          *, final_norm, tm=256, tff=256):
    m, d = x2.shape
    const = dict(pipeline_mode=pl.Buffered(1))
    whole = lambda a: pl.BlockSpec(a.shape, lambda i: (0,) * a.ndim, **const)
    hm_spec = pl.BlockSpec((RWKV_HEADS, tm, RWKV_HEAD), lambda i: (0, i, 0))
    lnw = ln_w.reshape(RWKV_HEADS, 1, RWKV_HEAD)
    lnb = ln_b.reshape(RWKV_HEADS, 1, RWKV_HEAD)
    woa = w_out_bf16[:RWKV_WIDTH].reshape(RWKV_HEADS, RWKV_HEAD, d)
    wob = w_out_bf16[RWKV_WIDTH:]
    nffn = norm_ffn_w.reshape(1, d)
    nfin = norm_final_w.reshape(1, d)
    return pl.pallas_call(
        functools.partial(_post_kernel, tff=tff, final_norm=final_norm),
        grid=(m // tm,),
        in_specs=[
            pl.BlockSpec((tm, d), lambda i: (i, 0)),
            hm_spec, hm_spec, hm_spec,
            pl.BlockSpec((tm, DIFF_WIDTH), lambda i: (i, 0)),
            whole(lnw), whole(lnb), whole(woa), whole(wob), whole(nffn),
            whole(wg), whole(wu), whole(wd), whole(nfin),
        ],
        out_specs=pl.BlockSpec((tm, d), lambda i: (i, 0)),
        out_shape=jax.ShapeDtypeStruct((m, d), F32),
        compiler_params=pltpu.CompilerParams(
            dimension_semantics=("arbitrary",), vmem_limit_bytes=VMEM_LIMIT_BYTES),
        name="post",
    )(x2, yw, bonus, gate, yb, lnw, lnb, woa, wob, nffn, wg, wu, wd, nfin)


def kernel(x, norm_mix_w, w_in, mu_shift, w0, w_lora_up, a0, a_lora_up, g_lora_up, k_k, k_a, r_k,
           ln_x_w, ln_x_b, lambda_q1, lambda_k1, lambda_q2, lambda_k2, subln_w, w_out,
           norm_ffn_w, w_gate, w_up, w_down, norm_final_w):
    batch, seq, d = x.shape
    depth = w_in.shape[0]
    h = x.reshape(batch * seq, d)
    for l in range(depth):
        lambda_init = 0.8 - 0.6 * math.exp(-0.3 * l)
        w2pad = jnp.concatenate(
            [w_lora_up[l], jnp.zeros((AAA_LORA, RWKV_WIDTH), F32)], axis=0).astype(BF16)
        a2pad = jnp.concatenate(
            [jnp.zeros((DECAY_LORA, RWKV_WIDTH), F32), a_lora_up[l]], axis=0).astype(BF16)

        prw, qk, vt = _in_proj(h, norm_mix_w[l], w_in[l].astype(BF16), tm=ATTN_TILE)
        rt, kt, bt, at, v, gam, bonus, gate = _rwkv_prep(
            prw, mu_shift[l], w0[l], w2pad, a0[l], a2pad, g_lora_up[l].astype(BF16),
            k_k[l], k_a[l], r_k[l].reshape(-1), seq=seq)
        yw = _wkv(rt, kt, bt, at, v, gam, batch=batch, seq=seq)
        yb = _diff_attn(qk, vt, lambda_q1[l], lambda_k1[l], lambda_q2[l], lambda_k2[l], subln_w[l],
                        batch=batch, seq=seq, lambda_init=lambda_init)
        h = _post(h, yw, bonus, gate, yb, ln_x_w[l], ln_x_b[l], w_out[l].astype(BF16),
                  norm_ffn_w[l], w_gate[l].astype(BF16), w_up[l].astype(BF16),
                  w_down[l].astype(BF16), norm_final_w, final_norm=(l == depth - 1))
    return h.reshape(batch, seq, d)
```

```python
import functools
import math

import jax
import jax.numpy as jnp
from jax import lax
from jax.experimental import pallas as pl
from jax.experimental.pallas import tpu as pltpu

F32 = jnp.float32
BF16 = jnp.bfloat16

ATTN_CHUNK = 64
RWKV_WIDTH = 512
RWKV_HEAD = 64
RWKV_HEADS = RWKV_WIDTH // RWKV_HEAD
DECAY_LORA = 64
AAA_LORA = 64
GATE_LORA = 128
DIFF_WIDTH = 512
DIFF_HALF = 64
DIFF_VDIM = 2 * DIFF_HALF
DIFF_HEADS = DIFF_WIDTH // DIFF_VDIM
NORM_EPS = 1e-6
GN_EPS = 1e-5 * RWKV_HEAD
SUBLN_EPS = 1e-5
RWKV_COLS = 3 * RWKV_WIDTH + DECAY_LORA + AAA_LORA + GATE_LORA
DIFF_COLS = 3 * DIFF_WIDTH

WKV_CHUNK = 64
ATTN_TILE = 512
BF16_SUBLANES = 16

VMEM_LIMIT_BYTES = 56 * 1024 * 1024


def _nt(a, b):
    return lax.dot_general(a, b, (((1,), (1,)), ((), ())), preferred_element_type=F32)


def _tn(a, b):
    return lax.dot_general(a, b, (((0,), (0,)), ((), ())), preferred_element_type=F32)


def _mm(a, b):
    return jnp.dot(a, b, preferred_element_type=F32)


def _rms(x, w, eps):
    return x * lax.rsqrt(jnp.mean(x * x, axis=-1, keepdims=True) + eps) * w


def _sigmoid(x):
    return 1.0 / (1.0 + jnp.exp(-x))


def _split_dot_left(mat_bf16, x, terms):
    acc = None
    rem = x
    for _ in range(terms):
        part = rem.astype(BF16)
        d = _mm(mat_bf16, part)
        acc = d if acc is None else acc + d
        rem = rem - part.astype(F32)
    return acc


def _split_dot_right(x, mat_bf16, terms):
    acc = None
    rem = x
    for _ in range(terms):
        part = rem.astype(BF16)
        d = _mm(part, mat_bf16)
        acc = d if acc is None else acc + d
        rem = rem - part.astype(F32)
    return acc


def _in_proj_kernel(x_ref, nw_ref, w_ref, wvt_ref, prw_ref, qk_ref, vt_ref, *, tn):
    ub = _rms(x_ref[...], nw_ref[...], NORM_EPS).astype(BF16)
    for n0 in range(0, RWKV_COLS + 2 * DIFF_WIDTH, tn):
        acc = _mm(ub, w_ref[:, n0:n0 + tn])
        if n0 < RWKV_COLS:
            prw_ref[:, n0:n0 + tn] = acc
        else:
            c0 = n0 - RWKV_COLS
            if c0 < DIFF_WIDTH:
                acc = acc * (DIFF_HALF ** -0.5 * math.log2(math.e))
            qk_ref[:, c0:c0 + tn] = acc.astype(BF16)
    for n0 in range(0, DIFF_WIDTH, tn):
        vt_ref[0, n0:n0 + tn, :] = _nt(wvt_ref[n0:n0 + tn, :], ub).astype(BF16)


def _in_proj(x2, norm_w, w_in_bf16, *, tm, tn=256):
    m, d = x2.shape
    const = dict(pipeline_mode=pl.Buffered(1))
    w_main = w_in_bf16[:, :RWKV_COLS + 2 * DIFF_WIDTH]
    w_vt = w_in_bf16[:, RWKV_COLS + 2 * DIFF_WIDTH:].T
    return pl.pallas_call(
        functools.partial(_in_proj_kernel, tn=tn),
        grid=(m // tm,),
        in_specs=[
            pl.BlockSpec((tm, d), lambda i: (i, 0)),
            pl.BlockSpec((1, d), lambda i: (0, 0), **const),
            pl.BlockSpec(w_main.shape, lambda i: (0, 0), **const),
            pl.BlockSpec(w_vt.shape, lambda i: (0, 0), **const),
        ],
        out_specs=[
            pl.BlockSpec((tm, RWKV_COLS), lambda i: (i, 0)),
            pl.BlockSpec((tm, 2 * DIFF_WIDTH), lambda i: (i, 0)),
            pl.BlockSpec((1, DIFF_WIDTH, tm), lambda i: (i, 0, 0)),
        ],
        out_shape=[
            jax.ShapeDtypeStruct((m, RWKV_COLS), F32),
            jax.ShapeDtypeStruct((m, 2 * DIFF_WIDTH), BF16),
            jax.ShapeDtypeStruct((m // tm, DIFF_WIDTH, tm), BF16),
        ],
        compiler_params=pltpu.CompilerParams(
            dimension_semantics=("arbitrary",), vmem_limit_bytes=VMEM_LIMIT_BYTES),
        name="in_proj",
    )(x2, norm_w.reshape(1, d), w_main, w_vt)


def _rwkv_prep_kernel(p_ref, prev_ref, mu_ref, w0_ref, w2_ref, a0_ref, a2_ref, g2_ref,
                      kk_ref, ka_ref, rk_ref,
                      rt_ref, kt_ref, bt_ref, at_ref, v_ref, gam_ref, bonus_ref, g_ref,
                      *, tm, tiles_per_seq):
    i = pl.program_id(0)
    seq_start = (i % tiles_per_seq) == 0

    def shifted(c0, c1):
        p = p_ref[:, c0:c1]
        carry = jnp.where(seq_start, 0.0, prev_ref[7:8, c0:c1])
        row = lax.broadcasted_iota(jnp.int32, p.shape, 0)
        p_prev = jnp.where(row == 0, carry, pltpu.roll(p, 1, axis=0))
        return p + (p_prev - p) * mu_ref[:, c0:c1]

    w = RWKV_WIDTH
    p_r = shifted(0, w)
    p_k = shifted(w, 2 * w)
    p_v = shifted(2 * w, 3 * w)
    p_wa = shifted(3 * w, 3 * w + DECAY_LORA + AAA_LORA)
    p_g = shifted(3 * w + DECAY_LORA + AAA_LORA, RWKV_COLS)

    z = w0_ref[...] + _mm(jnp.tanh(p_wa).astype(BF16), w2_ref[...])
    softplus_neg = jnp.maximum(-z, 0.0) + jnp.log(1.0 + jnp.exp(-jnp.abs(z)))
    log_decay = -jnp.exp(-softplus_neg - 0.5)
    a_sig = _sigmoid(a0_ref[...] + _mm(p_wa.astype(BF16), a2_ref[...]))
    gate = _mm(_sigmoid(p_g).astype(BF16), g2_ref[...])

    r_i = lax.broadcasted_iota(jnp.int32, (w, w), 0) // RWKV_HEAD
    c_i = lax.broadcasted_iota(jnp.int32, (w, w), 1) // RWKV_HEAD
    head_ones = (r_i == c_i).astype(BF16)

    kk = p_k * kk_ref[...]
    kk_norm = jnp.sqrt(_split_dot_right(kk * kk, head_ones, 2))
    kk = kk / jnp.maximum(kk_norm, 1e-12)
    k_fin = p_k * (1.0 + (a_sig - 1.0) * ka_ref[...])
    bonus = _split_dot_right(p_r * k_fin * rk_ref[...], head_ones, 2) * p_v

    t_r = lax.broadcasted_iota(jnp.int32, (tm, tm), 0)
    t_c = lax.broadcasted_iota(jnp.int32, (tm, tm), 1)
    tri = ((t_r // WKV_CHUNK == t_c // WKV_CHUNK) & (t_c <= t_r)).astype(BF16)
    lg = _split_dot_left(tri, log_decay, 3)

    gam = jnp.exp(lg)
    gam_inv = jnp.exp(-lg)
    gam_prev = jnp.exp(lg - log_decay)
    rt = p_r * gam
    kt = k_fin * gam_inv
    bt = kk * a_sig * gam_inv
    at = -kk * gam_prev

    for h in range(RWKV_HEADS):
        sl = slice(h * RWKV_HEAD, (h + 1) * RWKV_HEAD)
        rt_ref[h] = rt[:, sl].astype(BF16)
        kt_ref[h] = kt[:, sl].astype(BF16)
        bt_ref[h] = bt[:, sl].astype(BF16)
        at_ref[h] = at[:, sl].astype(BF16)
        v_ref[h] = p_v[:, sl].astype(BF16)
        gam_ref[h] = gam[:, sl]
        bonus_ref[h] = bonus[:, sl]
        g_ref[h] = gate[:, sl]


def _rwkv_prep(prw, mu, w0, w2pad, a0, a2pad, g2, k_k, k_a, r_k, *, seq, tm=512):
    m = prw.shape[0]
    w = RWKV_WIDTH
    const = dict(pipeline_mode=pl.Buffered(1))
    row = lambda a: a.reshape(1, -1)
    head_major = lambda dt: jax.ShapeDtypeStruct((RWKV_HEADS, m, RWKV_HEAD), dt)
    hm_spec = pl.BlockSpec((RWKV_HEADS, tm, RWKV_HEAD), lambda i: (0, i, 0))
    vec = lambda n: pl.BlockSpec((1, n), lambda i: (0, 0), **const)
    return pl.pallas_call(
        functools.partial(_rwkv_prep_kernel, tm=tm, tiles_per_seq=seq // tm),
        grid=(m // tm,),
        in_specs=[
            pl.BlockSpec((tm, RWKV_COLS), lambda i: (i, 0)),
            pl.BlockSpec((8, RWKV_COLS), lambda i: (jnp.maximum(i * (tm // 8) - 1, 0), 0)),
            vec(RWKV_COLS), vec(w),
            pl.BlockSpec(w2pad.shape, lambda i: (0, 0), **const),
            vec(w),
            pl.BlockSpec(a2pad.shape, lambda i: (0, 0), **const),
            pl.BlockSpec(g2.shape, lambda i: (0, 0), **const),
            vec(w), vec(w), vec(w),
        ],
        out_specs=[hm_spec] * 8,
        out_shape=[head_major(BF16)] * 5 + [head_major(F32)] * 3,
        compiler_params=pltpu.CompilerParams(
            dimension_semantics=("arbitrary",), vmem_limit_bytes=VMEM_LIMIT_BYTES),
        name="rwkv_prep",
    )(prw, prw, row(mu), row(w0), w2pad, row(a0), a2pad, g2, row(k_k), row(k_a), row(r_k))


def _wkv_kernel(rt_ref, kt_ref, bt_ref, at_ref, v_ref, gam_ref, y_ref, s_ref, *, nc):
    @pl.when(pl.program_id(0) == 0)
    def _():
        s_ref[...] = jnp.zeros_like(s_ref)

    heads, batch = rt_ref.shape[0], rt_ref.shape[1]
    lanes = [(h, b) for b in range(batch) for h in range(heads)]
    n = len(lanes)
    c = WKV_CHUNK
    row = lax.broadcasted_iota(jnp.int32, (c, c), 0)
    col = lax.broadcasted_iota(jnp.int32, (c, c), 1)
    strict = row > col
    incl = row >= col
    eye = (row == col).astype(F32)
    bf = lambda xs: [x.astype(BF16) for x in xs]

    pre = []
    for ci in range(nc):
        sl = slice(ci * c, (ci + 1) * c)
        a = [at_ref[h, b, sl, :] for h, b in lanes]
        r = [rt_ref[h, b, sl, :] for h, b in lanes]
        bb = [bt_ref[h, b, sl, :] for h, b in lanes]
        k = [kt_ref[h, b, sl, :] for h, b in lanes]
        v = [v_ref[h, b, sl, :] for h, b in lanes]
        a_ab = [jnp.where(strict, _nt(a[i], bb[i]), 0.0) for i in range(n)]
        a_ak = bf([jnp.where(strict, _nt(a[i], k[i]), 0.0) for i in range(n)])
        a_rb = bf([jnp.where(incl, _nt(r[i], bb[i]), 0.0) for i in range(n)])
        a_rk = bf([jnp.where(incl, _nt(r[i], k[i]), 0.0) for i in range(n)])

        t_inv = [eye + x for x in a_ab]
        power = bf(a_ab)
        for _ in range(int(math.log2(c)) - 1):
            power = [_mm(p, p) for p in power]
            t_inv = [_mm(t.astype(BF16), (eye + p).astype(BF16)) for t, p in zip(t_inv, power)]
            power = bf(power)
        t_b = bf(t_inv)

        w_mat = bf([_mm(t_b[i], a[i]) for i in range(n)])
        av = bf([_mm(a_ak[i], v[i]) for i in range(n)])
        u_free = [_mm(t_b[i], av[i]) for i in range(n)]
        y_free = [_mm(a_rk[i], v[i]) for i in range(n)]
        kv = [_tn(v[i], k[i]) for i in range(n)]
        pre.append((sl, r, bb, a_rb, w_mat, u_free, y_free, kv))

    s = [s_ref[i] for i in range(n)]
    for ci in range(nc):
        sl, r, bb, a_rb, w_mat, u_free, y_free, kv = pre[ci]
        s_b = bf(s)
        u = bf([_nt(w_mat[i], s_b[i]) + u_free[i] for i in range(n)])
        s_new = [s[i] + _tn(u[i], bb[i]) + kv[i] for i in range(n)]
        y = [_nt(r[i], s_b[i]) + _mm(a_rb[i], u[i]) + y_free[i] for i in range(n)]
        for i, (h, b) in enumerate(lanes):
            g_last = gam_ref[h, b, ci * c + c - 1:(ci + 1) * c, :]
            s[i] = s_new[i] * g_last
            mean = jnp.mean(y[i], axis=-1, keepdims=True)
            yc = y[i] - mean
            var = jnp.mean(yc * yc, axis=-1, keepdims=True)
            y_ref[h, b, sl, :] = yc * lax.rsqrt(var + GN_EPS)
    for i in range(n):
        s_ref[i] = s[i]


def _wkv(rt, kt, bt, at, v, gam, *, batch, seq, tc=128):
    heads, m, n = rt.shape
    shape4 = (heads, batch, seq, n)
    spec = pl.BlockSpec((heads, batch, tc, n), lambda t: (0, 0, t, 0))
    out = pl.pallas_call(
        functools.partial(_wkv_kernel, nc=tc // WKV_CHUNK),
        grid=(seq // tc,),
        in_specs=[spec] * 6,
        out_specs=spec,
        out_shape=jax.ShapeDtypeStruct(shape4, F32),
        scratch_shapes=[pltpu.VMEM((heads * batch, n, n), F32)],
        compiler_params=pltpu.CompilerParams(
            dimension_semantics=("arbitrary",), vmem_limit_bytes=VMEM_LIMIT_BYTES),
        name="wkv",
    )(*[x.reshape(shape4) for x in (rt, kt, bt, at, v, gam)])
    return out.reshape(heads, m, n)


def _diff_attn_kernel(q_ref, k_ref, vt_ref, lq1_ref, lk1_ref, lq2_ref, lk2_ref, sw_ref, o_ref,
                      m_ref, acc_ref, st_ref, *, tq, lambda_init):
    qi = pl.program_id(2)
    q = q_ref[...]
    lane = lax.broadcasted_iota(jnp.int32, q.shape, 1)
    zero = jnp.zeros_like(q)
    q_half = (jnp.where(lane < DIFF_HALF, q, zero), jnp.where(lane >= DIFF_HALF, q, zero))
    ones_rows = jnp.ones((acc_ref.shape[1] - DIFF_VDIM, tq), BF16)

    m_ref[...] = jnp.full_like(m_ref, -jnp.inf)
    acc_ref[...] = jnp.zeros_like(acc_ref)

    def scores(j, slot):
        start = pl.multiple_of(j * tq, tq)
        kb = k_ref[pl.ds(start, tq), :]
        for s in range(2):
            st_ref[slot, s] = _nt(kb, q_half[s])

    def accumulate(j, slot, diagonal):
        vtb = jnp.concatenate([vt_ref[j], ones_rows], axis=0)
        st = [st_ref[slot, s] for s in range(2)]
        if diagonal:
            ck = lax.broadcasted_iota(jnp.int32, st[0].shape, 0) // ATTN_CHUNK
            cq = lax.broadcasted_iota(jnp.int32, st[0].shape, 1) // ATTN_CHUNK
            st = [jnp.where(ck <= cq, x, -jnp.inf) for x in st]
        m_prev = [m_ref[s] for s in range(2)]
        m_new = [jnp.maximum(m_prev[s], jnp.max(st[s], axis=0, keepdims=True)) for s in range(2)]
        p = [jnp.exp2(st[s] - m_new[s]).astype(BF16) for s in range(2)]
        alpha = [jnp.exp2(m_prev[s] - m_new[s]) for s in range(2)]
        for s in range(2):
            acc_ref[s] = alpha[s] * acc_ref[s] + _mm(vtb, p[s])
            m_ref[s] = m_new[s]

    scores(0, 0)

    def block_pair(pair, carry):
        t = 2 * pair
        scores(t + 1, 1)
        accumulate(t, 0, False)
        scores(t + 2, 0)
        accumulate(t + 1, 1, False)
        return carry

    lax.fori_loop(0, qi // 2, block_pair, 0)

    @pl.when(qi % 2 == 1)
    def _():
        scores(qi, 1)
        accumulate(qi - 1, 0, False)

    accumulate(qi, qi % 2, True)

    lam = (jnp.exp(jnp.sum(lq1_ref[...] * lk1_ref[...], axis=-1, keepdims=True))
           - jnp.exp(jnp.sum(lq2_ref[...] * lk2_ref[...], axis=-1, keepdims=True)) + lambda_init)
    num = [acc_ref[s, :DIFF_VDIM, :] for s in range(2)]
    den = [acc_ref[s, DIFF_VDIM:DIFF_VDIM + 1, :] for s in range(2)]
    out_t = num[0] / den[0] - lam * (num[1] / den[1])
    out = _rms(out_t.T, sw_ref[...], SUBLN_EPS) * (1.0 - lambda_init)
    o_ref[...] = out.astype(o_ref.dtype)


def _diff_attn(qk, vt, lq1, lk1, lq2, lk2, subln_w, *, batch, seq, lambda_init):
    m = qk.shape[0]
    tq = vt.shape[2]
    nq = seq // tq
    const = dict(pipeline_mode=pl.Buffered(1))
    vec = lambda n: pl.BlockSpec((1, n), lambda b, h, i: (0, 0), **const)
    row = lambda a: a.reshape(1, -1)
    return pl.pallas_call(
        functools.partial(_diff_attn_kernel, tq=tq, lambda_init=lambda_init),
        grid=(batch, DIFF_HEADS, nq),
        in_specs=[
            pl.BlockSpec((tq, DIFF_VDIM), lambda b, h, i: (b * nq + i, h)),
            pl.BlockSpec((seq, DIFF_VDIM), lambda b, h, i: (b, DIFF_HEADS + h)),
            pl.BlockSpec((nq, DIFF_VDIM, tq), lambda b, h, i: (b, h, 0)),
            vec(DIFF_HALF), vec(DIFF_HALF), vec(DIFF_HALF), vec(DIFF_HALF), vec(DIFF_VDIM),
        ],
        out_specs=pl.BlockSpec((tq, DIFF_VDIM), lambda b, h, i: (b * nq + i, h)),
        out_shape=jax.ShapeDtypeStruct((m, DIFF_WIDTH), BF16),
        scratch_shapes=[
            pltpu.VMEM((2, 1, tq), F32),
            pltpu.VMEM((2, DIFF_VDIM + BF16_SUBLANES, tq), F32),
            pltpu.VMEM((2, 2, tq, tq), F32),
        ],
        compiler_params=pltpu.CompilerParams(
            dimension_semantics=("arbitrary", "arbitrary", "arbitrary"),
            vmem_limit_bytes=VMEM_LIMIT_BYTES),
        name="diff_attn",
    )(qk, qk, vt, row(lq1), row(lk1), row(lq2), row(lk2), row(subln_w))


def _post_kernel(x_ref, yw_ref, bonus_ref, g_ref, yb_ref, lnw_ref, lnb_ref, woa_ref, wob_ref,
                 nffn_ref, wg_ref, wu_ref, wd_ref, nfin_ref, o_ref, *, tff, final_norm):
    mix = _mm(yb_ref[...], wob_ref[...])
    for h in range(RWKV_HEADS):
        ya = (yw_ref[h] * lnw_ref[h] + lnb_ref[h] + bonus_ref[h]) * g_ref[h]
        mix = mix + _mm(ya.astype(BF16), woa_ref[h])
    h1 = x_ref[...] + mix
    u = _rms(h1, nffn_ref[...], NORM_EPS).astype(BF16)
    acc = jnp.zeros_like(h1)
    for c0 in range(0, wg_ref.shape[1], tff):
        gt = _mm(u, wg_ref[:, c0:c0 + tff])
        up = _mm(u, wu_ref[:, c0:c0 + tff])
        act = (gt * _sigmoid(gt) * up).astype(BF16)
        acc = acc + _mm(act, wd_ref[c0:c0 + tff, :])
    h2 = h1 + acc
    if final_norm:
        h2 = _rms(h2, nfin_ref[...], NORM_EPS)
    o_ref[...] = h2


def _post(x2, yw, bonus, gate, yb, ln_w, ln_b, w_out_bf16, norm_ffn_w, wg, wu, wd, norm_final_w,
          *, final_norm, tm=256, tff=256):
    m, d = x2.shape
    const = dict(pipeline_mode=pl.Buffered(1))
    whole = lambda a: pl.BlockSpec(a.shape, lambda i: (0,) * a.ndim, **const)
    hm_spec = pl.BlockSpec((RWKV_HEADS, tm, RWKV_HEAD), lambda i: (0, i, 0))
    lnw = ln_w.reshape(RWKV_HEADS, 1, RWKV_HEAD)
    lnb = ln_b.reshape(RWKV_HEADS, 1, RWKV_HEAD)
    woa = w_out_bf16[:RWKV_WIDTH].reshape(RWKV_HEADS, RWKV_HEAD, d)
    wob = w_out_bf16[RWKV_WIDTH:]
    nffn = norm_ffn_w.reshape(1, d)
    nfin = norm_final_w.reshape(1, d)
    return pl.pallas_call(
        functools.partial(_post_kernel, tff=tff, final_norm=final_norm),
        grid=(m // tm,),
        in_specs=[
            pl.BlockSpec((tm, d), lambda i: (i, 0)),
            hm_spec, hm_spec, hm_spec,
            pl.BlockSpec((tm, DIFF_WIDTH), lambda i: (i, 0)),
            whole(lnw), whole(lnb), whole(woa), whole(wob), whole(nffn),
            whole(wg), whole(wu), whole(wd), whole(nfin),
        ],
        out_specs=pl.BlockSpec((tm, d), lambda i: (i, 0)),
        out_shape=jax.ShapeDtypeStruct((m, d), F32),
        compiler_params=pltpu.CompilerParams(
            dimension_semantics=("arbitrary",), vmem_limit_bytes=VMEM_LIMIT_BYTES),
        name="post",
    )(x2, yw, bonus, gate, yb, lnw, lnb, woa, wob, nffn, wg, wu, wd, nfin)


def kernel(x, norm_mix_w, w_in, mu_shift, w0, w_lora_up, a0, a_lora_up, g_lora_up, k_k, k_a, r_k,
           ln_x_w, ln_x_b, lambda_q1, lambda_k1, lambda_q2, lambda_k2, subln_w, w_out,
           norm_ffn_w, w_gate, w_up, w_down, norm_final_w):
    batch, seq, d = x.shape
    depth = w_in.shape[0]
    h = x.reshape(batch * seq, d)
    for l in range(depth):
        lambda_init = 0.8 - 0.6 * math.exp(-0.3 * l)
        w2pad = jnp.concatenate(
            [w_lora_up[l], jnp.zeros((AAA_LORA, RWKV_WIDTH), F32)], axis=0).astype(BF16)
        a2pad = jnp.concatenate(
            [jnp.zeros((DECAY_LORA, RWKV_WIDTH), F32), a_lora_up[l]], axis=0).astype(BF16)

        prw, qk, vt = _in_proj(h, norm_mix_w[l], w_in[l].astype(BF16), tm=ATTN_TILE)
        rt, kt, bt, at, v, gam, bonus, gate = _rwkv_prep(
            prw, mu_shift[l], w0[l], w2pad, a0[l], a2pad, g_lora_up[l].astype(BF16),
            k_k[l], k_a[l], r_k[l].reshape(-1), seq=seq)
        yw = _wkv(rt, kt, bt, at, v, gam, batch=batch, seq=seq)
        yb = _diff_attn(qk, vt, lambda_q1[l], lambda_k1[l], lambda_q2[l], lambda_k2[l], subln_w[l],
                        batch=batch, seq=seq, lambda_init=lambda_init)
        h = _post(h, yw, bonus, gate, yb, ln_x_w[l], ln_x_b[l], w_out[l].astype(BF16),
                  norm_ffn_w[l], w_gate[l].astype(BF16), w_up[l].astype(BF16),
                  w_down[l].astype(BF16), norm_final_w, final_norm=(l == depth - 1))
    return h.reshape(batch, seq, d)
```

```python
import functools
import math

import jax
import jax.numpy as jnp
from jax import lax
from jax.experimental import pallas as pl
from jax.experimental.pallas import tpu as pltpu

F32 = jnp.float32
BF16 = jnp.bfloat16

ATTN_CHUNK = 64
RWKV_WIDTH = 512
RWKV_HEAD = 64
RWKV_HEADS = RWKV_WIDTH // RWKV_HEAD
DECAY_LORA = 64
AAA_LORA = 64
GATE_LORA = 128
DIFF_WIDTH = 512
DIFF_HALF = 64
DIFF_VDIM = 2 * DIFF_HALF
DIFF_HEADS = DIFF_WIDTH // DIFF_VDIM
NORM_EPS = 1e-6
GN_EPS = 1e-5 * RWKV_HEAD
SUBLN_EPS = 1e-5
RWKV_COLS = 3 * RWKV_WIDTH + DECAY_LORA + AAA_LORA + GATE_LORA
DIFF_COLS = 3 * DIFF_WIDTH

WKV_CHUNK = 64
ATTN_TILE = 512
BF16_SUBLANES = 16

VMEM_LIMIT_BYTES = 56 * 1024 * 1024


def _nt(a, b):
    return lax.dot_general(a, b, (((1,), (1,)), ((), ())), preferred_element_type=F32)


def _tn(a, b):
    return lax.dot_general(a, b, (((0,), (0,)), ((), ())), preferred_element_type=F32)


def _mm(a, b):
    return jnp.dot(a, b, preferred_element_type=F32)


def _rms(x, w, eps):
    return x * lax.rsqrt(jnp.mean(x * x, axis=-1, keepdims=True) + eps) * w


def _sigmoid(x):
    return 1.0 / (1.0 + jnp.exp(-x))


def _split_dot_left(mat_bf16, x, terms):
    acc = None
    rem = x
    for _ in range(terms):
        part = rem.astype(BF16)
        d = _mm(mat_bf16, part)
        acc = d if acc is None else acc + d
        rem = rem - part.astype(F32)
    return acc


def _split_dot_right(x, mat_bf16, terms):
    acc = None
    rem = x
    for _ in range(terms):
        part = rem.astype(BF16)
        d = _mm(part, mat_bf16)
        acc = d if acc is None else acc + d
        rem = rem - part.astype(F32)
    return acc


def _in_proj_kernel(x_ref, nw_ref, w_ref, wvt_ref, prw_ref, qk_ref, vt_ref, *, tn):
    ub = _rms(x_ref[...], nw_ref[...], NORM_EPS).astype(BF16)
    for n0 in range(0, RWKV_COLS + 2 * DIFF_WIDTH, tn):
        acc = _mm(ub, w_ref[:, n0:n0 + tn])
        if n0 < RWKV_COLS:
            prw_ref[:, n0:n0 + tn] = acc
        else:
            c0 = n0 - RWKV_COLS
            if c0 < DIFF_WIDTH:
                acc = acc * (DIFF_HALF ** -0.5 * math.log2(math.e))
            qk_ref[:, c0:c0 + tn] = acc.astype(BF16)
    for n0 in range(0, DIFF_WIDTH, tn):
        vt_ref[0, n0:n0 + tn, :] = _nt(wvt_ref[n0:n0 + tn, :], ub).astype(BF16)


def _in_proj(x2, norm_w, w_in_bf16, *, tm, tn=256):
    m, d = x2.shape
    const = dict(pipeline_mode=pl.Buffered(1))
    w_main = w_in_bf16[:, :RWKV_COLS + 2 * DIFF_WIDTH]
    w_vt = w_in_bf16[:, RWKV_COLS + 2 * DIFF_WIDTH:].T
    return pl.pallas_call(
        functools.partial(_in_proj_kernel, tn=tn),
        grid=(m // tm,),
        in_specs=[
            pl.BlockSpec((tm, d), lambda i: (i, 0)),
            pl.BlockSpec((1, d), lambda i: (0, 0), **const),
            pl.BlockSpec(w_main.shape, lambda i: (0, 0), **const),
            pl.BlockSpec(w_vt.shape, lambda i: (0, 0), **const),
        ],
        out_specs=[
            pl.BlockSpec((tm, RWKV_COLS), lambda i: (i, 0)),
            pl.BlockSpec((tm, 2 * DIFF_WIDTH), lambda i: (i, 0)),
            pl.BlockSpec((1, DIFF_WIDTH, tm), lambda i: (i, 0, 0)),
        ],
        out_shape=[
            jax.ShapeDtypeStruct((m, RWKV_COLS), F32),
            jax.ShapeDtypeStruct((m, 2 * DIFF_WIDTH), BF16),
            jax.ShapeDtypeStruct((m // tm, DIFF_WIDTH, tm), BF16),
        ],
        compiler_params=pltpu.CompilerParams(
            dimension_semantics=("arbitrary",), vmem_limit_bytes=VMEM_LIMIT_BYTES),
        name="in_proj",
    )(x2, norm_w.reshape(1, d), w_main, w_vt)


def _rwkv_prep_kernel(p_ref, prev_ref, mu_ref, w0_ref, w2_ref, a0_ref, a2_ref, g2_ref,
                      kk_ref, ka_ref, rk_ref,
                      rt_ref, kt_ref, bt_ref, at_ref, v_ref, gam_ref, bonus_ref, g_ref,
                      *, tm, tiles_per_seq):
    i = pl.program_id(0)
    seq_start = (i % tiles_per_seq) == 0

    def shifted(c0, c1):
        p = p_ref[:, c0:c1]
        carry = jnp.where(seq_start, 0.0, prev_ref[7:8, c0:c1])
        row = lax.broadcasted_iota(jnp.int32, p.shape, 0)
        p_prev = jnp.where(row == 0, carry, pltpu.roll(p, 1, axis=0))
        return p + (p_prev - p) * mu_ref[:, c0:c1]

    w = RWKV_WIDTH
    p_r = shifted(0, w)
    p_k = shifted(w, 2 * w)
    p_v = shifted(2 * w, 3 * w)
    p_wa = shifted(3 * w, 3 * w + DECAY_LORA + AAA_LORA)
    p_g = shifted(3 * w + DECAY_LORA + AAA_LORA, RWKV_COLS)

    z = w0_ref[...] + _mm(jnp.tanh(p_wa).astype(BF16), w2_ref[...])
    softplus_neg = jnp.maximum(-z, 0.0) + jnp.log(1.0 + jnp.exp(-jnp.abs(z)))
    log_decay = -jnp.exp(-softplus_neg - 0.5)
    a_sig = _sigmoid(a0_ref[...] + _mm(p_wa.astype(BF16), a2_ref[...]))
    gate = _mm(_sigmoid(p_g).astype(BF16), g2_ref[...])

    r_i = lax.broadcasted_iota(jnp.int32, (w, w), 0) // RWKV_HEAD
    c_i = lax.broadcasted_iota(jnp.int32, (w, w), 1) // RWKV_HEAD
    head_ones = (r_i == c_i).astype(BF16)

    kk = p_k * kk_ref[...]
    kk_norm = jnp.sqrt(_split_dot_right(kk * kk, head_ones, 2))
    kk = kk / jnp.maximum(kk_norm, 1e-12)
    k_fin = p_k * (1.0 + (a_sig - 1.0) * ka_ref[...])
    bonus = _split_dot_right(p_r * k_fin * rk_ref[...], head_ones, 2) * p_v

    t_r = lax.broadcasted_iota(jnp.int32, (tm, tm), 0)
    t_c = lax.broadcasted_iota(jnp.int32, (tm, tm), 1)
    tri = ((t_r // WKV_CHUNK == t_c // WKV_CHUNK) & (t_c <= t_r)).astype(BF16)
    lg = _split_dot_left(tri, log_decay, 3)

    gam = jnp.exp(lg)
    gam_inv = jnp.exp(-lg)
    gam_prev = jnp.exp(lg - log_decay)
    rt = p_r * gam
    kt = k_fin * gam_inv
    bt = kk * a_sig * gam_inv
    at = -kk * gam_prev

    for h in range(RWKV_HEADS):
        sl = slice(h * RWKV_HEAD, (h + 1) * RWKV_HEAD)
        rt_ref[h] = rt[:, sl].astype(BF16)
        kt_ref[h] = kt[:, sl].astype(BF16)
        bt_ref[h] = bt[:, sl].astype(BF16)
        at_ref[h] = at[:, sl].astype(BF16)
        v_ref[h] = p_v[:, sl].astype(BF16)
        gam_ref[h] = gam[:, sl]
    bonus_ref[...] = bonus
    g_ref[...] = gate


def _rwkv_prep(prw, mu, w0, w2pad, a0, a2pad, g2, k_k, k_a, r_k, *, seq, tm=512):
    m = prw.shape[0]
    w = RWKV_WIDTH
    const = dict(pipeline_mode=pl.Buffered(1))
    row = lambda a: a.reshape(1, -1)
    head_major = lambda dt: jax.ShapeDtypeStruct((RWKV_HEADS, m, RWKV_HEAD), dt)
    hm_spec = pl.BlockSpec((RWKV_HEADS, tm, RWKV_HEAD), lambda i: (0, i, 0))
    vec = lambda n: pl.BlockSpec((1, n), lambda i: (0, 0), **const)
    return pl.pallas_call(
        functools.partial(_rwkv_prep_kernel, tm=tm, tiles_per_seq=seq // tm),
        grid=(m // tm,),
        in_specs=[
            pl.BlockSpec((tm, RWKV_COLS), lambda i: (i, 0)),
            pl.BlockSpec((8, RWKV_COLS), lambda i: (jnp.maximum(i * (tm // 8) - 1, 0), 0)),
            vec(RWKV_COLS), vec(w),
            pl.BlockSpec(w2pad.shape, lambda i: (0, 0), **const),
            vec(w),
            pl.BlockSpec(a2pad.shape, lambda i: (0, 0), **const),
            pl.BlockSpec(g2.shape, lambda i: (0, 0), **const),
            vec(w), vec(w), vec(w),
        ],
        out_specs=[hm_spec] * 6 + [pl.BlockSpec((tm, w), lambda i: (i, 0))] * 2,
        out_shape=[head_major(BF16)] * 5 + [head_major(F32)]
        + [jax.ShapeDtypeStruct((m, w), F32)] * 2,
        compiler_params=pltpu.CompilerParams(
            dimension_semantics=("arbitrary",), vmem_limit_bytes=VMEM_LIMIT_BYTES),
        name="rwkv_prep",
    )(prw, prw, row(mu), row(w0), w2pad, row(a0), a2pad, g2, row(k_k), row(k_a), row(r_k))


def _wkv_kernel(rt_ref, kt_ref, bt_ref, at_ref, v_ref, gam_ref, y_ref, s_ref, *, nc):
    @pl.when(pl.program_id(0) == 0)
    def _():
        s_ref[...] = jnp.zeros_like(s_ref)

    heads, batch = rt_ref.shape[0], rt_ref.shape[1]
    lanes = [(h, b) for b in range(batch) for h in range(heads)]
    n = len(lanes)
    c = WKV_CHUNK
    row = lax.broadcasted_iota(jnp.int32, (c, c), 0)
    col = lax.broadcasted_iota(jnp.int32, (c, c), 1)
    strict = row > col
    incl = row >= col
    eye = (row == col).astype(F32)
    bf = lambda xs: [x.astype(BF16) for x in xs]

    pre = []
    for ci in range(nc):
        sl = slice(ci * c, (ci + 1) * c)
        a = [at_ref[h, b, sl, :] for h, b in lanes]
        r = [rt_ref[h, b, sl, :] for h, b in lanes]
        bb = [bt_ref[h, b, sl, :] for h, b in lanes]
        k = [kt_ref[h, b, sl, :] for h, b in lanes]
        v = [v_ref[h, b, sl, :] for h, b in lanes]
        a_ab = [jnp.where(strict, _nt(a[i], bb[i]), 0.0) for i in range(n)]
        a_ak = bf([jnp.where(strict, _nt(a[i], k[i]), 0.0) for i in range(n)])
        a_rb = bf([jnp.where(incl, _nt(r[i], bb[i]), 0.0) for i in range(n)])
        a_rk = bf([jnp.where(incl, _nt(r[i], k[i]), 0.0) for i in range(n)])

        t_inv = [eye + x for x in a_ab]
        power = bf(a_ab)
        for _ in range(int(math.log2(c)) - 1):
            power = [_mm(p, p) for p in power]
            t_inv = [_mm(t.astype(BF16), (eye + p).astype(BF16)) for t, p in zip(t_inv, power)]
            power = bf(power)
        t_b = bf(t_inv)

        w_mat = bf([_mm(t_b[i], a[i]) for i in range(n)])
        av = bf([_mm(a_ak[i], v[i]) for i in range(n)])
        u_free = [_mm(t_b[i], av[i]) for i in range(n)]
        y_free = [_mm(a_rk[i], v[i]) for i in range(n)]
        kv = [_tn(v[i], k[i]) for i in range(n)]
        pre.append((sl, r, bb, a_rb, w_mat, u_free, y_free, kv))

    s = [s_ref[i] for i in range(n)]
    for ci in range(nc):
        sl, r, bb, a_rb, w_mat, u_free, y_free, kv = pre[ci]
        s_b = bf(s)
        u = bf([_nt(w_mat[i], s_b[i]) + u_free[i] for i in range(n)])
        s_new = [s[i] + _tn(u[i], bb[i]) + kv[i] for i in range(n)]
        y = [_nt(r[i], s_b[i]) + _mm(a_rb[i], u[i]) + y_free[i] for i in range(n)]
        for i, (h, b) in enumerate(lanes):
            g_last = gam_ref[h, b, ci * c + c - 1:(ci + 1) * c, :]
            s[i] = s_new[i] * g_last
            mean = jnp.mean(y[i], axis=-1, keepdims=True)
            yc = y[i] - mean
            var = jnp.mean(yc * yc, axis=-1, keepdims=True)
            y_ref[b, sl, h * RWKV_HEAD:(h + 1) * RWKV_HEAD] = yc * lax.rsqrt(var + GN_EPS)
    for i in range(n):
        s_ref[i] = s[i]


def _wkv(rt, kt, bt, at, v, gam, *, batch, seq, tc=128):
    heads, m, n = rt.shape
    shape4 = (heads, batch, seq, n)
    spec = pl.BlockSpec((heads, batch, tc, n), lambda t: (0, 0, t, 0))
    out = pl.pallas_call(
        functools.partial(_wkv_kernel, nc=tc // WKV_CHUNK),
        grid=(seq // tc,),
        in_specs=[spec] * 6,
        out_specs=pl.BlockSpec((batch, tc, heads * n), lambda t: (0, t, 0)),
        out_shape=jax.ShapeDtypeStruct((batch, seq, heads * n), F32),
        scratch_shapes=[pltpu.VMEM((heads * batch, n, n), F32)],
        compiler_params=pltpu.CompilerParams(
            dimension_semantics=("arbitrary",), vmem_limit_bytes=VMEM_LIMIT_BYTES),
        name="wkv",
    )(*[x.reshape(shape4) for x in (rt, kt, bt, at, v, gam)])
    return out.reshape(m, heads * n)


def _diff_attn_kernel(q_ref, k_ref, vt_ref, lq1_ref, lk1_ref, lq2_ref, lk2_ref, sw_ref, o_ref,
                      m_ref, acc_ref, st_ref, *, tq, lambda_init):
    qi = pl.program_id(2)
    q = q_ref[...]
    lane = lax.broadcasted_iota(jnp.int32, q.shape, 1)
    zero = jnp.zeros_like(q)
    q_half = (jnp.where(lane < DIFF_HALF, q, zero), jnp.where(lane >= DIFF_HALF, q, zero))
    ones_rows = jnp.ones((acc_ref.shape[1] - DIFF_VDIM, tq), BF16)

    m_ref[...] = jnp.full_like(m_ref, -jnp.inf)
    acc_ref[...] = jnp.zeros_like(acc_ref)

    def scores(j, slot):
        start = pl.multiple_of(j * tq, tq)
        kb = k_ref[pl.ds(start, tq), :]
        for s in range(2):
            st_ref[slot, s] = _nt(kb, q_half[s])

    def accumulate(j, slot, diagonal):
        vtb = jnp.concatenate([vt_ref[j], ones_rows], axis=0)
        st = [st_ref[slot, s] for s in range(2)]
        if diagonal:
            ck = lax.broadcasted_iota(jnp.int32, st[0].shape, 0) // ATTN_CHUNK
            cq = lax.broadcasted_iota(jnp.int32, st[0].shape, 1) // ATTN_CHUNK
            st = [jnp.where(ck <= cq, x, -jnp.inf) for x in st]
        m_prev = [m_ref[s] for s in range(2)]
        m_new = [jnp.maximum(m_prev[s], jnp.max(st[s], axis=0, keepdims=True)) for s in range(2)]
        p = [jnp.exp2(st[s] - m_new[s]).astype(BF16) for s in range(2)]
        alpha = [jnp.exp2(m_prev[s] - m_new[s]) for s in range(2)]
        for s in range(2):
            acc_ref[s] = alpha[s] * acc_ref[s] + _mm(vtb, p[s])
            m_ref[s] = m_new[s]

    scores(0, 0)

    def block_pair(pair, carry):
        t = 2 * pair
        scores(t + 1, 1)
        accumulate(t, 0, False)
        scores(t + 2, 0)
        accumulate(t + 1, 1, False)
        return carry

    lax.fori_loop(0, qi // 2, block_pair, 0)

    @pl.when(qi % 2 == 1)
    def _():
        scores(qi, 1)
        accumulate(qi - 1, 0, False)

    accumulate(qi, qi % 2, True)

    lam = (jnp.exp(jnp.sum(lq1_ref[...] * lk1_ref[...], axis=-1, keepdims=True))
           - jnp.exp(jnp.sum(lq2_ref[...] * lk2_ref[...], axis=-1, keepdims=True)) + lambda_init)
    num = [acc_ref[s, :DIFF_VDIM, :] for s in range(2)]
    den = [acc_ref[s, DIFF_VDIM:DIFF_VDIM + 1, :] for s in range(2)]
    out_t = num[0] / den[0] - lam * (num[1] / den[1])
    out = _rms(out_t.T, sw_ref[...], SUBLN_EPS) * (1.0 - lambda_init)
    o_ref[...] = out.astype(o_ref.dtype)


def _diff_attn(qk, vt, lq1, lk1, lq2, lk2, subln_w, *, batch, seq, lambda_init):
    m = qk.shape[0]
    tq = vt.shape[2]
    nq = seq // tq
    const = dict(pipeline_mode=pl.Buffered(1))
    vec = lambda n: pl.BlockSpec((1, n), lambda b, h, i: (0, 0), **const)
    row = lambda a: a.reshape(1, -1)
    return pl.pallas_call(
        functools.partial(_diff_attn_kernel, tq=tq, lambda_init=lambda_init),
        grid=(batch, DIFF_HEADS, nq),
        in_specs=[
            pl.BlockSpec((tq, DIFF_VDIM), lambda b, h, i: (b * nq + i, h)),
            pl.BlockSpec((seq, DIFF_VDIM), lambda b, h, i: (b, DIFF_HEADS + h)),
            pl.BlockSpec((nq, DIFF_VDIM, tq), lambda b, h, i: (b, h, 0)),
            vec(DIFF_HALF), vec(DIFF_HALF), vec(DIFF_HALF), vec(DIFF_HALF), vec(DIFF_VDIM),
        ],
        out_specs=pl.BlockSpec((tq, DIFF_VDIM), lambda b, h, i: (b * nq + i, h)),
        out_shape=jax.ShapeDtypeStruct((m, DIFF_WIDTH), BF16),
        scratch_shapes=[
            pltpu.VMEM((2, 1, tq), F32),
            pltpu.VMEM((2, DIFF_VDIM + BF16_SUBLANES, tq), F32),
            pltpu.VMEM((2, 2, tq, tq), F32),
        ],
        compiler_params=pltpu.CompilerParams(
            dimension_semantics=("arbitrary", "arbitrary", "arbitrary"),
            vmem_limit_bytes=VMEM_LIMIT_BYTES),
        name="diff_attn",
    )(qk, qk, vt, row(lq1), row(lk1), row(lq2), row(lk2), row(subln_w))


def _post_kernel(x_ref, yw_ref, bonus_ref, g_ref, yb_ref, lnw_ref, lnb_ref, wo_ref,
                 nffn_ref, wg_ref, wu_ref, wd_ref, nfin_ref, o_ref, *, tff, final_norm):
    ya = (yw_ref[...] * lnw_ref[...] + lnb_ref[...] + bonus_ref[...]) * g_ref[...]
    mixed = jnp.concatenate([ya.astype(BF16), yb_ref[...]], axis=-1)
    h1 = x_ref[...] + _mm(mixed, wo_ref[...])
    u = _rms(h1, nffn_ref[...], NORM_EPS).astype(BF16)
    acc = jnp.zeros_like(h1)
    for c0 in range(0, wg_ref.shape[1], tff):
        gt = _mm(u, wg_ref[:, c0:c0 + tff])
        up = _mm(u, wu_ref[:, c0:c0 + tff])
        act = (gt * _sigmoid(gt) * up).astype(BF16)
        acc = acc + _mm(act, wd_ref[c0:c0 + tff, :])
    h2 = h1 + acc
    if final_norm:
        h2 = _rms(h2, nfin_ref[...], NORM_EPS)
    o_ref[...] = h2


def _post(x2, yw, bonus, gate, yb, ln_w, ln_b, w_out_bf16, norm_ffn_w, wg, wu, wd, norm_final_w,
          *, final_norm, tm=512, tff=256):
    m, d = x2.shape
    const = dict(pipeline_mode=pl.Buffered(1))
    whole = lambda a: pl.BlockSpec(a.shape, lambda i: (0,) * a.ndim, **const)
    rows = lambda n: pl.BlockSpec((tm, n), lambda i: (i, 0))
    row = lambda a: a.reshape(1, -1)
    consts = (row(ln_w), row(ln_b), w_out_bf16, row(norm_ffn_w), wg, wu, wd, row(norm_final_w))
    return pl.pallas_call(
        functools.partial(_post_kernel, tff=tff, final_norm=final_norm),
        grid=(m // tm,),
        in_specs=[rows(d), rows(RWKV_WIDTH), rows(RWKV_WIDTH), rows(RWKV_WIDTH), rows(DIFF_WIDTH)]
        + [whole(a) for a in consts],
        out_specs=rows(d),
        out_shape=jax.ShapeDtypeStruct((m, d), F32),
        compiler_params=pltpu.CompilerParams(
            dimension_semantics=("arbitrary",), vmem_limit_bytes=VMEM_LIMIT_BYTES),
        name="post",
    )(x2, yw, bonus, gate, yb, *consts)


def kernel(x, norm_mix_w, w_in, mu_shift, w0, w_lora_up, a0, a_lora_up, g_lora_up, k_k, k_a, r_k,
           ln_x_w, ln_x_b, lambda_q1, lambda_k1, lambda_q2, lambda_k2, subln_w, w_out,
           norm_ffn_w, w_gate, w_up, w_down, norm_final_w):
    batch, seq, d = x.shape
    depth = w_in.shape[0]
    h = x.reshape(batch * seq, d)
    for l in range(depth):
        lambda_init = 0.8 - 0.6 * math.exp(-0.3 * l)
        w2pad = jnp.concatenate(
            [w_lora_up[l], jnp.zeros((AAA_LORA, RWKV_WIDTH), F32)], axis=0).astype(BF16)
        a2pad = jnp.concatenate(
            [jnp.zeros((DECAY_LORA, RWKV_WIDTH), F32), a_lora_up[l]], axis=0).astype(BF16)

        prw, qk, vt = _in_proj(h, norm_mix_w[l], w_in[l].astype(BF16), tm=ATTN_TILE)
        rt, kt, bt, at, v, gam, bonus, gate = _rwkv_prep(
            prw, mu_shift[l], w0[l], w2pad, a0[l], a2pad, g_lora_up[l].astype(BF16),
            k_k[l], k_a[l], r_k[l].reshape(-1), seq=seq)
        yw = _wkv(rt, kt, bt, at, v, gam, batch=batch, seq=seq)
        yb = _diff_attn(qk, vt, lambda_q1[l], lambda_k1[l], lambda_q2[l], lambda_k2[l], subln_w[l],
                        batch=batch, seq=seq, lambda_init=lambda_init)
        h = _post(h, yw, bonus, gate, yb, ln_x_w[l], ln_x_b[l], w_out[l].astype(BF16),
                  norm_ffn_w[l], w_gate[l].astype(BF16), w_up[l].astype(BF16),
                  w_down[l].astype(BF16), norm_final_w, final_norm=(l == depth - 1))
    return h.reshape(batch, seq, d)
```

```python
import functools
import math

import jax
import jax.numpy as jnp
from jax import lax
from jax.experimental import pallas as pl
from jax.experimental.pallas import tpu as pltpu

F32 = jnp.float32
BF16 = jnp.bfloat16

ATTN_CHUNK = 64
RWKV_WIDTH = 512
RWKV_HEAD = 64
RWKV_HEADS = RWKV_WIDTH // RWKV_HEAD
DECAY_LORA = 64
AAA_LORA = 64
GATE_LORA = 128
DIFF_WIDTH = 512
DIFF_HALF = 64
DIFF_VDIM = 2 * DIFF_HALF
DIFF_HEADS = DIFF_WIDTH // DIFF_VDIM
NORM_EPS = 1e-6
GN_EPS = 1e-5 * RWKV_HEAD
SUBLN_EPS = 1e-5
RWKV_COLS = 3 * RWKV_WIDTH + DECAY_LORA + AAA_LORA + GATE_LORA
DIFF_COLS = 3 * DIFF_WIDTH

WKV_CHUNK = 64
ATTN_TILE = 512
BF16_SUBLANES = 16

VMEM_LIMIT_BYTES = 56 * 1024 * 1024


def _nt(a, b):
    return lax.dot_general(a, b, (((1,), (1,)), ((), ())), preferred_element_type=F32)


def _tn(a, b):
    return lax.dot_general(a, b, (((0,), (0,)), ((), ())), preferred_element_type=F32)


def _mm(a, b):
    return jnp.dot(a, b, preferred_element_type=F32)


def _rms(x, w, eps):
    return x * lax.rsqrt(jnp.mean(x * x, axis=-1, keepdims=True) + eps) * w


def _sigmoid(x):
    return 1.0 / (1.0 + jnp.exp(-x))


def _split_dot_left(mat_bf16, x, terms):
    acc = None
    rem = x
    for _ in range(terms):
        part = rem.astype(BF16)
        d = _mm(mat_bf16, part)
        acc = d if acc is None else acc + d
        rem = rem - part.astype(F32)
    return acc


def _split_dot_right(x, mat_bf16, terms):
    acc = None
    rem = x
    for _ in range(terms):
        part = rem.astype(BF16)
        d = _mm(part, mat_bf16)
        acc = d if acc is None else acc + d
        rem = rem - part.astype(F32)
    return acc


def _in_proj_kernel(x_ref, nw_ref, w_ref, wvt_ref, prw_ref, qk_ref, vt_ref, *, tn):
    ub = _rms(x_ref[...], nw_ref[...], NORM_EPS).astype(BF16)
    for n0 in range(0, RWKV_COLS + 2 * DIFF_WIDTH, tn):
        acc = _mm(ub, w_ref[:, n0:n0 + tn])
        if n0 < RWKV_COLS:
            prw_ref[:, n0:n0 + tn] = acc
        else:
            c0 = n0 - RWKV_COLS
            if c0 < DIFF_WIDTH:
                acc = acc * (DIFF_HALF ** -0.5 * math.log2(math.e))
            qk_ref[:, c0:c0 + tn] = acc.astype(BF16)
    for n0 in range(0, DIFF_WIDTH, tn):
        vt_ref[0, n0:n0 + tn, :] = _nt(wvt_ref[n0:n0 + tn, :], ub).astype(BF16)


def _in_proj(x2, norm_w, w_in_bf16, *, tm, tn=256):
    m, d = x2.shape
    const = dict(pipeline_mode=pl.Buffered(1))
    w_main = w_in_bf16[:, :RWKV_COLS + 2 * DIFF_WIDTH]
    w_vt = w_in_bf16[:, RWKV_COLS + 2 * DIFF_WIDTH:].T
    return pl.pallas_call(
        functools.partial(_in_proj_kernel, tn=tn),
        grid=(m // tm,),
        in_specs=[
            pl.BlockSpec((tm, d), lambda i: (i, 0)),
            pl.BlockSpec((1, d), lambda i: (0, 0), **const),
            pl.BlockSpec(w_main.shape, lambda i: (0, 0), **const),
            pl.BlockSpec(w_vt.shape, lambda i: (0, 0), **const),
        ],
        out_specs=[
            pl.BlockSpec((tm, RWKV_COLS), lambda i: (i, 0)),
            pl.BlockSpec((tm, 2 * DIFF_WIDTH), lambda i: (i, 0)),
            pl.BlockSpec((1, DIFF_WIDTH, tm), lambda i: (i, 0, 0)),
        ],
        out_shape=[
            jax.ShapeDtypeStruct((m, RWKV_COLS), F32),
            jax.ShapeDtypeStruct((m, 2 * DIFF_WIDTH), BF16),
            jax.ShapeDtypeStruct((m // tm, DIFF_WIDTH, tm), BF16),
        ],
        compiler_params=pltpu.CompilerParams(
            dimension_semantics=("arbitrary",), vmem_limit_bytes=VMEM_LIMIT_BYTES),
        name="in_proj",
    )(x2, norm_w.reshape(1, d), w_main, w_vt)


def _rwkv_prep_kernel(p_ref, prev_ref, mu_ref, w0_ref, w2_ref, a0_ref, a2_ref, g2_ref,
                      kk_ref, ka_ref, rk_ref,
                      rt_ref, kt_ref, bt_ref, at_ref, v_ref, gam_ref, bonus_ref, g_ref,
                      *, tm, tiles_per_seq):
    i = pl.program_id(0)
    seq_start = (i % tiles_per_seq) == 0

    def shifted(c0, c1):
        p = p_ref[:, c0:c1]
        carry = jnp.where(seq_start, 0.0, prev_ref[7:8, c0:c1])
        row = lax.broadcasted_iota(jnp.int32, p.shape, 0)
        p_prev = jnp.where(row == 0, carry, pltpu.roll(p, 1, axis=0))
        return p + (p_prev - p) * mu_ref[:, c0:c1]

    w = RWKV_WIDTH
    p_r = shifted(0, w)
    p_k = shifted(w, 2 * w)
    p_v = shifted(2 * w, 3 * w)
    p_wa = shifted(3 * w, 3 * w + DECAY_LORA + AAA_LORA)
    p_g = shifted(3 * w + DECAY_LORA + AAA_LORA, RWKV_COLS)

    z = w0_ref[...] + _mm(jnp.tanh(p_wa).astype(BF16), w2_ref[...])
    softplus_neg = jnp.maximum(-z, 0.0) + jnp.log(1.0 + jnp.exp(-jnp.abs(z)))
    log_decay = -jnp.exp(-softplus_neg - 0.5)
    a_sig = _sigmoid(a0_ref[...] + _mm(p_wa.astype(BF16), a2_ref[...]))
    gate = _mm(_sigmoid(p_g).astype(BF16), g2_ref[...])

    r_i = lax.broadcasted_iota(jnp.int32, (w, w), 0) // RWKV_HEAD
    c_i = lax.broadcasted_iota(jnp.int32, (w, w), 1) // RWKV_HEAD
    head_ones = (r_i == c_i).astype(BF16)

    kk = p_k * kk_ref[...]
    kk_norm = jnp.sqrt(_split_dot_right(kk * kk, head_ones, 2))
    kk = kk / jnp.maximum(kk_norm, 1e-12)
    k_fin = p_k * (1.0 + (a_sig - 1.0) * ka_ref[...])
    bonus = _split_dot_right(p_r * k_fin * rk_ref[...], head_ones, 2) * p_v

    t_r = lax.broadcasted_iota(jnp.int32, (tm, tm), 0)
    t_c = lax.broadcasted_iota(jnp.int32, (tm, tm), 1)
    tri = ((t_r // WKV_CHUNK == t_c // WKV_CHUNK) & (t_c <= t_r)).astype(BF16)
    lg = _split_dot_left(tri, log_decay, 3)

    gam = jnp.exp(lg)
    gam_inv = jnp.exp(-lg)
    gam_prev = jnp.exp(lg - log_decay)
    rt = p_r * gam
    kt = k_fin * gam_inv
    bt = kk * a_sig * gam_inv
    at = -kk * gam_prev

    for h in range(RWKV_HEADS):
        sl = slice(h * RWKV_HEAD, (h + 1) * RWKV_HEAD)
        rt_ref[h] = rt[:, sl].astype(BF16)
        kt_ref[h] = kt[:, sl].astype(BF16)
        bt_ref[h] = bt[:, sl].astype(BF16)
        at_ref[h] = at[:, sl].astype(BF16)
        v_ref[h] = p_v[:, sl].astype(BF16)
        gam_ref[h] = gam[:, sl]
    bonus_ref[...] = bonus
    g_ref[...] = gate


def _rwkv_prep(prw, mu, w0, w2pad, a0, a2pad, g2, k_k, k_a, r_k, *, seq, tm=512):
    m = prw.shape[0]
    w = RWKV_WIDTH
    const = dict(pipeline_mode=pl.Buffered(1))
    row = lambda a: a.reshape(1, -1)
    head_major = lambda dt: jax.ShapeDtypeStruct((RWKV_HEADS, m, RWKV_HEAD), dt)
    hm_spec = pl.BlockSpec((RWKV_HEADS, tm, RWKV_HEAD), lambda i: (0, i, 0))
    vec = lambda n: pl.BlockSpec((1, n), lambda i: (0, 0), **const)
    return pl.pallas_call(
        functools.partial(_rwkv_prep_kernel, tm=tm, tiles_per_seq=seq // tm),
        grid=(m // tm,),
        in_specs=[
            pl.BlockSpec((tm, RWKV_COLS), lambda i: (i, 0)),
            pl.BlockSpec((8, RWKV_COLS), lambda i: (jnp.maximum(i * (tm // 8) - 1, 0), 0)),
            vec(RWKV_COLS), vec(w),
            pl.BlockSpec(w2pad.shape, lambda i: (0, 0), **const),
            vec(w),
            pl.BlockSpec(a2pad.shape, lambda i: (0, 0), **const),
            pl.BlockSpec(g2.shape, lambda i: (0, 0), **const),
            vec(w), vec(w), vec(w),
        ],
        out_specs=[hm_spec] * 6 + [pl.BlockSpec((tm, w), lambda i: (i, 0))] * 2,
        out_shape=[head_major(BF16)] * 5 + [head_major(F32)]
        + [jax.ShapeDtypeStruct((m, w), F32)] * 2,
        compiler_params=pltpu.CompilerParams(
            dimension_semantics=("arbitrary",), vmem_limit_bytes=VMEM_LIMIT_BYTES),
        name="rwkv_prep",
    )(prw, prw, row(mu), row(w0), w2pad, row(a0), a2pad, g2, row(k_k), row(k_a), row(r_k))


def _wkv_kernel(rt_ref, kt_ref, bt_ref, at_ref, v_ref, gam_ref, y_ref, s_ref, *, nc):
    @pl.when(pl.program_id(0) == 0)
    def _():
        s_ref[...] = jnp.zeros_like(s_ref)

    heads, batch = rt_ref.shape[0], rt_ref.shape[1]
    lanes = [(h, b) for b in range(batch) for h in range(heads)]
    n = len(lanes)
    c = WKV_CHUNK
    row = lax.broadcasted_iota(jnp.int32, (c, c), 0)
    col = lax.broadcasted_iota(jnp.int32, (c, c), 1)
    strict = row > col
    incl = row >= col
    eye = (row == col).astype(F32)
    bf = lambda xs: [x.astype(BF16) for x in xs]

    pre = []
    for ci in range(nc):
        sl = slice(ci * c, (ci + 1) * c)
        a = [at_ref[h, b, sl, :] for h, b in lanes]
        r = [rt_ref[h, b, sl, :] for h, b in lanes]
        bb = [bt_ref[h, b, sl, :] for h, b in lanes]
        k = [kt_ref[h, b, sl, :] for h, b in lanes]
        v = [v_ref[h, b, sl, :] for h, b in lanes]
        a_ab = [jnp.where(strict, _nt(a[i], bb[i]), 0.0) for i in range(n)]
        a_ak = bf([jnp.where(strict, _nt(a[i], k[i]), 0.0) for i in range(n)])
        a_rb = bf([jnp.where(incl, _nt(r[i], bb[i]), 0.0) for i in range(n)])
        a_rk = bf([jnp.where(incl, _nt(r[i], k[i]), 0.0) for i in range(n)])

        t_inv = [eye + x for x in a_ab]
        power = bf(a_ab)
        for _ in range(int(math.log2(c)) - 1):
            power = [_mm(p, p) for p in power]
            t_inv = [_mm(t.astype(BF16), (eye + p).astype(BF16)) for t, p in zip(t_inv, power)]
            power = bf(power)
        t_b = bf(t_inv)

        w_mat = bf([_mm(t_b[i], a[i]) for i in range(n)])
        av = bf([_mm(a_ak[i], v[i]) for i in range(n)])
        u_free = [_mm(t_b[i], av[i]) for i in range(n)]
        y_free = [_mm(a_rk[i], v[i]) for i in range(n)]
        kv = [_tn(v[i], k[i]) for i in range(n)]
        pre.append((sl, r, bb, a_rb, w_mat, u_free, y_free, kv))

    s = [s_ref[i] for i in range(n)]
    for ci in range(nc):
        sl, r, bb, a_rb, w_mat, u_free, y_free, kv = pre[ci]
        s_b = bf(s)
        u = bf([_nt(w_mat[i], s_b[i]) + u_free[i] for i in range(n)])
        s_new = [s[i] + _tn(u[i], bb[i]) + kv[i] for i in range(n)]
        y = [_nt(r[i], s_b[i]) + _mm(a_rb[i], u[i]) + y_free[i] for i in range(n)]
        for i, (h, b) in enumerate(lanes):
            g_last = gam_ref[h, b, ci * c + c - 1:(ci + 1) * c, :]
            s[i] = s_new[i] * g_last
            mean = jnp.mean(y[i], axis=-1, keepdims=True)
            yc = y[i] - mean
            var = jnp.mean(yc * yc, axis=-1, keepdims=True)
            y_ref[b, sl, h * RWKV_HEAD:(h + 1) * RWKV_HEAD] = yc * lax.rsqrt(var + GN_EPS)
    for i in range(n):
        s_ref[i] = s[i]


def _wkv(rt, kt, bt, at, v, gam, *, batch, seq, tc=128):
    heads, m, n = rt.shape
    shape4 = (heads, batch, seq, n)
    spec = pl.BlockSpec((heads, batch, tc, n), lambda t: (0, 0, t, 0))
    out = pl.pallas_call(
        functools.partial(_wkv_kernel, nc=tc // WKV_CHUNK),
        grid=(seq // tc,),
        in_specs=[spec] * 6,
        out_specs=pl.BlockSpec((batch, tc, heads * n), lambda t: (0, t, 0)),
        out_shape=jax.ShapeDtypeStruct((batch, seq, heads * n), F32),
        scratch_shapes=[pltpu.VMEM((heads * batch, n, n), F32)],
        compiler_params=pltpu.CompilerParams(
            dimension_semantics=("arbitrary",), vmem_limit_bytes=VMEM_LIMIT_BYTES),
        name="wkv",
    )(*[x.reshape(shape4) for x in (rt, kt, bt, at, v, gam)])
    return out.reshape(m, heads * n)


def _diff_attn_kernel(q_ref, k_ref, vt_ref, lq1_ref, lk1_ref, lq2_ref, lk2_ref, sw_ref, o_ref,
                      m_ref, acc_ref, st_ref, bmax_ref, *, tq, lambda_init):
    qi = pl.program_id(2)
    q = q_ref[...]
    lane = lax.broadcasted_iota(jnp.int32, q.shape, 1)
    zero = jnp.zeros_like(q)
    q_half = (jnp.where(lane < DIFF_HALF, q, zero), jnp.where(lane >= DIFF_HALF, q, zero))
    ones_rows = jnp.ones((acc_ref.shape[1] - DIFF_VDIM, tq), BF16)

    m_ref[...] = jnp.full_like(m_ref, -jnp.inf)
    acc_ref[...] = jnp.zeros_like(acc_ref)

    def scores(j, slot):
        start = pl.multiple_of(j * tq, tq)
        kb = k_ref[pl.ds(start, tq), :]
        for s in range(2):
            st = _nt(kb, q_half[s])
            st_ref[slot, s] = st
            bmax_ref[slot, s] = jnp.max(st, axis=0, keepdims=True)

    def accumulate(j, slot, diagonal):
        vtb = jnp.concatenate([vt_ref[j], ones_rows], axis=0)
        st = [st_ref[slot, s] for s in range(2)]
        if diagonal:
            ck = lax.broadcasted_iota(jnp.int32, st[0].shape, 0) // ATTN_CHUNK
            cq = lax.broadcasted_iota(jnp.int32, st[0].shape, 1) // ATTN_CHUNK
            st = [jnp.where(ck <= cq, x, -jnp.inf) for x in st]
            bmax = [jnp.max(x, axis=0, keepdims=True) for x in st]
        else:
            bmax = [bmax_ref[slot, s] for s in range(2)]
        m_prev = [m_ref[s] for s in range(2)]
        m_new = [jnp.maximum(m_prev[s], bmax[s]) for s in range(2)]
        p = [jnp.exp2(st[s] - m_new[s]).astype(BF16) for s in range(2)]
        alpha = [jnp.exp2(m_prev[s] - m_new[s]) for s in range(2)]
        for s in range(2):
            acc_ref[s] = alpha[s] * acc_ref[s] + _mm(vtb, p[s])
            m_ref[s] = m_new[s]

    scores(0, 0)

    def block_pair(pair, carry):
        t = 2 * pair
        scores(t + 1, 1)
        accumulate(t, 0, False)
        scores(t + 2, 0)
        accumulate(t + 1, 1, False)
        return carry

    lax.fori_loop(0, qi // 2, block_pair, 0)

    @pl.when(qi % 2 == 1)
    def _():
        scores(qi, 1)
        accumulate(qi - 1, 0, False)

    accumulate(qi, qi % 2, True)

    lam = (jnp.exp(jnp.sum(lq1_ref[...] * lk1_ref[...], axis=-1, keepdims=True))
           - jnp.exp(jnp.sum(lq2_ref[...] * lk2_ref[...], axis=-1, keepdims=True)) + lambda_init)
    num = [acc_ref[s, :DIFF_VDIM, :] for s in range(2)]
    den = [acc_ref[s, DIFF_VDIM:DIFF_VDIM + 1, :] for s in range(2)]
    out_t = num[0] / den[0] - lam * (num[1] / den[1])
    out = _rms(out_t.T, sw_ref[...], SUBLN_EPS) * (1.0 - lambda_init)
    o_ref[...] = out.astype(o_ref.dtype)


def _diff_attn(qk, vt, lq1, lk1, lq2, lk2, subln_w, *, batch, seq, lambda_init):
    m = qk.shape[0]
    tq = vt.shape[2]
    nq = seq // tq
    const = dict(pipeline_mode=pl.Buffered(1))
    vec = lambda n: pl.BlockSpec((1, n), lambda b, h, i: (0, 0), **const)
    row = lambda a: a.reshape(1, -1)
    return pl.pallas_call(
        functools.partial(_diff_attn_kernel, tq=tq, lambda_init=lambda_init),
        grid=(batch, DIFF_HEADS, nq),
        in_specs=[
            pl.BlockSpec((tq, DIFF_VDIM), lambda b, h, i: (b * nq + i, h)),
            pl.BlockSpec((seq, DIFF_VDIM), lambda b, h, i: (b, DIFF_HEADS + h)),
            pl.BlockSpec((nq, DIFF_VDIM, tq), lambda b, h, i: (b, h, 0)),
            vec(DIFF_HALF), vec(DIFF_HALF), vec(DIFF_HALF), vec(DIFF_HALF), vec(DIFF_VDIM),
        ],
        out_specs=pl.BlockSpec((tq, DIFF_VDIM), lambda b, h, i: (b * nq + i, h)),
        out_shape=jax.ShapeDtypeStruct((m, DIFF_WIDTH), BF16),
        scratch_shapes=[
            pltpu.VMEM((2, 1, tq), F32),
            pltpu.VMEM((2, DIFF_VDIM + BF16_SUBLANES, tq), F32),
            pltpu.VMEM((2, 2, tq, tq), F32),
            pltpu.VMEM((2, 2, 1, tq), F32),
        ],
        compiler_params=pltpu.CompilerParams(
            dimension_semantics=("arbitrary", "arbitrary", "arbitrary"),
            vmem_limit_bytes=VMEM_LIMIT_BYTES),
        name="diff_attn",
    )(qk, qk, vt, row(lq1), row(lk1), row(lq2), row(lk2), row(subln_w))


def _post_kernel(x_ref, yw_ref, bonus_ref, g_ref, yb_ref, lnw_ref, lnb_ref, wo_ref,
                 nffn_ref, wg_ref, wu_ref, wd_ref, nfin_ref, o_ref, *, tff, final_norm):
    ya = (yw_ref[...] * lnw_ref[...] + lnb_ref[...] + bonus_ref[...]) * g_ref[...]
    mixed = jnp.concatenate([ya.astype(BF16), yb_ref[...]], axis=-1)
    h1 = x_ref[...] + _mm(mixed, wo_ref[...])
    u = _rms(h1, nffn_ref[...], NORM_EPS).astype(BF16)
    acc = jnp.zeros_like(h1)
    for c0 in range(0, wg_ref.shape[1], tff):
        gt = _mm(u, wg_ref[:, c0:c0 + tff])
        up = _mm(u, wu_ref[:, c0:c0 + tff])
        act = (gt * _sigmoid(gt) * up).astype(BF16)
        acc = acc + _mm(act, wd_ref[c0:c0 + tff, :])
    h2 = h1 + acc
    if final_norm:
        h2 = _rms(h2, nfin_ref[...], NORM_EPS)
    o_ref[...] = h2


def _post(x2, yw, bonus, gate, yb, ln_w, ln_b, w_out_bf16, norm_ffn_w, wg, wu, wd, norm_final_w,
          *, final_norm, tm=512, tff=256):
    m, d = x2.shape
    const = dict(pipeline_mode=pl.Buffered(1))
    whole = lambda a: pl.BlockSpec(a.shape, lambda i: (0,) * a.ndim, **const)
    rows = lambda n: pl.BlockSpec((tm, n), lambda i: (i, 0))
    row = lambda a: a.reshape(1, -1)
    consts = (row(ln_w), row(ln_b), w_out_bf16, row(norm_ffn_w), wg, wu, wd, row(norm_final_w))
    return pl.pallas_call(
        functools.partial(_post_kernel, tff=tff, final_norm=final_norm),
        grid=(m // tm,),
        in_specs=[rows(d), rows(RWKV_WIDTH), rows(RWKV_WIDTH), rows(RWKV_WIDTH), rows(DIFF_WIDTH)]
        + [whole(a) for a in consts],
        out_specs=rows(d),
        out_shape=jax.ShapeDtypeStruct((m, d), F32),
        compiler_params=pltpu.CompilerParams(
            dimension_semantics=("arbitrary",), vmem_limit_bytes=VMEM_LIMIT_BYTES),
        name="post",
    )(x2, yw, bonus, gate, yb, *consts)


def kernel(x, norm_mix_w, w_in, mu_shift, w0, w_lora_up, a0, a_lora_up, g_lora_up, k_k, k_a, r_k,
           ln_x_w, ln_x_b, lambda_q1, lambda_k1, lambda_q2, lambda_k2, subln_w, w_out,
           norm_ffn_w, w_gate, w_up, w_down, norm_final_w):
    batch, seq, d = x.shape
    depth = w_in.shape[0]
    h = x.reshape(batch * seq, d)
    for l in range(depth):
        lambda_init = 0.8 - 0.6 * math.exp(-0.3 * l)
        w2pad = jnp.concatenate(
            [w_lora_up[l], jnp.zeros((AAA_LORA, RWKV_WIDTH), F32)], axis=0).astype(BF16)
        a2pad = jnp.concatenate(
            [jnp.zeros((DECAY_LORA, RWKV_WIDTH), F32), a_lora_up[l]], axis=0).astype(BF16)

        prw, qk, vt = _in_proj(h, norm_mix_w[l], w_in[l].astype(BF16), tm=ATTN_TILE)
        rt, kt, bt, at, v, gam, bonus, gate = _rwkv_prep(
            prw, mu_shift[l], w0[l], w2pad, a0[l], a2pad, g_lora_up[l].astype(BF16),
            k_k[l], k_a[l], r_k[l].reshape(-1), seq=seq)
        yw = _wkv(rt, kt, bt, at, v, gam, batch=batch, seq=seq)
        yb = _diff_attn(qk, vt, lambda_q1[l], lambda_k1[l], lambda_q2[l], lambda_k2[l], subln_w[l],
                        batch=batch, seq=seq, lambda_init=lambda_init)
        h = _post(h, yw, bonus, gate, yb, ln_x_w[l], ln_x_b[l], w_out[l].astype(BF16),
                  norm_ffn_w[l], w_gate[l].astype(BF16), w_up[l].astype(BF16),
                  w_down[l].astype(BF16), norm_final_w, final_norm=(l == depth - 1))
    return h.reshape(batch, seq, d)
```

```python
import functools
import math

import jax
import jax.numpy as jnp
from jax import lax
from jax.experimental import pallas as pl
from jax.experimental.pallas import tpu as pltpu

F32 = jnp.float32
BF16 = jnp.bfloat16

ATTN_CHUNK = 64
RWKV_WIDTH = 512
RWKV_HEAD = 64
RWKV_HEADS = RWKV_WIDTH // RWKV_HEAD
DECAY_LORA = 64
AAA_LORA = 64
GATE_LORA = 128
DIFF_WIDTH = 512
DIFF_HALF = 64
DIFF_VDIM = 2 * DIFF_HALF
DIFF_HEADS = DIFF_WIDTH // DIFF_VDIM
NORM_EPS = 1e-6
GN_EPS = 1e-5 * RWKV_HEAD
SUBLN_EPS = 1e-5
RWKV_COLS = 3 * RWKV_WIDTH + DECAY_LORA + AAA_LORA + GATE_LORA
DIFF_COLS = 3 * DIFF_WIDTH

WKV_CHUNK = 64
ATTN_TILE = 512
BF16_SUBLANES = 16

VMEM_LIMIT_BYTES = 56 * 1024 * 1024


def _nt(a, b):
    return lax.dot_general(a, b, (((1,), (1,)), ((), ())), preferred_element_type=F32)


def _tn(a, b):
    return lax.dot_general(a, b, (((0,), (0,)), ((), ())), preferred_element_type=F32)


def _mm(a, b):
    return jnp.dot(a, b, preferred_element_type=F32)


def _rms(x, w, eps):
    return x * lax.rsqrt(jnp.mean(x * x, axis=-1, keepdims=True) + eps) * w


def _sigmoid(x):
    return 1.0 / (1.0 + jnp.exp(-x))


def _split_dot_left(mat_bf16, x, terms):
    acc = None
    rem = x
    for _ in range(terms):
        part = rem.astype(BF16)
        d = _mm(mat_bf16, part)
        acc = d if acc is None else acc + d
        rem = rem - part.astype(F32)
    return acc


def _split_dot_right(x, mat_bf16, terms):
    acc = None
    rem = x
    for _ in range(terms):
        part = rem.astype(BF16)
        d = _mm(part, mat_bf16)
        acc = d if acc is None else acc + d
        rem = rem - part.astype(F32)
    return acc


def _mix_in_kernel(x_ref, nw_ref, w_ref, wvt_ref, mu_ref, w0_ref, w2_ref, a0_ref, a2_ref, g2_ref,
                   kk_ref, ka_ref, rk_ref,
                   qk_ref, vt_ref, rt_ref, kt_ref, bt_ref, at_ref, v_ref, gam_ref, bonus_ref, g_ref,
                   carry_ref, *, tm, tiles_per_seq):
    i = pl.program_id(0)
    seq_start = (i % tiles_per_seq) == 0
    ub = _rms(x_ref[...], nw_ref[...], NORM_EPS).astype(BF16)
    w = RWKV_WIDTH

    def shifted(c0, c1):
        p = _mm(ub, w_ref[:, c0:c1])
        carry = jnp.where(seq_start, 0.0, carry_ref[:, c0:c1])
        carry_ref[:, c0:c1] = p[tm - 1:tm, :]
        row = lax.broadcasted_iota(jnp.int32, p.shape, 0)
        p_prev = jnp.where(row == 0, carry, pltpu.roll(p, 1, axis=0))
        return p + (p_prev - p) * mu_ref[:, c0:c1]

    p_r = shifted(0, w)
    p_k = shifted(w, 2 * w)
    p_v = shifted(2 * w, 3 * w)
    p_wa = shifted(3 * w, 3 * w + DECAY_LORA + AAA_LORA)
    p_g = shifted(3 * w + DECAY_LORA + AAA_LORA, RWKV_COLS)

    q = _mm(ub, w_ref[:, RWKV_COLS:RWKV_COLS + DIFF_WIDTH])
    qk_ref[:, :DIFF_WIDTH] = (q * (DIFF_HALF ** -0.5 * math.log2(math.e))).astype(BF16)
    qk_ref[:, DIFF_WIDTH:] = _mm(ub, w_ref[:, RWKV_COLS + DIFF_WIDTH:]).astype(BF16)
    vt_ref[0] = _nt(wvt_ref[...], ub).astype(BF16)

    z = w0_ref[...] + _mm(jnp.tanh(p_wa).astype(BF16), w2_ref[...])
    softplus_neg = jnp.maximum(-z, 0.0) + jnp.log(1.0 + jnp.exp(-jnp.abs(z)))
    log_decay = -jnp.exp(-softplus_neg - 0.5)
    a_sig = _sigmoid(a0_ref[...] + _mm(p_wa.astype(BF16), a2_ref[...]))
    gate = _mm(_sigmoid(p_g).astype(BF16), g2_ref[...])

    r_i = lax.broadcasted_iota(jnp.int32, (w, w), 0) // RWKV_HEAD
    c_i = lax.broadcasted_iota(jnp.int32, (w, w), 1) // RWKV_HEAD
    head_ones = (r_i == c_i).astype(BF16)

    kk = p_k * kk_ref[...]
    kk = kk * lax.rsqrt(jnp.maximum(_split_dot_right(kk * kk, head_ones, 2), 1e-24))
    k_fin = p_k * (1.0 + (a_sig - 1.0) * ka_ref[...])
    bonus = _split_dot_right(p_r * k_fin * rk_ref[...], head_ones, 2) * p_v

    pos = lax.broadcasted_iota(jnp.int32, log_decay.shape, 0) % WKV_CHUNK
    lg = log_decay
    step = 1
    while step < WKV_CHUNK:
        lg = lg + jnp.where(pos >= step, pltpu.roll(lg, step, axis=0), 0.0)
        step *= 2

    gam = jnp.exp(lg)
    gam_inv = jnp.exp(-lg)
    gam_prev = jnp.exp(lg - log_decay)
    rt = p_r * gam
    kt = k_fin * gam_inv
    bt = kk * a_sig * gam_inv
    at = -kk * gam_prev

    for h in range(RWKV_HEADS):
        sl = slice(h * RWKV_HEAD, (h + 1) * RWKV_HEAD)
        rt_ref[h] = rt[:, sl].astype(BF16)
        kt_ref[h] = kt[:, sl].astype(BF16)
        bt_ref[h] = bt[:, sl].astype(BF16)
        at_ref[h] = at[:, sl].astype(BF16)
        v_ref[h] = p_v[:, sl].astype(BF16)
        gam_ref[h] = gam[:, sl]
    bonus_ref[...] = bonus
    g_ref[...] = gate


def _mix_in(x2, norm_w, w_in_bf16, mu, w0, w2pad, a0, a2pad, g2, k_k, k_a, r_k, *, seq, tm):
    m, d = x2.shape
    w = RWKV_WIDTH
    const = dict(pipeline_mode=pl.Buffered(1))
    row = lambda a: a.reshape(1, -1)
    whole = lambda a: pl.BlockSpec(a.shape, lambda i: (0,) * a.ndim, **const)
    w_main = w_in_bf16[:, :RWKV_COLS + 2 * DIFF_WIDTH]
    w_vt = w_in_bf16[:, RWKV_COLS + 2 * DIFF_WIDTH:].T
    consts = (row(norm_w), w_main, w_vt, row(mu), row(w0), w2pad, row(a0), a2pad, g2,
              row(k_k), row(k_a), row(r_k))
    head_major = lambda dt: jax.ShapeDtypeStruct((RWKV_HEADS, m, RWKV_HEAD), dt)
    hm_spec = pl.BlockSpec((RWKV_HEADS, tm, RWKV_HEAD), lambda i: (0, i, 0))
    rows = lambda n: pl.BlockSpec((tm, n), lambda i: (i, 0))
    return pl.pallas_call(
        functools.partial(_mix_in_kernel, tm=tm, tiles_per_seq=seq // tm),
        grid=(m // tm,),
        in_specs=[rows(d)] + [whole(a) for a in consts],
        out_specs=[rows(2 * DIFF_WIDTH), pl.BlockSpec((1, DIFF_WIDTH, tm), lambda i: (i, 0, 0))]
        + [hm_spec] * 6 + [rows(w)] * 2,
        out_shape=[
            jax.ShapeDtypeStruct((m, 2 * DIFF_WIDTH), BF16),
            jax.ShapeDtypeStruct((m // tm, DIFF_WIDTH, tm), BF16),
        ] + [head_major(BF16)] * 5 + [head_major(F32)] + [jax.ShapeDtypeStruct((m, w), F32)] * 2,
        scratch_shapes=[pltpu.VMEM((1, RWKV_COLS), F32)],
        compiler_params=pltpu.CompilerParams(
            dimension_semantics=("arbitrary",), vmem_limit_bytes=VMEM_LIMIT_BYTES),
        name="mix_in",
    )(x2, *consts)


def _wkv_kernel(rt_ref, kt_ref, bt_ref, at_ref, v_ref, gam_ref, y_ref, s_ref, *, nc):
    @pl.when(pl.program_id(0) == 0)
    def _():
        s_ref[...] = jnp.zeros_like(s_ref)

    heads, batch = rt_ref.shape[0], rt_ref.shape[1]
    lanes = [(h, b) for b in range(batch) for h in range(heads)]
    n = len(lanes)
    c = WKV_CHUNK
    row = lax.broadcasted_iota(jnp.int32, (c, c), 0)
    col = lax.broadcasted_iota(jnp.int32, (c, c), 1)
    strict = row > col
    incl = row >= col
    eye = (row == col).astype(F32)
    bf = lambda xs: [x.astype(BF16) for x in xs]

    pre = []
    for ci in range(nc):
        sl = slice(ci * c, (ci + 1) * c)
        a = [at_ref[h, b, sl, :] for h, b in lanes]
        r = [rt_ref[h, b, sl, :] for h, b in lanes]
        bb = [bt_ref[h, b, sl, :] for h, b in lanes]
        k = [kt_ref[h, b, sl, :] for h, b in lanes]
        v = [v_ref[h, b, sl, :] for h, b in lanes]
        a_ab = [jnp.where(strict, _nt(a[i], bb[i]), 0.0) for i in range(n)]
        a_ak = bf([jnp.where(strict, _nt(a[i], k[i]), 0.0) for i in range(n)])
        a_rb = bf([jnp.where(incl, _nt(r[i], bb[i]), 0.0) for i in range(n)])
        a_rk = bf([jnp.where(incl, _nt(r[i], k[i]), 0.0) for i in range(n)])

        t_inv = [eye + x for x in a_ab]
        power = bf(a_ab)
        for _ in range(int(math.log2(c)) - 1):
            power = [_mm(p, p) for p in power]
            t_inv = [_mm(t.astype(BF16), (eye + p).astype(BF16)) for t, p in zip(t_inv, power)]
            power = bf(power)
        t_b = bf(t_inv)

        w_mat = bf([_mm(t_b[i], a[i]) for i in range(n)])
        av = bf([_mm(a_ak[i], v[i]) for i in range(n)])
        u_free = [_mm(t_b[i], av[i]) for i in range(n)]
        y_free = [_mm(a_rk[i], v[i]) for i in range(n)]
        kv = [_tn(v[i], k[i]) for i in range(n)]
        pre.append((sl, r, bb, a_rb, w_mat, u_free, y_free, kv))

    s = [s_ref[i] for i in range(n)]
    for ci in range(nc):
        sl, r, bb, a_rb, w_mat, u_free, y_free, kv = pre[ci]
        s_b = bf(s)
        u = bf([_nt(w_mat[i], s_b[i]) + u_free[i] for i in range(n)])
        s_new = [s[i] + _tn(u[i], bb[i]) + kv[i] for i in range(n)]
        y = [_nt(r[i], s_b[i]) + _mm(a_rb[i], u[i]) + y_free[i] for i in range(n)]
        for i, (h, b) in enumerate(lanes):
            g_last = gam_ref[h, b, ci * c + c - 1:(ci + 1) * c, :]
            s[i] = s_new[i] * g_last
            mean = jnp.mean(y[i], axis=-1, keepdims=True)
            yc = y[i] - mean
            var = jnp.mean(yc * yc, axis=-1, keepdims=True)
            y_ref[b, sl, h * RWKV_HEAD:(h + 1) * RWKV_HEAD] = yc * lax.rsqrt(var + GN_EPS)
    for i in range(n):
        s_ref[i] = s[i]


def _wkv(rt, kt, bt, at, v, gam, *, batch, seq, tc=128):
    heads, m, n = rt.shape
    shape4 = (heads, batch, seq, n)
    spec = pl.BlockSpec((heads, batch, tc, n), lambda t: (0, 0, t, 0))
    out = pl.pallas_call(
        functools.partial(_wkv_kernel, nc=tc // WKV_CHUNK),
        grid=(seq // tc,),
        in_specs=[spec] * 6,
        out_specs=pl.BlockSpec((batch, tc, heads * n), lambda t: (0, t, 0)),
        out_shape=jax.ShapeDtypeStruct((batch, seq, heads * n), F32),
        scratch_shapes=[pltpu.VMEM((heads * batch, n, n), F32)],
        compiler_params=pltpu.CompilerParams(
            dimension_semantics=("arbitrary",), vmem_limit_bytes=VMEM_LIMIT_BYTES),
        name="wkv",
    )(*[x.reshape(shape4) for x in (rt, kt, bt, at, v, gam)])
    return out.reshape(m, heads * n)


def _diff_attn_kernel(q_ref, k_ref, vt_ref, lq1_ref, lk1_ref, lq2_ref, lk2_ref, sw_ref, o_ref,
                      m_ref, acc_ref, st_ref, bmax_ref, *, tq, lambda_init):
    qi = pl.program_id(2)
    q = q_ref[...]
    lane = lax.broadcasted_iota(jnp.int32, q.shape, 1)
    zero = jnp.zeros_like(q)
    q_half = (jnp.where(lane < DIFF_HALF, q, zero), jnp.where(lane >= DIFF_HALF, q, zero))
    ones_rows = jnp.ones((acc_ref.shape[1] - DIFF_VDIM, tq), BF16)

    m_ref[...] = jnp.full_like(m_ref, -jnp.inf)
    acc_ref[...] = jnp.zeros_like(acc_ref)

    def scores(j, slot):
        start = pl.multiple_of(j * tq, tq)
        kb = k_ref[pl.ds(start, tq), :]
        for s in range(2):
            st = _nt(kb, q_half[s])
            st_ref[slot, s] = st
            bmax_ref[slot, s] = jnp.max(st, axis=0, keepdims=True)

    def accumulate(j, slot, diagonal):
        vtb = jnp.concatenate([vt_ref[j], ones_rows], axis=0)
        st = [st_ref[slot, s] for s in range(2)]
        if diagonal:
            ck = lax.broadcasted_iota(jnp.int32, st[0].shape, 0) // ATTN_CHUNK
            cq = lax.broadcasted_iota(jnp.int32, st[0].shape, 1) // ATTN_CHUNK
            st = [jnp.where(ck <= cq, x, -jnp.inf) for x in st]
            bmax = [jnp.max(x, axis=0, keepdims=True) for x in st]
        else:
            bmax = [bmax_ref[slot, s] for s in range(2)]
        m_prev = [m_ref[s] for s in range(2)]
        m_new = [jnp.maximum(m_prev[s], bmax[s]) for s in range(2)]
        p = [jnp.exp2(st[s] - m_new[s]).astype(BF16) for s in range(2)]
        alpha = [jnp.exp2(m_prev[s] - m_new[s]) for s in range(2)]
        for s in range(2):
            acc_ref[s] = alpha[s] * acc_ref[s] + _mm(vtb, p[s])
            m_ref[s] = m_new[s]

    scores(0, 0)

    def block_pair(pair, carry):
        t = 2 * pair
        scores(t + 1, 1)
        accumulate(t, 0, False)
        scores(t + 2, 0)
        accumulate(t + 1, 1, False)
        return carry

    lax.fori_loop(0, qi // 2, block_pair, 0)

    @pl.when(qi % 2 == 1)
    def _():
        scores(qi, 1)
        accumulate(qi - 1, 0, False)

    accumulate(qi, qi % 2, True)

    lam = (jnp.exp(jnp.sum(lq1_ref[...] * lk1_ref[...], axis=-1, keepdims=True))
           - jnp.exp(jnp.sum(lq2_ref[...] * lk2_ref[...], axis=-1, keepdims=True)) + lambda_init)
    num = [acc_ref[s, :DIFF_VDIM, :] for s in range(2)]
    den = [acc_ref[s, DIFF_VDIM:DIFF_VDIM + 1, :] for s in range(2)]
    out_t = num[0] / den[0] - lam * (num[1] / den[1])
    out = _rms(out_t.T, sw_ref[...], SUBLN_EPS) * (1.0 - lambda_init)
    o_ref[...] = out.astype(o_ref.dtype)


def _diff_attn(qk, vt, lq1, lk1, lq2, lk2, subln_w, *, batch, seq, lambda_init):
    m = qk.shape[0]
    tq = vt.shape[2]
    nq = seq // tq
    const = dict(pipeline_mode=pl.Buffered(1))
    vec = lambda n: pl.BlockSpec((1, n), lambda b, h, i: (0, 0), **const)
    row = lambda a: a.reshape(1, -1)
    return pl.pallas_call(
        functools.partial(_diff_attn_kernel, tq=tq, lambda_init=lambda_init),
        grid=(batch, DIFF_HEADS, nq),
        in_specs=[
            pl.BlockSpec((tq, DIFF_VDIM), lambda b, h, i: (b * nq + i, h)),
            pl.BlockSpec((seq, DIFF_VDIM), lambda b, h, i: (b, DIFF_HEADS + h)),
            pl.BlockSpec((nq, DIFF_VDIM, tq), lambda b, h, i: (b, h, 0)),
            vec(DIFF_HALF), vec(DIFF_HALF), vec(DIFF_HALF), vec(DIFF_HALF), vec(DIFF_VDIM),
        ],
        out_specs=pl.BlockSpec((tq, DIFF_VDIM), lambda b, h, i: (b * nq + i, h)),
        out_shape=jax.ShapeDtypeStruct((m, DIFF_WIDTH), BF16),
        scratch_shapes=[
            pltpu.VMEM((2, 1, tq), F32),
            pltpu.VMEM((2, DIFF_VDIM + BF16_SUBLANES, tq), F32),
            pltpu.VMEM((2, 2, tq, tq), F32),
            pltpu.VMEM((2, 2, 1, tq), F32),
        ],
        compiler_params=pltpu.CompilerParams(
            dimension_semantics=("arbitrary", "arbitrary", "arbitrary"),
            vmem_limit_bytes=VMEM_LIMIT_BYTES),
        name="diff_attn",
    )(qk, qk, vt, row(lq1), row(lk1), row(lq2), row(lk2), row(subln_w))


def _post_kernel(x_ref, yw_ref, bonus_ref, g_ref, yb_ref, lnw_ref, lnb_ref, wo_ref,
                 nffn_ref, wg_ref, wu_ref, wd_ref, nfin_ref, o_ref, *, tff, final_norm):
    ya = (yw_ref[...] * lnw_ref[...] + lnb_ref[...] + bonus_ref[...]) * g_ref[...]
    mixed = jnp.concatenate([ya.astype(BF16), yb_ref[...]], axis=-1)
    h1 = x_ref[...] + _mm(mixed, wo_ref[...])
    u = _rms(h1, nffn_ref[...], NORM_EPS).astype(BF16)
    acc = jnp.zeros_like(h1)
    for c0 in range(0, wg_ref.shape[1], tff):
        gt = _mm(u, wg_ref[:, c0:c0 + tff])
        up = _mm(u, wu_ref[:, c0:c0 + tff])
        act = (gt * _sigmoid(gt) * up).astype(BF16)
        acc = acc + _mm(act, wd_ref[c0:c0 + tff, :])
    h2 = h1 + acc
    if final_norm:
        h2 = _rms(h2, nfin_ref[...], NORM_EPS)
    o_ref[...] = h2


def _post(x2, yw, bonus, gate, yb, ln_w, ln_b, w_out_bf16, norm_ffn_w, wg, wu, wd, norm_final_w,
          *, final_norm, tm=512, tff=256):
    m, d = x2.shape
    const = dict(pipeline_mode=pl.Buffered(1))
    whole = lambda a: pl.BlockSpec(a.shape, lambda i: (0,) * a.ndim, **const)
    rows = lambda n: pl.BlockSpec((tm, n), lambda i: (i, 0))
    row = lambda a: a.reshape(1, -1)
    consts = (row(ln_w), row(ln_b), w_out_bf16, row(norm_ffn_w), wg, wu, wd, row(norm_final_w))
    return pl.pallas_call(
        functools.partial(_post_kernel, tff=tff, final_norm=final_norm),
        grid=(m // tm,),
        in_specs=[rows(d), rows(RWKV_WIDTH), rows(RWKV_WIDTH), rows(RWKV_WIDTH), rows(DIFF_WIDTH)]
        + [whole(a) for a in consts],
        out_specs=rows(d),
        out_shape=jax.ShapeDtypeStruct((m, d), F32),
        compiler_params=pltpu.CompilerParams(
            dimension_semantics=("arbitrary",), vmem_limit_bytes=VMEM_LIMIT_BYTES),
        name="post",
    )(x2, yw, bonus, gate, yb, *consts)


def kernel(x, norm_mix_w, w_in, mu_shift, w0, w_lora_up, a0, a_lora_up, g_lora_up, k_k, k_a, r_k,
           ln_x_w, ln_x_b, lambda_q1, lambda_k1, lambda_q2, lambda_k2, subln_w, w_out,
           norm_ffn_w, w_gate, w_up, w_down, norm_final_w):
    batch, seq, d = x.shape
    depth = w_in.shape[0]
    h = x.reshape(batch * seq, d)
    for l in range(depth):
        lambda_init = 0.8 - 0.6 * math.exp(-0.3 * l)
        w2pad = jnp.concatenate(
            [w_lora_up[l], jnp.zeros((AAA_LORA, RWKV_WIDTH), F32)], axis=0).astype(BF16)
        a2pad = jnp.concatenate(
            [jnp.zeros((DECAY_LORA, RWKV_WIDTH), F32), a_lora_up[l]], axis=0).astype(BF16)

        qk, vt, rt, kt, bt, at, v, gam, bonus, gate = _mix_in(
            h, norm_mix_w[l], w_in[l].astype(BF16), mu_shift[l], w0[l], w2pad, a0[l], a2pad,
            g_lora_up[l].astype(BF16), k_k[l], k_a[l], r_k[l].reshape(-1), seq=seq, tm=ATTN_TILE)
        yw = _wkv(rt, kt, bt, at, v, gam, batch=batch, seq=seq)
        yb = _diff_attn(qk, vt, lambda_q1[l], lambda_k1[l], lambda_q2[l], lambda_k2[l], subln_w[l],
                        batch=batch, seq=seq, lambda_init=lambda_init)
        h = _post(h, yw, bonus, gate, yb, ln_x_w[l], ln_x_b[l], w_out[l].astype(BF16),
                  norm_ffn_w[l], w_gate[l].astype(BF16), w_up[l].astype(BF16),
                  w_down[l].astype(BF16), norm_final_w, final_norm=(l == depth - 1))
    return h.reshape(batch, seq, d)
```

```python
import functools
import math

import jax
import jax.numpy as jnp
from jax import lax
from jax.experimental import pallas as pl
from jax.experimental.pallas import tpu as pltpu

F32 = jnp.float32
BF16 = jnp.bfloat16

ATTN_CHUNK = 64
RWKV_WIDTH = 512
RWKV_HEAD = 64
RWKV_HEADS = RWKV_WIDTH // RWKV_HEAD
DECAY_LORA = 64
AAA_LORA = 64
GATE_LORA = 128
DIFF_WIDTH = 512
DIFF_HALF = 64
DIFF_VDIM = 2 * DIFF_HALF
DIFF_HEADS = DIFF_WIDTH // DIFF_VDIM
NORM_EPS = 1e-6
GN_EPS = 1e-5 * RWKV_HEAD
SUBLN_EPS = 1e-5
RWKV_COLS = 3 * RWKV_WIDTH + DECAY_LORA + AAA_LORA + GATE_LORA
DIFF_COLS = 3 * DIFF_WIDTH

WKV_CHUNK = 64
ATTN_TILE = 512
BF16_SUBLANES = 16

VMEM_LIMIT_BYTES = 56 * 1024 * 1024


def _nt(a, b):
    return lax.dot_general(a, b, (((1,), (1,)), ((), ())), preferred_element_type=F32)


def _tn(a, b):
    return lax.dot_general(a, b, (((0,), (0,)), ((), ())), preferred_element_type=F32)


def _mm(a, b):
    return jnp.dot(a, b, preferred_element_type=F32)


def _rms(x, w, eps):
    return x * lax.rsqrt(jnp.mean(x * x, axis=-1, keepdims=True) + eps) * w


def _sigmoid(x):
    return 1.0 / (1.0 + jnp.exp(-x))


def _split_dot_left(mat_bf16, x, terms):
    acc = None
    rem = x
    for _ in range(terms):
        part = rem.astype(BF16)
        d = _mm(mat_bf16, part)
        acc = d if acc is None else acc + d
        rem = rem - part.astype(F32)
    return acc


def _split_dot_right(x, mat_bf16, terms):
    acc = None
    rem = x
    for _ in range(terms):
        part = rem.astype(BF16)
        d = _mm(part, mat_bf16)
        acc = d if acc is None else acc + d
        rem = rem - part.astype(F32)
    return acc


def _mix_in_kernel(x_ref, nw_ref, w_ref, wvt_ref, mu_ref, w0_ref, w2_ref, a0_ref, a2_ref, g2_ref,
                   kk_ref, ka_ref, rk_ref,
                   qk_ref, vt_ref, rt_ref, kt_ref, bt_ref, at_ref, v_ref, gam_ref, bonus_ref, g_ref,
                   carry_ref, *, tm, tiles_per_seq):
    i = pl.program_id(0)
    seq_start = (i % tiles_per_seq) == 0
    ub = _rms(x_ref[...], nw_ref[...], NORM_EPS).astype(BF16)
    w = RWKV_WIDTH

    def shifted(c0, c1):
        p = _mm(ub, w_ref[:, c0:c1])
        carry = jnp.where(seq_start, 0.0, carry_ref[:, c0:c1])
        carry_ref[:, c0:c1] = p[tm - 1:tm, :]
        row = lax.broadcasted_iota(jnp.int32, p.shape, 0)
        p_prev = jnp.where(row == 0, carry, pltpu.roll(p, 1, axis=0))
        return p + (p_prev - p) * mu_ref[:, c0:c1]

    p_r = shifted(0, w)
    p_k = shifted(w, 2 * w)
    p_v = shifted(2 * w, 3 * w)
    p_wa = shifted(3 * w, 3 * w + DECAY_LORA + AAA_LORA)
    p_g = shifted(3 * w + DECAY_LORA + AAA_LORA, RWKV_COLS)

    q = _mm(ub, w_ref[:, RWKV_COLS:RWKV_COLS + DIFF_WIDTH])
    qk_ref[:, :DIFF_WIDTH] = (q * (DIFF_HALF ** -0.5 * math.log2(math.e))).astype(BF16)
    qk_ref[:, DIFF_WIDTH:] = _mm(ub, w_ref[:, RWKV_COLS + DIFF_WIDTH:]).astype(BF16)
    vt_ref[0] = _nt(wvt_ref[...], ub).astype(BF16)

    z = w0_ref[...] + _mm(jnp.tanh(p_wa).astype(BF16), w2_ref[...])
    softplus_neg = jnp.maximum(-z, 0.0) + jnp.log(1.0 + jnp.exp(-jnp.abs(z)))
    log_decay = -jnp.exp(-softplus_neg - 0.5)
    a_sig = _sigmoid(a0_ref[...] + _mm(p_wa.astype(BF16), a2_ref[...]))
    gate = _mm(_sigmoid(p_g).astype(BF16), g2_ref[...])

    r_i = lax.broadcasted_iota(jnp.int32, (w, w), 0) // RWKV_HEAD
    c_i = lax.broadcasted_iota(jnp.int32, (w, w), 1) // RWKV_HEAD
    head_ones = (r_i == c_i).astype(BF16)

    kk = p_k * kk_ref[...]
    kk = kk * lax.rsqrt(jnp.maximum(_split_dot_right(kk * kk, head_ones, 2), 1e-24))
    k_fin = p_k * (1.0 + (a_sig - 1.0) * ka_ref[...])
    bonus = _split_dot_right(p_r * k_fin * rk_ref[...], head_ones, 2) * p_v

    pos = lax.broadcasted_iota(jnp.int32, log_decay.shape, 0) % WKV_CHUNK
    lg = log_decay
    step = 1
    while step < WKV_CHUNK:
        lg = lg + jnp.where(pos >= step, pltpu.roll(lg, step, axis=0), 0.0)
        step *= 2

    gam = jnp.exp(lg)
    gam_inv = jnp.exp(-lg)
    gam_prev = jnp.exp(lg - log_decay)
    rt = p_r * gam
    kt = k_fin * gam_inv
    bt = kk * a_sig * gam_inv
    at = -kk * gam_prev

    for h in range(RWKV_HEADS):
        sl = slice(h * RWKV_HEAD, (h + 1) * RWKV_HEAD)
        rt_ref[h] = rt[:, sl].astype(BF16)
        kt_ref[h] = kt[:, sl].astype(BF16)
        bt_ref[h] = bt[:, sl].astype(BF16)
        at_ref[h] = at[:, sl].astype(BF16)
        v_ref[h] = p_v[:, sl].astype(BF16)
        gam_ref[h] = gam[:, sl]
    bonus_ref[...] = bonus
    g_ref[...] = gate


def _mix_in(x2, norm_w, w_in_bf16, mu, w0, w2pad, a0, a2pad, g2, k_k, k_a, r_k, *, seq, tm):
    m, d = x2.shape
    w = RWKV_WIDTH
    const = dict(pipeline_mode=pl.Buffered(1))
    row = lambda a: a.reshape(1, -1)
    whole = lambda a: pl.BlockSpec(a.shape, lambda i: (0,) * a.ndim, **const)
    w_main = w_in_bf16[:, :RWKV_COLS + 2 * DIFF_WIDTH]
    w_vt = w_in_bf16[:, RWKV_COLS + 2 * DIFF_WIDTH:].T
    consts = (row(norm_w), w_main, w_vt, row(mu), row(w0), w2pad, row(a0), a2pad, g2,
              row(k_k), row(k_a), row(r_k))
    head_major = lambda dt: jax.ShapeDtypeStruct((RWKV_HEADS, m, RWKV_HEAD), dt)
    hm_spec = pl.BlockSpec((RWKV_HEADS, tm, RWKV_HEAD), lambda i: (0, i, 0))
    rows = lambda n: pl.BlockSpec((tm, n), lambda i: (i, 0))
    return pl.pallas_call(
        functools.partial(_mix_in_kernel, tm=tm, tiles_per_seq=seq // tm),
        grid=(m // tm,),
        in_specs=[rows(d)] + [whole(a) for a in consts],
        out_specs=[rows(2 * DIFF_WIDTH), pl.BlockSpec((1, DIFF_WIDTH, tm), lambda i: (i, 0, 0))]
        + [hm_spec] * 6 + [rows(w)] * 2,
        out_shape=[
            jax.ShapeDtypeStruct((m, 2 * DIFF_WIDTH), BF16),
            jax.ShapeDtypeStruct((m // tm, DIFF_WIDTH, tm), BF16),
        ] + [head_major(BF16)] * 5 + [head_major(F32)] + [jax.ShapeDtypeStruct((m, w), F32)] * 2,
        scratch_shapes=[pltpu.VMEM((1, RWKV_COLS), F32)],
        compiler_params=pltpu.CompilerParams(
            dimension_semantics=("arbitrary",), vmem_limit_bytes=VMEM_LIMIT_BYTES),
        name="mix_in",
    )(x2, *consts)


def _wkv_kernel(rt_ref, kt_ref, bt_ref, at_ref, v_ref, gam_ref, y_ref, s_ref, *, nc):
    @pl.when(pl.program_id(0) == 0)
    def _():
        s_ref[...] = jnp.zeros_like(s_ref)

    heads, batch = rt_ref.shape[0], rt_ref.shape[1]
    lanes = [(h, b) for b in range(batch) for h in range(heads)]
    n = len(lanes)
    c = WKV_CHUNK
    row = lax.broadcasted_iota(jnp.int32, (c, 2 * c), 0)
    col = lax.broadcasted_iota(jnp.int32, (c, 2 * c), 1)
    strict2 = row > col % c
    incl2 = row >= col % c
    eye_hi = (col == row + c).astype(BF16)
    zeros_b = jnp.zeros((c, c), BF16)
    bf = lambda xs: [x.astype(BF16) for x in xs]
    widen = lambda x: jnp.concatenate([x, jnp.zeros_like(x)], axis=1)

    pre = []
    for ci in range(nc):
        sl = slice(ci * c, (ci + 1) * c)
        a = [at_ref[h, b, sl, :] for h, b in lanes]
        r = [rt_ref[h, b, sl, :] for h, b in lanes]
        v = [v_ref[h, b, sl, :] for h, b in lanes]
        bk = [jnp.concatenate([bt_ref[h, b, sl, :], kt_ref[h, b, sl, :]], axis=0) for h, b in lanes]
        a_a = [jnp.where(strict2, _nt(a[i], bk[i]), 0.0) for i in range(n)]
        a_r = bf([jnp.where(incl2, _nt(r[i], bk[i]), 0.0) for i in range(n)])
        zv = [jnp.concatenate([zeros_b, v[i]], axis=0) for i in range(n)]
        av = [_mm(a_a[i].astype(BF16), widen(zv[i])) for i in range(n)]
        x = [widen(a[i]).astype(F32) + pltpu.roll(av[i], c, axis=1) for i in range(n)]
        power = bf([m[:, :c] for m in a_a])
        levels = int(math.log2(c))
        for lv in range(levels):
            if lv + 1 < levels:
                rhs = [jnp.concatenate([x[i].astype(BF16), widen(power[i])], axis=1)
                       for i in range(n)]
                res = [_mm(power[i], rhs[i]) for i in range(n)]
                x = [x[i] + res[i][:, :2 * c] for i in range(n)]
                power = bf([res[i][:, 2 * c:3 * c] for i in range(n)])
            else:
                x = [x[i] + _mm(power[i], x[i].astype(BF16)) for i in range(n)]
        uv_tail = v
        pre.append((sl, r, bk, a_r, bf(x), uv_tail))

    s = [s_ref[i] for i in range(n)]
    for ci in range(nc):
        sl, r, bk, a_r, x, v = pre[ci]
        s_b = bf(s)
        s_i = [widen(s_b[i]) + eye_hi for i in range(n)]
        u = bf([_nt(x[i], s_i[i]) for i in range(n)])
        uv = [jnp.concatenate([u[i], v[i]], axis=0) for i in range(n)]
        s_new = [s[i] + _tn(uv[i], bk[i]) for i in range(n)]
        y = [_nt(r[i], s_b[i]) + _mm(a_r[i], uv[i]) for i in range(n)]
        for i, (h, b) in enumerate(lanes):
            g_last = gam_ref[h, b, ci * c + c - 1:(ci + 1) * c, :]
            s[i] = s_new[i] * g_last
            mean = jnp.mean(y[i], axis=-1, keepdims=True)
            yc = y[i] - mean
            var = jnp.mean(yc * yc, axis=-1, keepdims=True)
            y_ref[b, sl, h * RWKV_HEAD:(h + 1) * RWKV_HEAD] = yc * lax.rsqrt(var + GN_EPS)
    for i in range(n):
        s_ref[i] = s[i]


def _wkv(rt, kt, bt, at, v, gam, *, batch, seq, tc=256):
    heads, m, n = rt.shape
    shape4 = (heads, batch, seq, n)
    spec = pl.BlockSpec((heads, batch, tc, n), lambda t: (0, 0, t, 0))
    out = pl.pallas_call(
        functools.partial(_wkv_kernel, nc=tc // WKV_CHUNK),
        grid=(seq // tc,),
        in_specs=[spec] * 6,
        out_specs=pl.BlockSpec((batch, tc, heads * n), lambda t: (0, t, 0)),
        out_shape=jax.ShapeDtypeStruct((batch, seq, heads * n), F32),
        scratch_shapes=[pltpu.VMEM((heads * batch, n, n), F32)],
        compiler_params=pltpu.CompilerParams(
            dimension_semantics=("arbitrary",), vmem_limit_bytes=VMEM_LIMIT_BYTES),
        name="wkv",
    )(*[x.reshape(shape4) for x in (rt, kt, bt, at, v, gam)])
    return out.reshape(m, heads * n)


def _diff_attn_kernel(q_ref, k_ref, vt_ref, lq1_ref, lk1_ref, lq2_ref, lk2_ref, sw_ref, o_ref,
                      m_ref, acc_ref, st_ref, bmax_ref, *, tq, lambda_init):
    qi = pl.program_id(2)
    q = q_ref[...]
    lane = lax.broadcasted_iota(jnp.int32, q.shape, 1)
    zero = jnp.zeros_like(q)
    q_half = (jnp.where(lane < DIFF_HALF, q, zero), jnp.where(lane >= DIFF_HALF, q, zero))
    ones_rows = jnp.ones((acc_ref.shape[1] - DIFF_VDIM, tq), BF16)

    m_ref[...] = jnp.full_like(m_ref, -jnp.inf)
    acc_ref[...] = jnp.zeros_like(acc_ref)

    def scores(j, slot):
        start = pl.multiple_of(j * tq, tq)
        kb = k_ref[pl.ds(start, tq), :]
        for s in range(2):
            st = _nt(kb, q_half[s])
            st_ref[slot, s] = st
            bmax_ref[slot, s] = jnp.max(st, axis=0, keepdims=True)

    def accumulate(j, slot, diagonal):
        vtb = jnp.concatenate([vt_ref[j], ones_rows], axis=0)
        st = [st_ref[slot, s] for s in range(2)]
        if diagonal:
            ck = lax.broadcasted_iota(jnp.int32, st[0].shape, 0) // ATTN_CHUNK
            cq = lax.broadcasted_iota(jnp.int32, st[0].shape, 1) // ATTN_CHUNK
            st = [jnp.where(ck <= cq, x, -jnp.inf) for x in st]
            bmax = [jnp.max(x, axis=0, keepdims=True) for x in st]
        else:
            bmax = [bmax_ref[slot, s] for s in range(2)]
        m_prev = [m_ref[s] for s in range(2)]
        m_new = [jnp.maximum(m_prev[s], bmax[s]) for s in range(2)]
        p = [jnp.exp2(st[s] - m_new[s]).astype(BF16) for s in range(2)]
        alpha = [jnp.exp2(m_prev[s] - m_new[s]) for s in range(2)]
        for s in range(2):
            acc_ref[s] = alpha[s] * acc_ref[s] + _mm(vtb, p[s])
            m_ref[s] = m_new[s]

    scores(0, 0)

    def block_pair(pair, carry):
        t = 2 * pair
        scores(t + 1, 1)
        accumulate(t, 0, False)
        scores(t + 2, 0)
        accumulate(t + 1, 1, False)
        return carry

    lax.fori_loop(0, qi // 2, block_pair, 0)

    @pl.when(qi % 2 == 1)
    def _():
        scores(qi, 1)
        accumulate(qi - 1, 0, False)

    accumulate(qi, qi % 2, True)

    lam = (jnp.exp(jnp.sum(lq1_ref[...] * lk1_ref[...], axis=-1, keepdims=True))
           - jnp.exp(jnp.sum(lq2_ref[...] * lk2_ref[...], axis=-1, keepdims=True)) + lambda_init)
    num = [acc_ref[s, :DIFF_VDIM, :] for s in range(2)]
    den = [acc_ref[s, DIFF_VDIM:DIFF_VDIM + 1, :] for s in range(2)]
    out_t = num[0] / den[0] - lam * (num[1] / den[1])
    out = _rms(out_t.T, sw_ref[...], SUBLN_EPS) * (1.0 - lambda_init)
    o_ref[...] = out.astype(o_ref.dtype)


def _diff_attn(qk, vt, lq1, lk1, lq2, lk2, subln_w, *, batch, seq, lambda_init):
    m = qk.shape[0]
    tq = vt.shape[2]
    nq = seq // tq
    const = dict(pipeline_mode=pl.Buffered(1))
    vec = lambda n: pl.BlockSpec((1, n), lambda b, h, i: (0, 0), **const)
    row = lambda a: a.reshape(1, -1)
    return pl.pallas_call(
        functools.partial(_diff_attn_kernel, tq=tq, lambda_init=lambda_init),
        grid=(batch, DIFF_HEADS, nq),
        in_specs=[
            pl.BlockSpec((tq, DIFF_VDIM), lambda b, h, i: (b * nq + i, h)),
            pl.BlockSpec((seq, DIFF_VDIM), lambda b, h, i: (b, DIFF_HEADS + h)),
            pl.BlockSpec((nq, DIFF_VDIM, tq), lambda b, h, i: (b, h, 0)),
            vec(DIFF_HALF), vec(DIFF_HALF), vec(DIFF_HALF), vec(DIFF_HALF), vec(DIFF_VDIM),
        ],
        out_specs=pl.BlockSpec((tq, DIFF_VDIM), lambda b, h, i: (b * nq + i, h)),
        out_shape=jax.ShapeDtypeStruct((m, DIFF_WIDTH), BF16),
        scratch_shapes=[
            pltpu.VMEM((2, 1, tq), F32),
            pltpu.VMEM((2, DIFF_VDIM + BF16_SUBLANES, tq), F32),
            pltpu.VMEM((2, 2, tq, tq), F32),
            pltpu.VMEM((2, 2, 1, tq), F32),
        ],
        compiler_params=pltpu.CompilerParams(
            dimension_semantics=("arbitrary", "arbitrary", "arbitrary"),
            vmem_limit_bytes=VMEM_LIMIT_BYTES),
        name="diff_attn",
    )(qk, qk, vt, row(lq1), row(lk1), row(lq2), row(lk2), row(subln_w))


def _post_kernel(x_ref, yw_ref, bonus_ref, g_ref, yb_ref, lnw_ref, lnb_ref, wo_ref,
                 nffn_ref, wg_ref, wu_ref, wd_ref, nfin_ref, o_ref, *, tff, final_norm):
    ya = (yw_ref[...] * lnw_ref[...] + lnb_ref[...] + bonus_ref[...]) * g_ref[...]
    mixed = jnp.concatenate([ya.astype(BF16), yb_ref[...]], axis=-1)
    h1 = x_ref[...] + _mm(mixed, wo_ref[...])
    u = _rms(h1, nffn_ref[...], NORM_EPS).astype(BF16)
    acc = jnp.zeros_like(h1)
    for c0 in range(0, wg_ref.shape[1], tff):
        gt = _mm(u, wg_ref[:, c0:c0 + tff])
        up = _mm(u, wu_ref[:, c0:c0 + tff])
        act = (gt * _sigmoid(gt) * up).astype(BF16)
        acc = acc + _mm(act, wd_ref[c0:c0 + tff, :])
    h2 = h1 + acc
    if final_norm:
        h2 = _rms(h2, nfin_ref[...], NORM_EPS)
    o_ref[...] = h2


def _post(x2, yw, bonus, gate, yb, ln_w, ln_b, w_out_bf16, norm_ffn_w, wg, wu, wd, norm_final_w,
          *, final_norm, tm=512, tff=256):
    m, d = x2.shape
    const = dict(pipeline_mode=pl.Buffered(1))
    whole = lambda a: pl.BlockSpec(a.shape, lambda i: (0,) * a.ndim, **const)
    rows = lambda n: pl.BlockSpec((tm, n), lambda i: (i, 0))
    row = lambda a: a.reshape(1, -1)
    consts = (row(ln_w), row(ln_b), w_out_bf16, row(norm_ffn_w), wg, wu, wd, row(norm_final_w))
    return pl.pallas_call(
        functools.partial(_post_kernel, tff=tff, final_norm=final_norm),
        grid=(m // tm,),
        in_specs=[rows(d), rows(RWKV_WIDTH), rows(RWKV_WIDTH), rows(RWKV_WIDTH), rows(DIFF_WIDTH)]
        + [whole(a) for a in consts],
        out_specs=rows(d),
        out_shape=jax.ShapeDtypeStruct((m, d), F32),
        compiler_params=pltpu.CompilerParams(
            dimension_semantics=("arbitrary",), vmem_limit_bytes=VMEM_LIMIT_BYTES),
        name="post",
    )(x2, yw, bonus, gate, yb, *consts)


def kernel(x, norm_mix_w, w_in, mu_shift, w0, w_lora_up, a0, a_lora_up, g_lora_up, k_k, k_a, r_k,
           ln_x_w, ln_x_b, lambda_q1, lambda_k1, lambda_q2, lambda_k2, subln_w, w_out,
           norm_ffn_w, w_gate, w_up, w_down, norm_final_w):
    batch, seq, d = x.shape
    depth = w_in.shape[0]
    h = x.reshape(batch * seq, d)
    for l in range(depth):
        lambda_init = 0.8 - 0.6 * math.exp(-0.3 * l)
        w2pad = jnp.concatenate(
            [w_lora_up[l], jnp.zeros((AAA_LORA, RWKV_WIDTH), F32)], axis=0).astype(BF16)
        a2pad = jnp.concatenate(
            [jnp.zeros((DECAY_LORA, RWKV_WIDTH), F32), a_lora_up[l]], axis=0).astype(BF16)

        qk, vt, rt, kt, bt, at, v, gam, bonus, gate = _mix_in(
            h, norm_mix_w[l], w_in[l].astype(BF16), mu_shift[l], w0[l], w2pad, a0[l], a2pad,
            g_lora_up[l].astype(BF16), k_k[l], k_a[l], r_k[l].reshape(-1), seq=seq, tm=ATTN_TILE)
        yw = _wkv(rt, kt, bt, at, v, gam, batch=batch, seq=seq)
        yb = _diff_attn(qk, vt, lambda_q1[l], lambda_k1[l], lambda_q2[l], lambda_k2[l], subln_w[l],
                        batch=batch, seq=seq, lambda_init=lambda_init)
        h = _post(h, yw, bonus, gate, yb, ln_x_w[l], ln_x_b[l], w_out[l].astype(BF16),
                  norm_ffn_w[l], w_gate[l].astype(BF16), w_up[l].astype(BF16),
                  w_down[l].astype(BF16), norm_final_w, final_norm=(l == depth - 1))
    return h.reshape(batch, seq, d)
```

```python
import functools
import math

import jax
import jax.numpy as jnp
from jax import lax
from jax.experimental import pallas as pl
from jax.experimental.pallas import tpu as pltpu

F32 = jnp.float32
BF16 = jnp.bfloat16

ATTN_CHUNK = 64
RWKV_WIDTH = 512
RWKV_HEAD = 64
RWKV_HEADS = RWKV_WIDTH // RWKV_HEAD
DECAY_LORA = 64
AAA_LORA = 64
GATE_LORA = 128
DIFF_WIDTH = 512
DIFF_HALF = 64
DIFF_VDIM = 2 * DIFF_HALF
DIFF_HEADS = DIFF_WIDTH // DIFF_VDIM
NORM_EPS = 1e-6
GN_EPS = 1e-5 * RWKV_HEAD
SUBLN_EPS = 1e-5
RWKV_COLS = 3 * RWKV_WIDTH + DECAY_LORA + AAA_LORA + GATE_LORA
DIFF_COLS = 3 * DIFF_WIDTH

WKV_CHUNK = 64
ATTN_TILE = 512
BF16_SUBLANES = 16
ATTN_HEADS_PER_STEP = 2

VMEM_LIMIT_BYTES = 56 * 1024 * 1024


def _nt(a, b):
    return lax.dot_general(a, b, (((1,), (1,)), ((), ())), preferred_element_type=F32)


def _tn(a, b):
    return lax.dot_general(a, b, (((0,), (0,)), ((), ())), preferred_element_type=F32)


def _mm(a, b):
    return jnp.dot(a, b, preferred_element_type=F32)


def _rms(x, w, eps):
    return x * lax.rsqrt(jnp.mean(x * x, axis=-1, keepdims=True) + eps) * w


def _sigmoid(x):
    return 1.0 / (1.0 + jnp.exp(-x))


def _split_dot_left(mat_bf16, x, terms):
    acc = None
    rem = x
    for _ in range(terms):
        part = rem.astype(BF16)
        d = _mm(mat_bf16, part)
        acc = d if acc is None else acc + d
        rem = rem - part.astype(F32)
    return acc


def _split_dot_right(x, mat_bf16, terms):
    acc = None
    rem = x
    for _ in range(terms):
        part = rem.astype(BF16)
        d = _mm(part, mat_bf16)
        acc = d if acc is None else acc + d
        rem = rem - part.astype(F32)
    return acc


def _mix_in_kernel(x_ref, nw_ref, w_ref, wvt_ref, mu_ref, w0_ref, w2_ref, a0_ref, a2_ref, g2_ref,
                   kk_ref, ka_ref, rk_ref,
                   qk_ref, vt_ref, rt_ref, kt_ref, bt_ref, at_ref, v_ref, gam_ref, bonus_ref, g_ref,
                   carry_ref, *, tm, tiles_per_seq):
    i = pl.program_id(0)
    seq_start = (i % tiles_per_seq) == 0
    ub = _rms(x_ref[...], nw_ref[...], NORM_EPS).astype(BF16)
    w = RWKV_WIDTH

    def shifted(c0, c1):
        p = _mm(ub, w_ref[:, c0:c1])
        carry = jnp.where(seq_start, 0.0, carry_ref[:, c0:c1])
        carry_ref[:, c0:c1] = p[tm - 1:tm, :]
        row = lax.broadcasted_iota(jnp.int32, p.shape, 0)
        p_prev = jnp.where(row == 0, carry, pltpu.roll(p, 1, axis=0))
        return p + (p_prev - p) * mu_ref[:, c0:c1]

    p_r = shifted(0, w)
    p_k = shifted(w, 2 * w)
    p_v = shifted(2 * w, 3 * w)
    p_wa = shifted(3 * w, 3 * w + DECAY_LORA + AAA_LORA)
    p_g = shifted(3 * w + DECAY_LORA + AAA_LORA, RWKV_COLS)

    q = _mm(ub, w_ref[:, RWKV_COLS:RWKV_COLS + DIFF_WIDTH])
    qk_ref[:, :DIFF_WIDTH] = (q * (DIFF_HALF ** -0.5 * math.log2(math.e))).astype(BF16)
    qk_ref[:, DIFF_WIDTH:] = _mm(ub, w_ref[:, RWKV_COLS + DIFF_WIDTH:]).astype(BF16)
    vt_ref[0] = _nt(wvt_ref[...], ub).astype(BF16)

    z = w0_ref[...] + _mm(jnp.tanh(p_wa).astype(BF16), w2_ref[...])
    softplus_neg = jnp.maximum(-z, 0.0) + jnp.log(1.0 + jnp.exp(-jnp.abs(z)))
    log_decay = -jnp.exp(-softplus_neg - 0.5)
    a_sig = _sigmoid(a0_ref[...] + _mm(p_wa.astype(BF16), a2_ref[...]))
    gate = _mm(_sigmoid(p_g).astype(BF16), g2_ref[...])

    r_i = lax.broadcasted_iota(jnp.int32, (w, w), 0) // RWKV_HEAD
    c_i = lax.broadcasted_iota(jnp.int32, (w, w), 1) // RWKV_HEAD
    head_ones = (r_i == c_i).astype(BF16)

    kk = p_k * kk_ref[...]
    kk = kk * lax.rsqrt(jnp.maximum(_split_dot_right(kk * kk, head_ones, 2), 1e-24))
    k_fin = p_k * (1.0 + (a_sig - 1.0) * ka_ref[...])
    bonus = _split_dot_right(p_r * k_fin * rk_ref[...], head_ones, 2) * p_v

    pos = lax.broadcasted_iota(jnp.int32, log_decay.shape, 0) % WKV_CHUNK
    lg = log_decay
    step = 1
    while step < WKV_CHUNK:
        lg = lg + jnp.where(pos >= step, pltpu.roll(lg, step, axis=0), 0.0)
        step *= 2

    gam = jnp.exp(lg)
    gam_inv = jnp.exp(-lg)
    gam_prev = jnp.exp(lg - log_decay)
    rt = p_r * gam
    kt = k_fin * gam_inv
    bt = kk * a_sig * gam_inv
    at = -kk * gam_prev

    for h in range(RWKV_HEADS):
        sl = slice(h * RWKV_HEAD, (h + 1) * RWKV_HEAD)
        rt_ref[h] = rt[:, sl].astype(BF16)
        kt_ref[h] = kt[:, sl].astype(BF16)
        bt_ref[h] = bt[:, sl].astype(BF16)
        at_ref[h] = at[:, sl].astype(BF16)
        v_ref[h] = p_v[:, sl].astype(BF16)
        gam_ref[h] = gam[:, sl]
    bonus_ref[...] = bonus
    g_ref[...] = gate


def _mix_in(x2, norm_w, w_in_bf16, mu, w0, w2pad, a0, a2pad, g2, k_k, k_a, r_k, *, seq, tm):
    m, d = x2.shape
    w = RWKV_WIDTH
    const = dict(pipeline_mode=pl.Buffered(1))
    row = lambda a: a.reshape(1, -1)
    whole = lambda a: pl.BlockSpec(a.shape, lambda i: (0,) * a.ndim, **const)
    w_main = w_in_bf16[:, :RWKV_COLS + 2 * DIFF_WIDTH]
    w_vt = w_in_bf16[:, RWKV_COLS + 2 * DIFF_WIDTH:].T
    consts = (row(norm_w), w_main, w_vt, row(mu), row(w0), w2pad, row(a0), a2pad, g2,
              row(k_k), row(k_a), row(r_k))
    head_major = lambda dt: jax.ShapeDtypeStruct((RWKV_HEADS, m, RWKV_HEAD), dt)
    hm_spec = pl.BlockSpec((RWKV_HEADS, tm, RWKV_HEAD), lambda i: (0, i, 0))
    rows = lambda n: pl.BlockSpec((tm, n), lambda i: (i, 0))
    return pl.pallas_call(
        functools.partial(_mix_in_kernel, tm=tm, tiles_per_seq=seq // tm),
        grid=(m // tm,),
        in_specs=[rows(d)] + [whole(a) for a in consts],
        out_specs=[rows(2 * DIFF_WIDTH), pl.BlockSpec((1, DIFF_WIDTH, tm), lambda i: (i, 0, 0))]
        + [hm_spec] * 6 + [rows(w)] * 2,
        out_shape=[
            jax.ShapeDtypeStruct((m, 2 * DIFF_WIDTH), BF16),
            jax.ShapeDtypeStruct((m // tm, DIFF_WIDTH, tm), BF16),
        ] + [head_major(BF16)] * 5 + [head_major(F32)] + [jax.ShapeDtypeStruct((m, w), F32)] * 2,
        scratch_shapes=[pltpu.VMEM((1, RWKV_COLS), F32)],
        compiler_params=pltpu.CompilerParams(
            dimension_semantics=("arbitrary",), vmem_limit_bytes=VMEM_LIMIT_BYTES),
        name="mix_in",
    )(x2, *consts)


def _wkv_kernel(rt_ref, kt_ref, bt_ref, at_ref, v_ref, gam_ref, y_ref, s_ref, *, nc):
    @pl.when(pl.program_id(0) == 0)
    def _():
        s_ref[...] = jnp.zeros_like(s_ref)

    heads, batch = rt_ref.shape[0], rt_ref.shape[1]
    lanes = [(h, b) for b in range(batch) for h in range(heads)]
    n = len(lanes)
    c = WKV_CHUNK
    row = lax.broadcasted_iota(jnp.int32, (c, 2 * c), 0)
    col = lax.broadcasted_iota(jnp.int32, (c, 2 * c), 1)
    strict2 = row > col % c
    incl2 = row >= col % c
    eye_hi = (col == row + c).astype(BF16)
    zeros_b = jnp.zeros((c, c), BF16)
    bf = lambda xs: [x.astype(BF16) for x in xs]
    widen = lambda x: jnp.concatenate([x, jnp.zeros_like(x)], axis=1)

    pre = []
    for ci in range(nc):
        sl = slice(ci * c, (ci + 1) * c)
        a = [at_ref[h, b, sl, :] for h, b in lanes]
        r = [rt_ref[h, b, sl, :] for h, b in lanes]
        v = [v_ref[h, b, sl, :] for h, b in lanes]
        bk = [jnp.concatenate([bt_ref[h, b, sl, :], kt_ref[h, b, sl, :]], axis=0) for h, b in lanes]
        a_a = [jnp.where(strict2, _nt(a[i], bk[i]), 0.0) for i in range(n)]
        a_r = bf([jnp.where(incl2, _nt(r[i], bk[i]), 0.0) for i in range(n)])
        zv = [jnp.concatenate([zeros_b, v[i]], axis=0) for i in range(n)]
        av = [_mm(a_a[i].astype(BF16), widen(zv[i])) for i in range(n)]
        x = [widen(a[i]).astype(F32) + pltpu.roll(av[i], c, axis=1) for i in range(n)]
        power = bf([m[:, :c] for m in a_a])
        levels = int(math.log2(c))
        for lv in range(levels):
            if lv + 1 < levels:
                rhs = [jnp.concatenate([x[i].astype(BF16), widen(power[i])], axis=1)
                       for i in range(n)]
                res = [_mm(power[i], rhs[i]) for i in range(n)]
                x = [x[i] + res[i][:, :2 * c] for i in range(n)]
                power = bf([res[i][:, 2 * c:3 * c] for i in range(n)])
            else:
                x = [x[i] + _mm(power[i], x[i].astype(BF16)) for i in range(n)]
        uv_tail = v
        pre.append((sl, r, bk, a_r, bf(x), uv_tail))

    s = [s_ref[i] for i in range(n)]
    for ci in range(nc):
        sl, r, bk, a_r, x, v = pre[ci]
        s_b = bf(s)
        s_i = [widen(s_b[i]) + eye_hi for i in range(n)]
        u = bf([_nt(x[i], s_i[i]) for i in range(n)])
        uv = [jnp.concatenate([u[i], v[i]], axis=0) for i in range(n)]
        s_new = [s[i] + _tn(uv[i], bk[i]) for i in range(n)]
        y = [_nt(r[i], s_b[i]) + _mm(a_r[i], uv[i]) for i in range(n)]
        for i, (h, b) in enumerate(lanes):
            g_last = gam_ref[h, b, ci * c + c - 1:(ci + 1) * c, :]
            s[i] = s_new[i] * g_last
            mean = jnp.mean(y[i], axis=-1, keepdims=True)
            yc = y[i] - mean
            var = jnp.mean(yc * yc, axis=-1, keepdims=True)
            y_ref[b, sl, h * RWKV_HEAD:(h + 1) * RWKV_HEAD] = yc * lax.rsqrt(var + GN_EPS)
    for i in range(n):
        s_ref[i] = s[i]


def _wkv(rt, kt, bt, at, v, gam, *, batch, seq, tc=256):
    heads, m, n = rt.shape
    shape4 = (heads, batch, seq, n)
    spec = pl.BlockSpec((heads, batch, tc, n), lambda t: (0, 0, t, 0))
    out = pl.pallas_call(
        functools.partial(_wkv_kernel, nc=tc // WKV_CHUNK),
        grid=(seq // tc,),
        in_specs=[spec] * 6,
        out_specs=pl.BlockSpec((batch, tc, heads * n), lambda t: (0, t, 0)),
        out_shape=jax.ShapeDtypeStruct((batch, seq, heads * n), F32),
        scratch_shapes=[pltpu.VMEM((heads * batch, n, n), F32)],
        compiler_params=pltpu.CompilerParams(
            dimension_semantics=("arbitrary",), vmem_limit_bytes=VMEM_LIMIT_BYTES),
        name="wkv",
    )(*[x.reshape(shape4) for x in (rt, kt, bt, at, v, gam)])
    return out.reshape(m, heads * n)


def _diff_attn_kernel(q_ref, k_ref, vt_ref, lq1_ref, lk1_ref, lq2_ref, lk2_ref, sw_ref, o_ref,
                      m_ref, acc_ref, st_ref, bmax_ref, *, tq, lambda_init):
    qi = pl.program_id(2)
    hw = DIFF_VDIM
    streams = [(hd, s) for hd in range(ATTN_HEADS_PER_STEP) for s in range(2)]
    q = q_ref[...]
    lane = lax.broadcasted_iota(jnp.int32, (tq, hw), 1)
    q_half = []
    for hd, s in streams:
        qh = q[:, hd * hw:(hd + 1) * hw]
        keep = (lane < DIFF_HALF) if s == 0 else (lane >= DIFF_HALF)
        q_half.append(jnp.where(keep, qh, jnp.zeros_like(qh)))
    ones_rows = jnp.ones((acc_ref.shape[1] - hw, tq), BF16)

    m_ref[...] = jnp.full_like(m_ref, -jnp.inf)
    acc_ref[...] = jnp.zeros_like(acc_ref)

    def scores(j, slot):
        start = pl.multiple_of(j * tq, tq)
        kb = k_ref[pl.ds(start, tq), :]
        for i, (hd, s) in enumerate(streams):
            st = _nt(kb[:, hd * hw:(hd + 1) * hw], q_half[i])
            st_ref[slot, i] = st
            bmax_ref[slot, i] = jnp.max(st, axis=0, keepdims=True)

    def accumulate(j, slot, diagonal):
        vt = vt_ref[j]
        vtb = [jnp.concatenate([vt[hd * hw:(hd + 1) * hw], ones_rows], axis=0)
               for hd in range(ATTN_HEADS_PER_STEP)]
        st = [st_ref[slot, i] for i in range(len(streams))]
        if diagonal:
            ck = lax.broadcasted_iota(jnp.int32, st[0].shape, 0) // ATTN_CHUNK
            cq = lax.broadcasted_iota(jnp.int32, st[0].shape, 1) // ATTN_CHUNK
            st = [jnp.where(ck <= cq, x, -jnp.inf) for x in st]
            bmax = [jnp.max(x, axis=0, keepdims=True) for x in st]
        else:
            bmax = [bmax_ref[slot, i] for i in range(len(streams))]
        m_prev = [m_ref[i] for i in range(len(streams))]
        m_new = [jnp.maximum(a, b) for a, b in zip(m_prev, bmax)]
        p = [jnp.exp2(x - mx).astype(BF16) for x, mx in zip(st, m_new)]
        alpha = [jnp.exp2(a - b) for a, b in zip(m_prev, m_new)]
        for i, (hd, s) in enumerate(streams):
            acc_ref[i] = alpha[i] * acc_ref[i] + _mm(vtb[hd], p[i])
            m_ref[i] = m_new[i]

    scores(0, 0)

    def block_pair(pair, carry):
        t = 2 * pair
        scores(t + 1, 1)
        accumulate(t, 0, False)
        scores(t + 2, 0)
        accumulate(t + 1, 1, False)
        return carry

    lax.fori_loop(0, qi // 2, block_pair, 0)

    @pl.when(qi % 2 == 1)
    def _():
        scores(qi, 1)
        accumulate(qi - 1, 0, False)

    accumulate(qi, qi % 2, True)

    lam = (jnp.exp(jnp.sum(lq1_ref[...] * lk1_ref[...], axis=-1, keepdims=True))
           - jnp.exp(jnp.sum(lq2_ref[...] * lk2_ref[...], axis=-1, keepdims=True)) + lambda_init)
    for hd in range(ATTN_HEADS_PER_STEP):
        num = [acc_ref[2 * hd + s, :hw, :] for s in range(2)]
        den = [acc_ref[2 * hd + s, hw:hw + 1, :] for s in range(2)]
        out_t = num[0] / den[0] - lam * (num[1] / den[1])
        out = _rms(out_t.T, sw_ref[...], SUBLN_EPS) * (1.0 - lambda_init)
        o_ref[:, hd * hw:(hd + 1) * hw] = out.astype(o_ref.dtype)


def _diff_attn(qk, vt, lq1, lk1, lq2, lk2, subln_w, *, batch, seq, lambda_init):
    m = qk.shape[0]
    tq = vt.shape[2]
    nq = seq // tq
    hs = ATTN_HEADS_PER_STEP
    wide = hs * DIFF_VDIM
    n_streams = 2 * hs
    const = dict(pipeline_mode=pl.Buffered(1))
    vec = lambda n: pl.BlockSpec((1, n), lambda b, h, i: (0, 0), **const)
    row = lambda a: a.reshape(1, -1)
    return pl.pallas_call(
        functools.partial(_diff_attn_kernel, tq=tq, lambda_init=lambda_init),
        grid=(batch, DIFF_HEADS // hs, nq),
        in_specs=[
            pl.BlockSpec((tq, wide), lambda b, h, i: (b * nq + i, h)),
            pl.BlockSpec((seq, wide), lambda b, h, i: (b, DIFF_HEADS // hs + h)),
            pl.BlockSpec((nq, wide, tq), lambda b, h, i: (b, h, 0)),
            vec(DIFF_HALF), vec(DIFF_HALF), vec(DIFF_HALF), vec(DIFF_HALF), vec(DIFF_VDIM),
        ],
        out_specs=pl.BlockSpec((tq, wide), lambda b, h, i: (b * nq + i, h)),
        out_shape=jax.ShapeDtypeStruct((m, DIFF_WIDTH), BF16),
        scratch_shapes=[
            pltpu.VMEM((n_streams, 1, tq), F32),
            pltpu.VMEM((n_streams, DIFF_VDIM + BF16_SUBLANES, tq), F32),
            pltpu.VMEM((2, n_streams, tq, tq), F32),
            pltpu.VMEM((2, n_streams, 1, tq), F32),
        ],
        compiler_params=pltpu.CompilerParams(
            dimension_semantics=("arbitrary", "arbitrary", "arbitrary"),
            vmem_limit_bytes=VMEM_LIMIT_BYTES),
        name="diff_attn",
    )(qk, qk, vt, row(lq1), row(lk1), row(lq2), row(lk2), row(subln_w))


def _post_kernel(x_ref, yw_ref, bonus_ref, g_ref, yb_ref, lnw_ref, lnb_ref, wo_ref,
                 nffn_ref, wg_ref, wu_ref, wd_ref, nfin_ref, o_ref, *, tff, final_norm):
    ya = (yw_ref[...] * lnw_ref[...] + lnb_ref[...] + bonus_ref[...]) * g_ref[...]
    mixed = jnp.concatenate([ya.astype(BF16), yb_ref[...]], axis=-1)
    h1 = x_ref[...] + _mm(mixed, wo_ref[...])
    u = _rms(h1, nffn_ref[...], NORM_EPS).astype(BF16)
    acc = jnp.zeros_like(h1)
    for c0 in range(0, wg_ref.shape[1], tff):
        gt = _mm(u, wg_ref[:, c0:c0 + tff])
        up = _mm(u, wu_ref[:, c0:c0 + tff])
        act = (gt * _sigmoid(gt) * up).astype(BF16)
        acc = acc + _mm(act, wd_ref[c0:c0 + tff, :])
    h2 = h1 + acc
    if final_norm:
        h2 = _rms(h2, nfin_ref[...], NORM_EPS)
    o_ref[...] = h2


def _post(x2, yw, bonus, gate, yb, ln_w, ln_b, w_out_bf16, norm_ffn_w, wg, wu, wd, norm_final_w,
          *, final_norm, tm=512, tff=256):
    m, d = x2.shape
    const = dict(pipeline_mode=pl.Buffered(1))
    whole = lambda a: pl.BlockSpec(a.shape, lambda i: (0,) * a.ndim, **const)
    rows = lambda n: pl.BlockSpec((tm, n), lambda i: (i, 0))
    row = lambda a: a.reshape(1, -1)
    consts = (row(ln_w), row(ln_b), w_out_bf16, row(norm_ffn_w), wg, wu, wd, row(norm_final_w))
    return pl.pallas_call(
        functools.partial(_post_kernel, tff=tff, final_norm=final_norm),
        grid=(m // tm,),
        in_specs=[rows(d), rows(RWKV_WIDTH), rows(RWKV_WIDTH), rows(RWKV_WIDTH), rows(DIFF_WIDTH)]
        + [whole(a) for a in consts],
        out_specs=rows(d),
        out_shape=jax.ShapeDtypeStruct((m, d), F32),
        compiler_params=pltpu.CompilerParams(
            dimension_semantics=("arbitrary",), vmem_limit_bytes=VMEM_LIMIT_BYTES),
        name="post",
    )(x2, yw, bonus, gate, yb, *consts)


def kernel(x, norm_mix_w, w_in, mu_shift, w0, w_lora_up, a0, a_lora_up, g_lora_up, k_k, k_a, r_k,
           ln_x_w, ln_x_b, lambda_q1, lambda_k1, lambda_q2, lambda_k2, subln_w, w_out,
           norm_ffn_w, w_gate, w_up, w_down, norm_final_w):
    batch, seq, d = x.shape
    depth = w_in.shape[0]
    h = x.reshape(batch * seq, d)
    for l in range(depth):
        lambda_init = 0.8 - 0.6 * math.exp(-0.3 * l)
        w2pad = jnp.concatenate(
            [w_lora_up[l], jnp.zeros((AAA_LORA, RWKV_WIDTH), F32)], axis=0).astype(BF16)
        a2pad = jnp.concatenate(
            [jnp.zeros((DECAY_LORA, RWKV_WIDTH), F32), a_lora_up[l]], axis=0).astype(BF16)

        qk, vt, rt, kt, bt, at, v, gam, bonus, gate = _mix_in(
            h, norm_mix_w[l], w_in[l].astype(BF16), mu_shift[l], w0[l], w2pad, a0[l], a2pad,
            g_lora_up[l].astype(BF16), k_k[l], k_a[l], r_k[l].reshape(-1), seq=seq, tm=ATTN_TILE)
        yw = _wkv(rt, kt, bt, at, v, gam, batch=batch, seq=seq)
        yb = _diff_attn(qk, vt, lambda_q1[l], lambda_k1[l], lambda_q2[l], lambda_k2[l], subln_w[l],
                        batch=batch, seq=seq, lambda_init=lambda_init)
        h = _post(h, yw, bonus, gate, yb, ln_x_w[l], ln_x_b[l], w_out[l].astype(BF16),
                  norm_ffn_w[l], w_gate[l].astype(BF16), w_up[l].astype(BF16),
                  w_down[l].astype(BF16), norm_final_w, final_norm=(l == depth - 1))
    return h.reshape(batch, seq, d)
```

```python
import functools
import math

import jax
import jax.numpy as jnp
from jax import lax
from jax.experimental import pallas as pl
from jax.experimental.pallas import tpu as pltpu

F32 = jnp.float32
BF16 = jnp.bfloat16

ATTN_CHUNK = 64
RWKV_WIDTH = 512
RWKV_HEAD = 64
RWKV_HEADS = RWKV_WIDTH // RWKV_HEAD
DECAY_LORA = 64
AAA_LORA = 64
GATE_LORA = 128
DIFF_WIDTH = 512
DIFF_HALF = 64
DIFF_VDIM = 2 * DIFF_HALF
DIFF_HEADS = DIFF_WIDTH // DIFF_VDIM
NORM_EPS = 1e-6
GN_EPS = 1e-5 * RWKV_HEAD
SUBLN_EPS = 1e-5
RWKV_COLS = 3 * RWKV_WIDTH + DECAY_LORA + AAA_LORA + GATE_LORA

WKV_CHUNK = 64
ATTN_TILE = 512
BF16_SUBLANES = 16
ATTN_HEADS_PER_STEP = 2

VMEM_LIMIT_BYTES = 56 * 1024 * 1024


def _nt(a, b):
    return lax.dot_general(a, b, (((1,), (1,)), ((), ())), preferred_element_type=F32)


def _tn(a, b):
    return lax.dot_general(a, b, (((0,), (0,)), ((), ())), preferred_element_type=F32)


def _mm(a, b):
    return jnp.dot(a, b, preferred_element_type=F32)


def _rms(x, w, eps):
    return x * lax.rsqrt(jnp.mean(x * x, axis=-1, keepdims=True) + eps) * w


def _sigmoid(x):
    return 1.0 / (1.0 + jnp.exp(-x))


def _split_dot_right(x, mat_bf16, terms):
    acc = None
    rem = x
    for _ in range(terms):
        part = rem.astype(BF16)
        d = _mm(part, mat_bf16)
        acc = d if acc is None else acc + d
        rem = rem - part.astype(F32)
    return acc


def _mix_in_kernel(x_ref, nw_ref, w_ref, wvt_ref, mu_ref, w0_ref, w2_ref, a0_ref, a2_ref, g2_ref,
                   kk_ref, ka_ref, rk_ref,
                   qk_ref, vt_ref, rt_ref, kt_ref, bt_ref, at_ref, v_ref, gam_ref, bonus_ref, g_ref,
                   carry_ref, *, tm, tiles_per_seq):
    i = pl.program_id(0)
    seq_start = (i % tiles_per_seq) == 0
    ub = _rms(x_ref[...], nw_ref[...], NORM_EPS).astype(BF16)
    w = RWKV_WIDTH

    def shifted(c0, c1):
        p = _mm(ub, w_ref[:, c0:c1])
        carry = jnp.where(seq_start, 0.0, carry_ref[:, c0:c1])
        carry_ref[:, c0:c1] = p[tm - 1:tm, :]
        row = lax.broadcasted_iota(jnp.int32, p.shape, 0)
        p_prev = jnp.where(row == 0, carry, pltpu.roll(p, 1, axis=0))
        return p + (p_prev - p) * mu_ref[:, c0:c1]

    p_r = shifted(0, w)
    p_k = shifted(w, 2 * w)
    p_v = shifted(2 * w, 3 * w)
    p_wa = shifted(3 * w, 3 * w + DECAY_LORA + AAA_LORA)
    p_g = shifted(3 * w + DECAY_LORA + AAA_LORA, RWKV_COLS)

    q = _mm(ub, w_ref[:, RWKV_COLS:RWKV_COLS + DIFF_WIDTH])
    qk_ref[:, :DIFF_WIDTH] = (q * (DIFF_HALF ** -0.5 * math.log2(math.e))).astype(BF16)
    qk_ref[:, DIFF_WIDTH:] = _mm(ub, w_ref[:, RWKV_COLS + DIFF_WIDTH:]).astype(BF16)
    vt_ref[0] = _nt(wvt_ref[...], ub).astype(BF16)

    z = w0_ref[...] + _mm(jnp.tanh(p_wa).astype(BF16), w2_ref[...])
    softplus_neg = jnp.maximum(-z, 0.0) + jnp.log(1.0 + jnp.exp(-jnp.abs(z)))
    log_decay = -jnp.exp(-softplus_neg - 0.5)
    a_sig = _sigmoid(a0_ref[...] + _mm(p_wa.astype(BF16), a2_ref[...]))
    gate = _mm(_sigmoid(p_g).astype(BF16), g2_ref[...])

    r_i = lax.broadcasted_iota(jnp.int32, (w, w), 0) // RWKV_HEAD
    c_i = lax.broadcasted_iota(jnp.int32, (w, w), 1) // RWKV_HEAD
    head_ones = (r_i == c_i).astype(BF16)

    kk = p_k * kk_ref[...]
    kk = kk * lax.rsqrt(jnp.maximum(_split_dot_right(kk * kk, head_ones, 2), 1e-24))
    k_fin = p_k * (1.0 + (a_sig - 1.0) * ka_ref[...])
    bonus = _split_dot_right(p_r * k_fin * rk_ref[...], head_ones, 2) * p_v

    pos = lax.broadcasted_iota(jnp.int32, log_decay.shape, 0) % WKV_CHUNK
    lg = log_decay
    step = 1
    while step < WKV_CHUNK:
        lg = lg + jnp.where(pos >= step, pltpu.roll(lg, step, axis=0), 0.0)
        step *= 2

    gam = jnp.exp(lg)
    gam_inv = jnp.exp(-lg)
    gam_prev = jnp.exp(lg - log_decay)
    rt = p_r * gam
    kt = k_fin * gam_inv
    bt = kk * a_sig * gam_inv
    at = -kk * gam_prev

    for h in range(RWKV_HEADS):
        sl = slice(h * RWKV_HEAD, (h + 1) * RWKV_HEAD)
        rt_ref[h] = rt[:, sl].astype(BF16)
        kt_ref[h] = kt[:, sl].astype(BF16)
        bt_ref[h] = bt[:, sl].astype(BF16)
        at_ref[h] = at[:, sl].astype(BF16)
        v_ref[h] = p_v[:, sl].astype(BF16)
        gam_ref[h] = gam[:, sl]
    bonus_ref[...] = bonus
    g_ref[...] = gate


def _mix_in(x2, norm_w, w_in_bf16, mu, w0, w2pad, a0, a2pad, g2, k_k, k_a, r_k, *, seq, tm):
    m, d = x2.shape
    w = RWKV_WIDTH
    const = dict(pipeline_mode=pl.Buffered(1))
    row = lambda a: a.reshape(1, -1)
    whole = lambda a: pl.BlockSpec(a.shape, lambda i: (0,) * a.ndim, **const)
    w_main = w_in_bf16[:, :RWKV_COLS + 2 * DIFF_WIDTH]
    w_vt = w_in_bf16[:, RWKV_COLS + 2 * DIFF_WIDTH:].T
    consts = (row(norm_w), w_main, w_vt, row(mu), row(w0), w2pad, row(a0), a2pad, g2,
              row(k_k), row(k_a), row(r_k))
    head_major = lambda dt: jax.ShapeDtypeStruct((RWKV_HEADS, m, RWKV_HEAD), dt)
    hm_spec = pl.BlockSpec((RWKV_HEADS, tm, RWKV_HEAD), lambda i: (0, i, 0))
    rows = lambda n: pl.BlockSpec((tm, n), lambda i: (i, 0))
    return pl.pallas_call(
        functools.partial(_mix_in_kernel, tm=tm, tiles_per_seq=seq // tm),
        grid=(m // tm,),
        in_specs=[rows(d)] + [whole(a) for a in consts],
        out_specs=[rows(2 * DIFF_WIDTH), pl.BlockSpec((1, DIFF_WIDTH, tm), lambda i: (i, 0, 0))]
        + [hm_spec] * 6 + [rows(w)] * 2,
        out_shape=[
            jax.ShapeDtypeStruct((m, 2 * DIFF_WIDTH), BF16),
            jax.ShapeDtypeStruct((m // tm, DIFF_WIDTH, tm), BF16),
        ] + [head_major(BF16)] * 5 + [head_major(F32)] + [jax.ShapeDtypeStruct((m, w), F32)] * 2,
        scratch_shapes=[pltpu.VMEM((1, RWKV_COLS), F32)],
        compiler_params=pltpu.CompilerParams(
            dimension_semantics=("arbitrary",), vmem_limit_bytes=VMEM_LIMIT_BYTES),
        name="mix_in",
    )(x2, *consts)


def _wkv_lanes(ref):
    heads, batch = ref.shape[0], ref.shape[1]
    return [(h, b) for b in range(batch) for h in range(heads)]


def _widen(x):
    return jnp.concatenate([x, jnp.zeros_like(x)], axis=1)


def _wkv_chunk_setup(rt_ref, kt_ref, bt_ref, at_ref, v_ref, xs_ref, ar_ref, *, nc):
    lanes = _wkv_lanes(rt_ref)
    n = len(lanes)
    c = WKV_CHUNK
    row = lax.broadcasted_iota(jnp.int32, (c, 2 * c), 0)
    col = lax.broadcasted_iota(jnp.int32, (c, 2 * c), 1)
    strict2 = row > col % c
    incl2 = row >= col % c
    zeros_b = jnp.zeros((c, c), BF16)
    bf = lambda xs: [x.astype(BF16) for x in xs]
    for ci in range(nc):
        sl = slice(ci * c, (ci + 1) * c)
        a = [at_ref[h, b, sl, :] for h, b in lanes]
        r = [rt_ref[h, b, sl, :] for h, b in lanes]
        v = [v_ref[h, b, sl, :] for h, b in lanes]
        bk = [jnp.concatenate([bt_ref[h, b, sl, :], kt_ref[h, b, sl, :]], axis=0) for h, b in lanes]
        a_a = [jnp.where(strict2, _nt(a[i], bk[i]), 0.0) for i in range(n)]
        for i in range(n):
            ar_ref[ci * n + i] = jnp.where(incl2, _nt(r[i], bk[i]), 0.0).astype(BF16)
        zv = [jnp.concatenate([zeros_b, v[i]], axis=0) for i in range(n)]
        av = [_mm(a_a[i].astype(BF16), _widen(zv[i])) for i in range(n)]
        x = [_widen(a[i]).astype(F32) + pltpu.roll(av[i], c, axis=1) for i in range(n)]
        power = bf([m[:, :c] for m in a_a])
        levels = int(math.log2(c))
        for lv in range(levels):
            if lv + 1 < levels:
                rhs = [jnp.concatenate([x[i].astype(BF16), _widen(power[i])], axis=1)
                       for i in range(n)]
                res = [_mm(power[i], rhs[i]) for i in range(n)]
                x = [x[i] + res[i][:, :2 * c] for i in range(n)]
                power = bf([res[i][:, 2 * c:3 * c] for i in range(n)])
            else:
                x = [x[i] + _mm(power[i], x[i].astype(BF16)) for i in range(n)]
        for i in range(n):
            xs_ref[ci * n + i] = x[i].astype(BF16)


def _wkv_chunk_chain(rt_ref, kt_ref, bt_ref, v_ref, gam_ref, xs_ref, ar_ref, s_ref, y_ref, *, nc):
    lanes = _wkv_lanes(rt_ref)
    n = len(lanes)
    c = WKV_CHUNK
    row = lax.broadcasted_iota(jnp.int32, (c, 2 * c), 0)
    col = lax.broadcasted_iota(jnp.int32, (c, 2 * c), 1)
    eye_hi = (col == row + c).astype(BF16)
    bf = lambda xs: [x.astype(BF16) for x in xs]
    s = [s_ref[i] for i in range(n)]
    for ci in range(nc):
        sl = slice(ci * c, (ci + 1) * c)
        r = [rt_ref[h, b, sl, :] for h, b in lanes]
        v = [v_ref[h, b, sl, :] for h, b in lanes]
        bk = [jnp.concatenate([bt_ref[h, b, sl, :], kt_ref[h, b, sl, :]], axis=0) for h, b in lanes]
        s_b = bf(s)
        s_i = [_widen(s_b[i]) + eye_hi for i in range(n)]
        u = bf([_nt(xs_ref[ci * n + i], s_i[i]) for i in range(n)])
        uv = [jnp.concatenate([u[i], v[i]], axis=0) for i in range(n)]
        s_new = [s[i] + _tn(uv[i], bk[i]) for i in range(n)]
        y = [_nt(r[i], s_b[i]) + _mm(ar_ref[ci * n + i], uv[i]) for i in range(n)]
        for i, (h, b) in enumerate(lanes):
            g_last = gam_ref[h, b, ci * c + c - 1:(ci + 1) * c, :]
            s[i] = s_new[i] * g_last
            mean = jnp.mean(y[i], axis=-1, keepdims=True)
            yc = y[i] - mean
            var = jnp.mean(yc * yc, axis=-1, keepdims=True)
            y_ref[b, sl, h * RWKV_HEAD:(h + 1) * RWKV_HEAD] = yc * lax.rsqrt(var + GN_EPS)
    for i in range(n):
        s_ref[i] = s[i]


def _seq_mix_kernel(q_ref, k_ref, vt_ref, lq1_ref, lk1_ref, lq2_ref, lk2_ref, sw_ref,
                    rt_ref, kt_ref, bt_ref, at_ref, v_ref, gam_ref,
                    o_ref, y_ref,
                    m_ref, acc_ref, st_ref, bmax_ref, s_ref, xs_ref, ar_ref,
                    *, tq, nq, nc, lambda_init):
    step = pl.program_id(0)
    qi = step % nq

    @pl.when(step == 0)
    def _():
        s_ref[...] = jnp.zeros_like(s_ref)

    hw = DIFF_VDIM
    streams = [(hd, s) for hd in range(ATTN_HEADS_PER_STEP) for s in range(2)]
    q = q_ref[...]
    lane = lax.broadcasted_iota(jnp.int32, (tq, hw), 1)
    q_half = []
    for hd, s in streams:
        qh = q[:, hd * hw:(hd + 1) * hw]
        keep = (lane < DIFF_HALF) if s == 0 else (lane >= DIFF_HALF)
        q_half.append(jnp.where(keep, qh, jnp.zeros_like(qh)))
    ones_rows = jnp.ones((acc_ref.shape[1] - hw, tq), BF16)

    m_ref[...] = jnp.full_like(m_ref, -jnp.inf)
    acc_ref[...] = jnp.zeros_like(acc_ref)

    def scores(j, slot):
        start = pl.multiple_of(j * tq, tq)
        kb = k_ref[pl.ds(start, tq), :]
        for i, (hd, s) in enumerate(streams):
            st = _nt(kb[:, hd * hw:(hd + 1) * hw], q_half[i])
            st_ref[slot, i] = st
            bmax_ref[slot, i] = jnp.max(st, axis=0, keepdims=True)

    def accumulate(j, slot, diagonal):
        vt = vt_ref[j]
        vtb = [jnp.concatenate([vt[hd * hw:(hd + 1) * hw], ones_rows], axis=0)
               for hd in range(ATTN_HEADS_PER_STEP)]
        st = [st_ref[slot, i] for i in range(len(streams))]
        if diagonal:
            ck = lax.broadcasted_iota(jnp.int32, st[0].shape, 0) // ATTN_CHUNK
            cq = lax.broadcasted_iota(jnp.int32, st[0].shape, 1) // ATTN_CHUNK
            st = [jnp.where(ck <= cq, x, -jnp.inf) for x in st]
            bmax = [jnp.max(x, axis=0, keepdims=True) for x in st]
        else:
            bmax = [bmax_ref[slot, i] for i in range(len(streams))]
        m_prev = [m_ref[i] for i in range(len(streams))]
        m_new = [jnp.maximum(a, b) for a, b in zip(m_prev, bmax)]
        p = [jnp.exp2(x - mx).astype(BF16) for x, mx in zip(st, m_new)]
        alpha = [jnp.exp2(a - b) for a, b in zip(m_prev, m_new)]
        for i, (hd, s) in enumerate(streams):
            acc_ref[i] = alpha[i] * acc_ref[i] + _mm(vtb[hd], p[i])
            m_ref[i] = m_new[i]

    _wkv_chunk_setup(rt_ref, kt_ref, bt_ref, at_ref, v_ref, xs_ref, ar_ref, nc=nc)

    scores(0, 0)

    def block_pair(pair, carry):
        t = 2 * pair
        scores(t + 1, 1)
        accumulate(t, 0, False)
        scores(t + 2, 0)
        accumulate(t + 1, 1, False)
        return carry

    lax.fori_loop(0, qi // 2, block_pair, 0)

    @pl.when(qi % 2 == 1)
    def _():
        scores(qi, 1)
        accumulate(qi - 1, 0, False)

    _wkv_chunk_chain(rt_ref, kt_ref, bt_ref, v_ref, gam_ref, xs_ref, ar_ref, s_ref, y_ref, nc=nc)
    accumulate(qi, qi % 2, True)

    lam = (jnp.exp(jnp.sum(lq1_ref[...] * lk1_ref[...], axis=-1, keepdims=True))
           - jnp.exp(jnp.sum(lq2_ref[...] * lk2_ref[...], axis=-1, keepdims=True)) + lambda_init)
    for hd in range(ATTN_HEADS_PER_STEP):
        num = [acc_ref[2 * hd + s, :hw, :] for s in range(2)]
        den = [acc_ref[2 * hd + s, hw:hw + 1, :] for s in range(2)]
        out_t = num[0] / den[0] - lam * (num[1] / den[1])
        out = _rms(out_t.T, sw_ref[...], SUBLN_EPS) * (1.0 - lambda_init)
        o_ref[:, hd * hw:(hd + 1) * hw] = out.astype(o_ref.dtype)


def _seq_mix(qk, vt, lq1, lk1, lq2, lk2, subln_w, rt, kt, bt, at, v, gam,
             *, batch, seq, lambda_init):
    m = qk.shape[0]
    tq = vt.shape[2]
    nq = seq // tq
    hs = ATTN_HEADS_PER_STEP
    groups = DIFF_HEADS // hs
    wide = hs * DIFF_VDIM
    n_streams = 2 * hs
    steps = batch * groups * nq
    heads, _, n = rt.shape
    tc = seq // steps
    assert tc % WKV_CHUNK == 0 and tc * steps == seq
    nc = tc // WKV_CHUNK
    shape4 = (heads, batch, seq, n)

    b_of = lambda s: s // (groups * nq)
    g_of = lambda s: (s // nq) % groups
    i_of = lambda s: s % nq
    const = dict(pipeline_mode=pl.Buffered(1))
    vec = lambda k: pl.BlockSpec((1, k), lambda s: (0, 0), **const)
    row = lambda a: a.reshape(1, -1)
    wkv_spec = pl.BlockSpec((heads, batch, tc, n), lambda s: (0, 0, s, 0))
    yb, yw = pl.pallas_call(
        functools.partial(_seq_mix_kernel, tq=tq, nq=nq, nc=nc, lambda_init=lambda_init),
        grid=(steps,),
        in_specs=[
            pl.BlockSpec((tq, wide), lambda s: (b_of(s) * nq + i_of(s), g_of(s))),
            pl.BlockSpec((seq, wide), lambda s: (b_of(s), groups + g_of(s))),
            pl.BlockSpec((nq, wide, tq), lambda s: (b_of(s), g_of(s), 0)),
            vec(DIFF_HALF), vec(DIFF_HALF), vec(DIFF_HALF), vec(DIFF_HALF), vec(DIFF_VDIM),
        ] + [wkv_spec] * 6,
        out_specs=[
            pl.BlockSpec((tq, wide), lambda s: (b_of(s) * nq + i_of(s), g_of(s))),
            pl.BlockSpec((batch, tc, heads * n), lambda s: (0, s, 0)),
        ],
        out_shape=[
            jax.ShapeDtypeStruct((m, DIFF_WIDTH), BF16),
            jax.ShapeDtypeStruct((batch, seq, heads * n), F32),
        ],
        scratch_shapes=[
            pltpu.VMEM((n_streams, 1, tq), F32),
            pltpu.VMEM((n_streams, DIFF_VDIM + BF16_SUBLANES, tq), F32),
            pltpu.VMEM((2, n_streams, tq, tq), F32),
            pltpu.VMEM((2, n_streams, 1, tq), F32),
            pltpu.VMEM((heads * batch, n, n), F32),
            pltpu.VMEM((nc * heads * batch, WKV_CHUNK, 2 * WKV_CHUNK), BF16),
            pltpu.VMEM((nc * heads * batch, WKV_CHUNK, 2 * WKV_CHUNK), BF16),
        ],
        compiler_params=pltpu.CompilerParams(
            dimension_semantics=("arbitrary",), vmem_limit_bytes=VMEM_LIMIT_BYTES),
        name="seq_mix",
    )(qk, qk, vt, row(lq1), row(lk1), row(lq2), row(lk2), row(subln_w),
      *[x.reshape(shape4) for x in (rt, kt, bt, at, v, gam)])
    return yb, yw.reshape(m, heads * n)


def _post_kernel(x_ref, yw_ref, bonus_ref, g_ref, yb_ref, lnw_ref, lnb_ref, wo_ref,
                 nffn_ref, wg_ref, wu_ref, wd_ref, nfin_ref, o_ref, *, tff, final_norm):
    ya = (yw_ref[...] * lnw_ref[...] + lnb_ref[...] + bonus_ref[...]) * g_ref[...]
    mixed = jnp.concatenate([ya.astype(BF16), yb_ref[...]], axis=-1)
    h1 = x_ref[...] + _mm(mixed, wo_ref[...])
    u = _rms(h1, nffn_ref[...], NORM_EPS).astype(BF16)
    acc = jnp.zeros_like(h1)
    for c0 in range(0, wg_ref.shape[1], tff):
        gt = _mm(u, wg_ref[:, c0:c0 + tff])
        up = _mm(u, wu_ref[:, c0:c0 + tff])
        act = (gt * _sigmoid(gt) * up).astype(BF16)
        acc = acc + _mm(act, wd_ref[c0:c0 + tff, :])
    h2 = h1 + acc
    if final_norm:
        h2 = _rms(h2, nfin_ref[...], NORM_EPS)
    o_ref[...] = h2


def _post(x2, yw, bonus, gate, yb, ln_w, ln_b, w_out_bf16, norm_ffn_w, wg, wu, wd, norm_final_w,
          *, final_norm, tm=512, tff=256):
    m, d = x2.shape
    const = dict(pipeline_mode=pl.Buffered(1))
    whole = lambda a: pl.BlockSpec(a.shape, lambda i: (0,) * a.ndim, **const)
    rows = lambda n: pl.BlockSpec((tm, n), lambda i: (i, 0))
    row = lambda a: a.reshape(1, -1)
    consts = (row(ln_w), row(ln_b), w_out_bf16, row(norm_ffn_w), wg, wu, wd, row(norm_final_w))
    return pl.pallas_call(
        functools.partial(_post_kernel, tff=tff, final_norm=final_norm),
        grid=(m // tm,),
        in_specs=[rows(d), rows(RWKV_WIDTH), rows(RWKV_WIDTH), rows(RWKV_WIDTH), rows(DIFF_WIDTH)]
        + [whole(a) for a in consts],
        out_specs=rows(d),
        out_shape=jax.ShapeDtypeStruct((m, d), F32),
        compiler_params=pltpu.CompilerParams(
            dimension_semantics=("arbitrary",), vmem_limit_bytes=VMEM_LIMIT_BYTES),
        name="post",
    )(x2, yw, bonus, gate, yb, *consts)


def kernel(x, norm_mix_w, w_in, mu_shift, w0, w_lora_up, a0, a_lora_up, g_lora_up, k_k, k_a, r_k,
           ln_x_w, ln_x_b, lambda_q1, lambda_k1, lambda_q2, lambda_k2, subln_w, w_out,
           norm_ffn_w, w_gate, w_up, w_down, norm_final_w):
    batch, seq, d = x.shape
    depth = w_in.shape[0]
    h = x.reshape(batch * seq, d)
    for l in range(depth):
        lambda_init = 0.8 - 0.6 * math.exp(-0.3 * l)
        w2pad = jnp.concatenate(
            [w_lora_up[l], jnp.zeros((AAA_LORA, RWKV_WIDTH), F32)], axis=0).astype(BF16)
        a2pad = jnp.concatenate(
            [jnp.zeros((DECAY_LORA, RWKV_WIDTH), F32), a_lora_up[l]], axis=0).astype(BF16)

        qk, vt, rt, kt, bt, at, v, gam, bonus, gate = _mix_in(
            h, norm_mix_w[l], w_in[l].astype(BF16), mu_shift[l], w0[l], w2pad, a0[l], a2pad,
            g_lora_up[l].astype(BF16), k_k[l], k_a[l], r_k[l].reshape(-1), seq=seq, tm=ATTN_TILE)
        yb, yw = _seq_mix(qk, vt, lambda_q1[l], lambda_k1[l], lambda_q2[l], lambda_k2[l], subln_w[l],
                          rt, kt, bt, at, v, gam, batch=batch, seq=seq, lambda_init=lambda_init)
        h = _post(h, yw, bonus, gate, yb, ln_x_w[l], ln_x_b[l], w_out[l].astype(BF16),
                  norm_ffn_w[l], w_gate[l].astype(BF16), w_up[l].astype(BF16),
                  w_down[l].astype(BF16), norm_final_w, final_norm=(l == depth - 1))
    return h.reshape(batch, seq, d)
```

```python
import functools
import math

import jax
import jax.numpy as jnp
from jax import lax
from jax.experimental import pallas as pl
from jax.experimental.pallas import tpu as pltpu

F32 = jnp.float32
BF16 = jnp.bfloat16

ATTN_CHUNK = 64
RWKV_WIDTH = 512
RWKV_HEAD = 64
RWKV_HEADS = RWKV_WIDTH // RWKV_HEAD
DECAY_LORA = 64
AAA_LORA = 64
GATE_LORA = 128
DIFF_WIDTH = 512
DIFF_HALF = 64
DIFF_VDIM = 2 * DIFF_HALF
DIFF_HEADS = DIFF_WIDTH // DIFF_VDIM
NORM_EPS = 1e-6
GN_EPS = 1e-5 * RWKV_HEAD
SUBLN_EPS = 1e-5
RWKV_COLS = 3 * RWKV_WIDTH + DECAY_LORA + AAA_LORA + GATE_LORA

WKV_CHUNK = 64
ATTN_TILE = 512
BF16_SUBLANES = 16
ATTN_HEADS_PER_STEP = 2

VMEM_LIMIT_BYTES = 56 * 1024 * 1024


def _nt(a, b):
    return lax.dot_general(a, b, (((1,), (1,)), ((), ())), preferred_element_type=F32)


def _tn(a, b):
    return lax.dot_general(a, b, (((0,), (0,)), ((), ())), preferred_element_type=F32)


def _mm(a, b):
    return jnp.dot(a, b, preferred_element_type=F32)


def _rms(x, w, eps):
    return x * lax.rsqrt(jnp.mean(x * x, axis=-1, keepdims=True) + eps) * w


def _sigmoid(x):
    return 1.0 / (1.0 + jnp.exp(-x))


def _split_dot_right(x, mat_bf16, terms):
    acc = None
    rem = x
    for _ in range(terms):
        part = rem.astype(BF16)
        d = _mm(part, mat_bf16)
        acc = d if acc is None else acc + d
        rem = rem - part.astype(F32)
    return acc


def _mix_in_kernel(x_ref, nw_ref, w_ref, mu_ref, w0_ref, w2_ref, a0_ref, a2_ref, g2_ref,
                   kk_ref, ka_ref, rk_ref,
                   qk_ref, vt_ref, rt_ref, kt_ref, bt_ref, at_ref, v_ref, gam_ref, bonus_ref, g_ref,
                   carry_ref, *, tm, tiles_per_seq):
    i = pl.program_id(0)
    seq_start = (i % tiles_per_seq) == 0
    ub = _rms(x_ref[...], nw_ref[...], NORM_EPS).astype(BF16)
    w = RWKV_WIDTH

    def shifted(c0, c1):
        p = _mm(ub, w_ref[:, c0:c1])
        carry = jnp.where(seq_start, 0.0, carry_ref[:, c0:c1])
        carry_ref[:, c0:c1] = p[tm - 1:tm, :]
        row = lax.broadcasted_iota(jnp.int32, p.shape, 0)
        p_prev = jnp.where(row == 0, carry, pltpu.roll(p, 1, axis=0))
        return p + (p_prev - p) * mu_ref[:, c0:c1]

    p_r = shifted(0, w)
    p_k = shifted(w, 2 * w)
    p_v = shifted(2 * w, 3 * w)
    p_wa = shifted(3 * w, 3 * w + DECAY_LORA + AAA_LORA)
    p_g = shifted(3 * w + DECAY_LORA + AAA_LORA, RWKV_COLS)

    q = _mm(ub, w_ref[:, RWKV_COLS:RWKV_COLS + DIFF_WIDTH])
    qk_ref[:, :DIFF_WIDTH] = (q * (DIFF_HALF ** -0.5 * math.log2(math.e))).astype(BF16)
    k0 = RWKV_COLS + DIFF_WIDTH
    qk_ref[:, DIFF_WIDTH:] = _mm(ub, w_ref[:, k0:k0 + DIFF_WIDTH]).astype(BF16)
    vt_ref[0] = _mm(ub, w_ref[:, k0 + DIFF_WIDTH:]).T.astype(BF16)

    z = w0_ref[...] + _mm(jnp.tanh(p_wa).astype(BF16), w2_ref[...])
    softplus_neg = jnp.maximum(-z, 0.0) + jnp.log(1.0 + jnp.exp(-jnp.abs(z)))
    log_decay = -jnp.exp(-softplus_neg - 0.5)
    a_sig = _sigmoid(a0_ref[...] + _mm(p_wa.astype(BF16), a2_ref[...]))
    gate = _mm(_sigmoid(p_g).astype(BF16), g2_ref[...])

    r_i = lax.broadcasted_iota(jnp.int32, (w, w), 0) // RWKV_HEAD
    c_i = lax.broadcasted_iota(jnp.int32, (w, w), 1) // RWKV_HEAD
    head_ones = (r_i == c_i).astype(BF16)

    kk = p_k * kk_ref[...]
    kk = kk * lax.rsqrt(jnp.maximum(_split_dot_right(kk * kk, head_ones, 1), 1e-24))
    k_fin = p_k * (1.0 + (a_sig - 1.0) * ka_ref[...])
    bonus = _split_dot_right(p_r * k_fin * rk_ref[...], head_ones, 2) * p_v

    pos = lax.broadcasted_iota(jnp.int32, log_decay.shape, 0) % WKV_CHUNK
    lg = log_decay
    step = 1
    while step < WKV_CHUNK:
        lg = lg + jnp.where(pos >= step, pltpu.roll(lg, step, axis=0), 0.0)
        step *= 2

    gam = jnp.exp(lg)
    gam_inv = jnp.exp(-lg)
    gam_prev = jnp.exp(lg - log_decay)
    rt = p_r * gam
    kt = k_fin * gam_inv
    bt = kk * a_sig * gam_inv
    at = -kk * gam_prev

    for h in range(RWKV_HEADS):
        sl = slice(h * RWKV_HEAD, (h + 1) * RWKV_HEAD)
        rt_ref[h] = rt[:, sl].astype(BF16)
        kt_ref[h] = kt[:, sl].astype(BF16)
        bt_ref[h] = bt[:, sl].astype(BF16)
        at_ref[h] = at[:, sl].astype(BF16)
        v_ref[h] = p_v[:, sl].astype(BF16)
        gam_ref[h] = gam[:, sl]
    bonus_ref[...] = bonus
    g_ref[...] = gate


def _mix_in(x2, norm_w, w_in_bf16, mu, w0, w2pad, a0, a2pad, g2, k_k, k_a, r_k, *, seq, tm):
    m, d = x2.shape
    w = RWKV_WIDTH
    const = dict(pipeline_mode=pl.Buffered(1))
    row = lambda a: a.reshape(1, -1)
    whole = lambda a: pl.BlockSpec(a.shape, lambda i: (0,) * a.ndim, **const)
    consts = (row(norm_w), w_in_bf16, row(mu), row(w0), w2pad, row(a0), a2pad, g2,
              row(k_k), row(k_a), row(r_k))
    head_major = lambda dt: jax.ShapeDtypeStruct((RWKV_HEADS, m, RWKV_HEAD), dt)
    hm_spec = pl.BlockSpec((RWKV_HEADS, tm, RWKV_HEAD), lambda i: (0, i, 0))
    rows = lambda n: pl.BlockSpec((tm, n), lambda i: (i, 0))
    return pl.pallas_call(
        functools.partial(_mix_in_kernel, tm=tm, tiles_per_seq=seq // tm),
        grid=(m // tm,),
        in_specs=[rows(d)] + [whole(a) for a in consts],
        out_specs=[rows(2 * DIFF_WIDTH), pl.BlockSpec((1, DIFF_WIDTH, tm), lambda i: (i, 0, 0))]
        + [hm_spec] * 6 + [rows(w)] * 2,
        out_shape=[
            jax.ShapeDtypeStruct((m, 2 * DIFF_WIDTH), BF16),
            jax.ShapeDtypeStruct((m // tm, DIFF_WIDTH, tm), BF16),
        ] + [head_major(BF16)] * 5 + [head_major(F32)] + [jax.ShapeDtypeStruct((m, w), F32)] * 2,
        scratch_shapes=[pltpu.VMEM((1, RWKV_COLS), F32)],
        compiler_params=pltpu.CompilerParams(
            dimension_semantics=("arbitrary",), vmem_limit_bytes=VMEM_LIMIT_BYTES),
        name="mix_in",
    )(x2, *consts)


def _wkv_lanes(ref):
    heads, batch = ref.shape[0], ref.shape[1]
    return [(h, b) for b in range(batch) for h in range(heads)]


def _widen(x):
    return jnp.concatenate([x, jnp.zeros_like(x)], axis=1)


def _wkv_chunk_setup(rt_ref, kt_ref, bt_ref, at_ref, v_ref, xs_ref, ar_ref, *, nc):
    lanes = _wkv_lanes(rt_ref)
    n = len(lanes)
    c = WKV_CHUNK
    row = lax.broadcasted_iota(jnp.int32, (c, 2 * c), 0)
    col = lax.broadcasted_iota(jnp.int32, (c, 2 * c), 1)
    strict2 = row > col % c
    incl2 = row >= col % c
    zeros_b = jnp.zeros((c, c), BF16)
    bf = lambda xs: [x.astype(BF16) for x in xs]
    for ci in range(nc):
        sl = slice(ci * c, (ci + 1) * c)
        a = [at_ref[h, b, sl, :] for h, b in lanes]
        r = [rt_ref[h, b, sl, :] for h, b in lanes]
        v = [v_ref[h, b, sl, :] for h, b in lanes]
        bk = [jnp.concatenate([bt_ref[h, b, sl, :], kt_ref[h, b, sl, :]], axis=0) for h, b in lanes]
        a_a = [jnp.where(strict2, _nt(a[i], bk[i]), 0.0) for i in range(n)]
        for i in range(n):
            ar_ref[ci * n + i] = jnp.where(incl2, _nt(r[i], bk[i]), 0.0).astype(BF16)
        zv = [jnp.concatenate([zeros_b, v[i]], axis=0) for i in range(n)]
        av = [_mm(a_a[i].astype(BF16), _widen(zv[i])) for i in range(n)]
        x = [_widen(a[i]).astype(F32) + pltpu.roll(av[i], c, axis=1) for i in range(n)]
        power = bf([m[:, :c] for m in a_a])
        levels = int(math.log2(c))
        for lv in range(levels):
            if lv + 1 < levels:
                rhs = [jnp.concatenate([x[i].astype(BF16), _widen(power[i])], axis=1)
                       for i in range(n)]
                res = [_mm(power[i], rhs[i]) for i in range(n)]
                x = [x[i] + res[i][:, :2 * c] for i in range(n)]
                power = bf([res[i][:, 2 * c:3 * c] for i in range(n)])
            else:
                x = [x[i] + _mm(power[i], x[i].astype(BF16)) for i in range(n)]
        for i in range(n):
            xs_ref[ci * n + i] = x[i].astype(BF16)


def _wkv_chunk_chain(rt_ref, kt_ref, bt_ref, v_ref, gam_ref, xs_ref, ar_ref, s_ref, y_ref, *, nc):
    lanes = _wkv_lanes(rt_ref)
    n = len(lanes)
    c = WKV_CHUNK
    row = lax.broadcasted_iota(jnp.int32, (c, 2 * c), 0)
    col = lax.broadcasted_iota(jnp.int32, (c, 2 * c), 1)
    eye_hi = (col == row + c).astype(BF16)
    bf = lambda xs: [x.astype(BF16) for x in xs]
    s = [s_ref[i] for i in range(n)]
    for ci in range(nc):
        sl = slice(ci * c, (ci + 1) * c)
        r = [rt_ref[h, b, sl, :] for h, b in lanes]
        v = [v_ref[h, b, sl, :] for h, b in lanes]
        bk = [jnp.concatenate([bt_ref[h, b, sl, :], kt_ref[h, b, sl, :]], axis=0) for h, b in lanes]
        s_b = bf(s)
        s_i = [_widen(s_b[i]) + eye_hi for i in range(n)]
        u = bf([_nt(xs_ref[ci * n + i], s_i[i]) for i in range(n)])
        uv = [jnp.concatenate([u[i], v[i]], axis=0) for i in range(n)]
        s_new = [s[i] + _tn(uv[i], bk[i]) for i in range(n)]
        y = [_nt(r[i], s_b[i]) + _mm(ar_ref[ci * n + i], uv[i]) for i in range(n)]
        for i, (h, b) in enumerate(lanes):
            g_last = gam_ref[h, b, ci * c + c - 1:(ci + 1) * c, :]
            s[i] = s_new[i] * g_last
            mean = jnp.mean(y[i], axis=-1, keepdims=True)
            yc = y[i] - mean
            var = jnp.mean(yc * yc, axis=-1, keepdims=True)
            y_ref[b, sl, h * RWKV_HEAD:(h + 1) * RWKV_HEAD] = yc * lax.rsqrt(var + GN_EPS)
    for i in range(n):
        s_ref[i] = s[i]


def _seq_mix_kernel(q_ref, k_ref, vt_ref, lq1_ref, lk1_ref, lq2_ref, lk2_ref, sw_ref,
                    rt_ref, kt_ref, bt_ref, at_ref, v_ref, gam_ref,
                    o_ref, y_ref,
                    m_ref, acc_ref, st_ref, bmax_ref, s_ref, xs_ref, ar_ref,
                    *, tq, nq, nc, lambda_init):
    step = pl.program_id(0)
    qi = step % nq

    @pl.when(step == 0)
    def _():
        s_ref[...] = jnp.zeros_like(s_ref)

    hw = DIFF_VDIM
    streams = [(hd, s) for hd in range(ATTN_HEADS_PER_STEP) for s in range(2)]
    q = q_ref[...]
    lane = lax.broadcasted_iota(jnp.int32, (tq, hw), 1)
    q_half = []
    for hd, s in streams:
        qh = q[:, hd * hw:(hd + 1) * hw]
        keep = (lane < DIFF_HALF) if s == 0 else (lane >= DIFF_HALF)
        q_half.append(jnp.where(keep, qh, jnp.zeros_like(qh)))
    ones_rows = jnp.ones((acc_ref.shape[1] - hw, tq), BF16)

    m_ref[...] = jnp.full_like(m_ref, -jnp.inf)
    acc_ref[...] = jnp.zeros_like(acc_ref)

    def scores(j, slot):
        start = pl.multiple_of(j * tq, tq)
        kb = k_ref[pl.ds(start, tq), :]
        for i, (hd, s) in enumerate(streams):
            st = _nt(kb[:, hd * hw:(hd + 1) * hw], q_half[i])
            st_ref[slot, i] = st
            bmax_ref[slot, i] = jnp.max(st, axis=0, keepdims=True)

    def accumulate(j, slot, diagonal):
        vt = vt_ref[j]
        vtb = [jnp.concatenate([vt[hd * hw:(hd + 1) * hw], ones_rows], axis=0)
               for hd in range(ATTN_HEADS_PER_STEP)]
        st = [st_ref[slot, i] for i in range(len(streams))]
        if diagonal:
            ck = lax.broadcasted_iota(jnp.int32, st[0].shape, 0) // ATTN_CHUNK
            cq = lax.broadcasted_iota(jnp.int32, st[0].shape, 1) // ATTN_CHUNK
            st = [jnp.where(ck <= cq, x, -jnp.inf) for x in st]
            bmax = [jnp.max(x, axis=0, keepdims=True) for x in st]
        else:
            bmax = [bmax_ref[slot, i] for i in range(len(streams))]
        m_prev = [m_ref[i] for i in range(len(streams))]
        m_new = [jnp.maximum(a, b) for a, b in zip(m_prev, bmax)]
        p = [jnp.exp2(x - mx).astype(BF16) for x, mx in zip(st, m_new)]
        alpha = [jnp.exp2(a - b) for a, b in zip(m_prev, m_new)]
        for i, (hd, s) in enumerate(streams):
            acc_ref[i] = alpha[i] * acc_ref[i] + _mm(vtb[hd], p[i])
            m_ref[i] = m_new[i]

    _wkv_chunk_setup(rt_ref, kt_ref, bt_ref, at_ref, v_ref, xs_ref, ar_ref, nc=nc)

    scores(0, 0)

    def block_pair(pair, carry):
        t = 2 * pair
        scores(t + 1, 1)
        accumulate(t, 0, False)
        scores(t + 2, 0)
        accumulate(t + 1, 1, False)
        return carry

    lax.fori_loop(0, qi // 2, block_pair, 0)

    @pl.when(qi % 2 == 1)
    def _():
        scores(qi, 1)
        accumulate(qi - 1, 0, False)

    _wkv_chunk_chain(rt_ref, kt_ref, bt_ref, v_ref, gam_ref, xs_ref, ar_ref, s_ref, y_ref, nc=nc)
    accumulate(qi, qi % 2, True)

    lam = (jnp.exp(jnp.sum(lq1_ref[...] * lk1_ref[...], axis=-1, keepdims=True))
           - jnp.exp(jnp.sum(lq2_ref[...] * lk2_ref[...], axis=-1, keepdims=True)) + lambda_init)
    for hd in range(ATTN_HEADS_PER_STEP):
        num = [acc_ref[2 * hd + s, :hw, :] for s in range(2)]
        den = [acc_ref[2 * hd + s, hw:hw + 1, :] for s in range(2)]
        out_t = num[0] / den[0] - lam * (num[1] / den[1])
        out = _rms(out_t.T, sw_ref[...], SUBLN_EPS) * (1.0 - lambda_init)
        o_ref[:, hd * hw:(hd + 1) * hw] = out.astype(o_ref.dtype)


def _seq_mix(qk, vt, lq1, lk1, lq2, lk2, subln_w, rt, kt, bt, at, v, gam,
             *, batch, seq, lambda_init):
    m = qk.shape[0]
    tq = vt.shape[2]
    nq = seq // tq
    hs = ATTN_HEADS_PER_STEP
    groups = DIFF_HEADS // hs
    wide = hs * DIFF_VDIM
    n_streams = 2 * hs
    steps = batch * groups * nq
    heads, _, n = rt.shape
    tc = seq // steps
    assert tc % WKV_CHUNK == 0 and tc * steps == seq
    nc = tc // WKV_CHUNK
    shape4 = (heads, batch, seq, n)

    b_of = lambda s: s // (groups * nq)
    g_of = lambda s: (s // nq) % groups
    i_of = lambda s: s % nq
    const = dict(pipeline_mode=pl.Buffered(1))
    vec = lambda k: pl.BlockSpec((1, k), lambda s: (0, 0), **const)
    row = lambda a: a.reshape(1, -1)
    wkv_spec = pl.BlockSpec((heads, batch, tc, n), lambda s: (0, 0, s, 0))
    yb, yw = pl.pallas_call(
        functools.partial(_seq_mix_kernel, tq=tq, nq=nq, nc=nc, lambda_init=lambda_init),
        grid=(steps,),
        in_specs=[
            pl.BlockSpec((tq, wide), lambda s: (b_of(s) * nq + i_of(s), g_of(s))),
            pl.BlockSpec((seq, wide), lambda s: (b_of(s), groups + g_of(s))),
            pl.BlockSpec((nq, wide, tq), lambda s: (b_of(s), g_of(s), 0)),
            vec(DIFF_HALF), vec(DIFF_HALF), vec(DIFF_HALF), vec(DIFF_HALF), vec(DIFF_VDIM),
        ] + [wkv_spec] * 6,
        out_specs=[
            pl.BlockSpec((tq, wide), lambda s: (b_of(s) * nq + i_of(s), g_of(s))),
            pl.BlockSpec((batch, tc, heads * n), lambda s: (0, s, 0)),
        ],
        out_shape=[
            jax.ShapeDtypeStruct((m, DIFF_WIDTH), BF16),
            jax.ShapeDtypeStruct((batch, seq, heads * n), F32),
        ],
        scratch_shapes=[
            pltpu.VMEM((n_streams, 1, tq), F32),
            pltpu.VMEM((n_streams, DIFF_VDIM + BF16_SUBLANES, tq), F32),
            pltpu.VMEM((2, n_streams, tq, tq), F32),
            pltpu.VMEM((2, n_streams, 1, tq), F32),
            pltpu.VMEM((heads * batch, n, n), F32),
            pltpu.VMEM((nc * heads * batch, WKV_CHUNK, 2 * WKV_CHUNK), BF16),
            pltpu.VMEM((nc * heads * batch, WKV_CHUNK, 2 * WKV_CHUNK), BF16),
        ],
        compiler_params=pltpu.CompilerParams(
            dimension_semantics=("arbitrary",), vmem_limit_bytes=VMEM_LIMIT_BYTES),
        name="seq_mix",
    )(qk, qk, vt, row(lq1), row(lk1), row(lq2), row(lk2), row(subln_w),
      *[x.reshape(shape4) for x in (rt, kt, bt, at, v, gam)])
    return yb, yw.reshape(m, heads * n)


def _post_kernel(x_ref, yw_ref, bonus_ref, g_ref, yb_ref, lnw_ref, lnb_ref, wo_ref,
                 nffn_ref, wg_ref, wu_ref, wd_ref, nfin_ref, o_ref, *, tff, final_norm):
    ya = (yw_ref[...] * lnw_ref[...] + lnb_ref[...] + bonus_ref[...]) * g_ref[...]
    mixed = jnp.concatenate([ya.astype(BF16), yb_ref[...]], axis=-1)
    h1 = x_ref[...] + _mm(mixed, wo_ref[...])
    u = _rms(h1, nffn_ref[...], NORM_EPS).astype(BF16)
    acc = jnp.zeros_like(h1)
    for c0 in range(0, wg_ref.shape[1], tff):
        gt = _mm(u, wg_ref[:, c0:c0 + tff])
        up = _mm(u, wu_ref[:, c0:c0 + tff])
        act = (gt * _sigmoid(gt) * up).astype(BF16)
        acc = acc + _mm(act, wd_ref[c0:c0 + tff, :])
    h2 = h1 + acc
    if final_norm:
        h2 = _rms(h2, nfin_ref[...], NORM_EPS)
    o_ref[...] = h2


def _post(x2, yw, bonus, gate, yb, ln_w, ln_b, w_out_bf16, norm_ffn_w, wg, wu, wd, norm_final_w,
          *, final_norm, tm=512, tff=256):
    m, d = x2.shape
    const = dict(pipeline_mode=pl.Buffered(1))
    whole = lambda a: pl.BlockSpec(a.shape, lambda i: (0,) * a.ndim, **const)
    rows = lambda n: pl.BlockSpec((tm, n), lambda i: (i, 0))
    row = lambda a: a.reshape(1, -1)
    consts = (row(ln_w), row(ln_b), w_out_bf16, row(norm_ffn_w), wg, wu, wd, row(norm_final_w))
    return pl.pallas_call(
        functools.partial(_post_kernel, tff=tff, final_norm=final_norm),
        grid=(m // tm,),
        in_specs=[rows(d), rows(RWKV_WIDTH), rows(RWKV_WIDTH), rows(RWKV_WIDTH), rows(DIFF_WIDTH)]
        + [whole(a) for a in consts],
        out_specs=rows(d),
        out_shape=jax.ShapeDtypeStruct((m, d), F32),
        compiler_params=pltpu.CompilerParams(
            dimension_semantics=("arbitrary",), vmem_limit_bytes=VMEM_LIMIT_BYTES),
        name="post",
    )(x2, yw, bonus, gate, yb, *consts)


def kernel(x, norm_mix_w, w_in, mu_shift, w0, w_lora_up, a0, a_lora_up, g_lora_up, k_k, k_a, r_k,
           ln_x_w, ln_x_b, lambda_q1, lambda_k1, lambda_q2, lambda_k2, subln_w, w_out,
           norm_ffn_w, w_gate, w_up, w_down, norm_final_w):
    batch, seq, d = x.shape
    depth = w_in.shape[0]
    h = x.reshape(batch * seq, d)
    for l in range(depth):
        lambda_init = 0.8 - 0.6 * math.exp(-0.3 * l)
        w2pad = jnp.concatenate(
            [w_lora_up[l], jnp.zeros((AAA_LORA, RWKV_WIDTH), F32)], axis=0).astype(BF16)
        a2pad = jnp.concatenate(
            [jnp.zeros((DECAY_LORA, RWKV_WIDTH), F32), a_lora_up[l]], axis=0).astype(BF16)

        qk, vt, rt, kt, bt, at, v, gam, bonus, gate = _mix_in(
            h, norm_mix_w[l], w_in[l].astype(BF16), mu_shift[l], w0[l], w2pad, a0[l], a2pad,
            g_lora_up[l].astype(BF16), k_k[l], k_a[l], r_k[l].reshape(-1), seq=seq, tm=ATTN_TILE)
        yb, yw = _seq_mix(qk, vt, lambda_q1[l], lambda_k1[l], lambda_q2[l], lambda_k2[l], subln_w[l],
                          rt, kt, bt, at, v, gam, batch=batch, seq=seq, lambda_init=lambda_init)
        h = _post(h, yw, bonus, gate, yb, ln_x_w[l], ln_x_b[l], w_out[l].astype(BF16),
                  norm_ffn_w[l], w_gate[l].astype(BF16), w_up[l].astype(BF16),
                  w_down[l].astype(BF16), norm_final_w, final_norm=(l == depth - 1))
    return h.reshape(batch, seq, d)
```

```python
import functools
import math

import jax
import jax.numpy as jnp
from jax import lax
from jax.experimental import pallas as pl
from jax.experimental.pallas import tpu as pltpu

F32 = jnp.float32
BF16 = jnp.bfloat16

ATTN_CHUNK = 64
RWKV_WIDTH = 512
RWKV_HEAD = 64
RWKV_HEADS = RWKV_WIDTH // RWKV_HEAD
DECAY_LORA = 64
AAA_LORA = 64
GATE_LORA = 128
DIFF_WIDTH = 512
DIFF_HALF = 64
DIFF_VDIM = 2 * DIFF_HALF
DIFF_HEADS = DIFF_WIDTH // DIFF_VDIM
NORM_EPS = 1e-6
GN_EPS = 1e-5 * RWKV_HEAD
SUBLN_EPS = 1e-5
RWKV_COLS = 3 * RWKV_WIDTH + DECAY_LORA + AAA_LORA + GATE_LORA

WKV_CHUNK = 64
ATTN_TILE = 512
BF16_SUBLANES = 16
ATTN_HEADS_PER_STEP = 2

VMEM_LIMIT_BYTES = 56 * 1024 * 1024


def _nt(a, b):
    return lax.dot_general(a, b, (((1,), (1,)), ((), ())), preferred_element_type=F32)


def _tn(a, b):
    return lax.dot_general(a, b, (((0,), (0,)), ((), ())), preferred_element_type=F32)


def _mm(a, b):
    return jnp.dot(a, b, preferred_element_type=F32)


def _rms(x, w, eps):
    return x * lax.rsqrt(jnp.mean(x * x, axis=-1, keepdims=True) + eps) * w


def _sigmoid(x):
    return 1.0 / (1.0 + jnp.exp(-x))


def _split_dot_right(x, mat_bf16, terms):
    acc = None
    rem = x
    for _ in range(terms):
        part = rem.astype(BF16)
        d = _mm(part, mat_bf16)
        acc = d if acc is None else acc + d
        rem = rem - part.astype(F32)
    return acc


def _mix_in_kernel(x_ref, nw_ref, w_ref, mu_ref, w0_ref, w2_ref, a0_ref, a2_ref, g2_ref,
                   kk_ref, ka_ref, rk_ref,
                   qk_ref, vt_ref, rt_ref, kt_ref, bt_ref, at_ref, v_ref, gam_ref, bonus_ref, g_ref,
                   carry_ref, *, tm, tiles_per_seq):
    i = pl.program_id(0)
    seq_start = (i % tiles_per_seq) == 0
    ub = _rms(x_ref[...], nw_ref[...], NORM_EPS).astype(BF16)
    w = RWKV_WIDTH

    def shifted(c0, c1):
        p = _mm(ub, w_ref[:, c0:c1])
        carry = jnp.where(seq_start, 0.0, carry_ref[:, c0:c1])
        carry_ref[:, c0:c1] = p[tm - 1:tm, :]
        row = lax.broadcasted_iota(jnp.int32, p.shape, 0)
        p_prev = jnp.where(row == 0, carry, pltpu.roll(p, 1, axis=0))
        return p + (p_prev - p) * mu_ref[:, c0:c1]

    p_wa = shifted(3 * w, 3 * w + DECAY_LORA + AAA_LORA)
    p_g = shifted(3 * w + DECAY_LORA + AAA_LORA, RWKV_COLS)
    tanh_wa = jnp.tanh(p_wa).astype(BF16)
    wa_b = p_wa.astype(BF16)
    sig_g = _sigmoid(p_g).astype(BF16)

    hw = w // 2
    r_i = lax.broadcasted_iota(jnp.int32, (hw, hw), 0) // RWKV_HEAD
    c_i = lax.broadcasted_iota(jnp.int32, (hw, hw), 1) // RWKV_HEAD
    head_ones = (r_i == c_i).astype(BF16)

    for half in range(2):
        c0 = half * hw
        cs = slice(c0, c0 + hw)
        p_r = shifted(c0, c0 + hw)
        p_k = shifted(w + c0, w + c0 + hw)
        p_v = shifted(2 * w + c0, 2 * w + c0 + hw)

        z = w0_ref[:, cs] + _mm(tanh_wa, w2_ref[:, cs])
        softplus_neg = jnp.maximum(-z, 0.0) + jnp.log(1.0 + jnp.exp(-jnp.abs(z)))
        log_decay = -jnp.exp(-softplus_neg - 0.5)
        a_sig = _sigmoid(a0_ref[:, cs] + _mm(wa_b, a2_ref[:, cs]))
        g_ref[:, cs] = _mm(sig_g, g2_ref[:, cs])

        kk = p_k * kk_ref[:, cs]
        kk = kk * lax.rsqrt(jnp.maximum(_split_dot_right(kk * kk, head_ones, 1), 1e-24))
        k_fin = p_k * (1.0 + (a_sig - 1.0) * ka_ref[:, cs])
        bonus_ref[:, cs] = _split_dot_right(p_r * k_fin * rk_ref[:, cs], head_ones, 2) * p_v

        pos = lax.broadcasted_iota(jnp.int32, log_decay.shape, 0) % WKV_CHUNK
        lg = log_decay
        step = 1
        while step < WKV_CHUNK:
            lg = lg + jnp.where(pos >= step, pltpu.roll(lg, step, axis=0), 0.0)
            step *= 2

        gam = jnp.exp(lg)
        gam_inv = jnp.exp(-lg)
        gam_prev = jnp.exp(lg - log_decay)
        outs = ((rt_ref, p_r * gam), (kt_ref, k_fin * gam_inv), (bt_ref, kk * a_sig * gam_inv),
                (at_ref, -kk * gam_prev), (v_ref, p_v), (gam_ref, gam))
        for ref, val in outs:
            for hh in range(hw // RWKV_HEAD):
                h = half * (hw // RWKV_HEAD) + hh
                ref[h] = val[:, hh * RWKV_HEAD:(hh + 1) * RWKV_HEAD].astype(ref.dtype)

    q = _mm(ub, w_ref[:, RWKV_COLS:RWKV_COLS + DIFF_WIDTH])
    qk_ref[:, :DIFF_WIDTH] = (q * (DIFF_HALF ** -0.5 * math.log2(math.e))).astype(BF16)
    k0 = RWKV_COLS + DIFF_WIDTH
    qk_ref[:, DIFF_WIDTH:] = _mm(ub, w_ref[:, k0:k0 + DIFF_WIDTH]).astype(BF16)
    vt_ref[0] = _mm(ub, w_ref[:, k0 + DIFF_WIDTH:]).T.astype(BF16)


def _mix_in(x2, norm_w, w_in_bf16, mu, w0, w2pad, a0, a2pad, g2, k_k, k_a, r_k, *, seq, tm):
    m, d = x2.shape
    w = RWKV_WIDTH
    const = dict(pipeline_mode=pl.Buffered(1))
    row = lambda a: a.reshape(1, -1)
    whole = lambda a: pl.BlockSpec(a.shape, lambda i: (0,) * a.ndim, **const)
    consts = (row(norm_w), w_in_bf16, row(mu), row(w0), w2pad, row(a0), a2pad, g2,
              row(k_k), row(k_a), row(r_k))
    head_major = lambda dt: jax.ShapeDtypeStruct((RWKV_HEADS, m, RWKV_HEAD), dt)
    hm_spec = pl.BlockSpec((RWKV_HEADS, tm, RWKV_HEAD), lambda i: (0, i, 0))
    rows = lambda n: pl.BlockSpec((tm, n), lambda i: (i, 0))
    return pl.pallas_call(
        functools.partial(_mix_in_kernel, tm=tm, tiles_per_seq=seq // tm),
        grid=(m // tm,),
        in_specs=[rows(d)] + [whole(a) for a in consts],
        out_specs=[rows(2 * DIFF_WIDTH), pl.BlockSpec((1, DIFF_WIDTH, tm), lambda i: (i, 0, 0))]
        + [hm_spec] * 6 + [rows(w)] * 2,
        out_shape=[
            jax.ShapeDtypeStruct((m, 2 * DIFF_WIDTH), BF16),
            jax.ShapeDtypeStruct((m // tm, DIFF_WIDTH, tm), BF16),
        ] + [head_major(BF16)] * 5 + [head_major(F32)] + [jax.ShapeDtypeStruct((m, w), F32)] * 2,
        scratch_shapes=[pltpu.VMEM((1, RWKV_COLS), F32)],
        compiler_params=pltpu.CompilerParams(
            dimension_semantics=("arbitrary",), vmem_limit_bytes=VMEM_LIMIT_BYTES),
        name="mix_in",
    )(x2, *consts)


def _wkv_lanes(ref):
    heads, batch = ref.shape[0], ref.shape[1]
    return [(h, b) for b in range(batch) for h in range(heads)]


def _widen(x):
    return jnp.concatenate([x, jnp.zeros_like(x)], axis=1)


def _wkv_chunk_setup(rt_ref, kt_ref, bt_ref, at_ref, v_ref, xs_ref, ar_ref, *, nc):
    lanes = _wkv_lanes(rt_ref)
    n = len(lanes)
    c = WKV_CHUNK
    row = lax.broadcasted_iota(jnp.int32, (c, 2 * c), 0)
    col = lax.broadcasted_iota(jnp.int32, (c, 2 * c), 1)
    strict2 = row > col % c
    incl2 = row >= col % c
    zeros_b = jnp.zeros((c, c), BF16)
    bf = lambda xs: [x.astype(BF16) for x in xs]
    for ci in range(nc):
        sl = slice(ci * c, (ci + 1) * c)
        a = [at_ref[h, b, sl, :] for h, b in lanes]
        r = [rt_ref[h, b, sl, :] for h, b in lanes]
        v = [v_ref[h, b, sl, :] for h, b in lanes]
        bk = [jnp.concatenate([bt_ref[h, b, sl, :], kt_ref[h, b, sl, :]], axis=0) for h, b in lanes]
        prod = [_nt(jnp.concatenate([a[i], r[i]], axis=0), bk[i]) for i in range(n)]
        a_a = [jnp.where(strict2, prod[i][:c], 0.0) for i in range(n)]
        for i in range(n):
            ar_ref[ci * n + i] = jnp.where(incl2, prod[i][c:], 0.0).astype(BF16)
        zv = [jnp.concatenate([zeros_b, v[i]], axis=0) for i in range(n)]
        av = [_mm(a_a[i].astype(BF16), _widen(zv[i])) for i in range(n)]
        x = [_widen(a[i]).astype(F32) + pltpu.roll(av[i], c, axis=1) for i in range(n)]
        power = bf([m[:, :c] for m in a_a])
        levels = int(math.log2(c))
        for lv in range(levels):
            if lv + 1 < levels:
                rhs = [jnp.concatenate([x[i].astype(BF16), _widen(power[i])], axis=1)
                       for i in range(n)]
                res = [_mm(power[i], rhs[i]) for i in range(n)]
                x = [x[i] + res[i][:, :2 * c] for i in range(n)]
                power = bf([res[i][:, 2 * c:3 * c] for i in range(n)])
            else:
                x = [x[i] + _mm(power[i], x[i].astype(BF16)) for i in range(n)]
        for i in range(n):
            xs_ref[ci * n + i] = x[i].astype(BF16)


def _wkv_chunk_chain(rt_ref, kt_ref, bt_ref, v_ref, gam_ref, xs_ref, ar_ref, s_ref, y_ref, *, nc):
    lanes = _wkv_lanes(rt_ref)
    n = len(lanes)
    c = WKV_CHUNK
    row = lax.broadcasted_iota(jnp.int32, (c, 2 * c), 0)
    col = lax.broadcasted_iota(jnp.int32, (c, 2 * c), 1)
    eye_hi = (col == row + c).astype(BF16)
    bf = lambda xs: [x.astype(BF16) for x in xs]
    s = [s_ref[i] for i in range(n)]
    for ci in range(nc):
        sl = slice(ci * c, (ci + 1) * c)
        r = [rt_ref[h, b, sl, :] for h, b in lanes]
        v = [v_ref[h, b, sl, :] for h, b in lanes]
        bk = [jnp.concatenate([bt_ref[h, b, sl, :], kt_ref[h, b, sl, :]], axis=0) for h, b in lanes]
        s_b = bf(s)
        s_i = [_widen(s_b[i]) + eye_hi for i in range(n)]
        u = bf([_nt(xs_ref[ci * n + i], s_i[i]) for i in range(n)])
        uv = [jnp.concatenate([u[i], v[i]], axis=0) for i in range(n)]
        s_new = [s[i] + _tn(uv[i], bk[i]) for i in range(n)]
        y = [_nt(r[i], s_b[i]) + _mm(ar_ref[ci * n + i], uv[i]) for i in range(n)]
        for i, (h, b) in enumerate(lanes):
            g_last = gam_ref[h, b, ci * c + c - 1:(ci + 1) * c, :]
            s[i] = s_new[i] * g_last
            mean = jnp.mean(y[i], axis=-1, keepdims=True)
            yc = y[i] - mean
            var = jnp.mean(yc * yc, axis=-1, keepdims=True)
            y_ref[b, sl, h * RWKV_HEAD:(h + 1) * RWKV_HEAD] = yc * lax.rsqrt(var + GN_EPS)
    for i in range(n):
        s_ref[i] = s[i]


def _seq_mix_kernel(q_ref, k_ref, vt_ref, lq1_ref, lk1_ref, lq2_ref, lk2_ref, sw_ref,
                    rt_ref, kt_ref, bt_ref, at_ref, v_ref, gam_ref,
                    o_ref, y_ref,
                    m_ref, acc_ref, st_ref, bmax_ref, s_ref, xs_ref, ar_ref,
                    *, tq, nq, nc, lambda_init):
    step = pl.program_id(0)
    qi = step % nq

    @pl.when(step == 0)
    def _():
        s_ref[...] = jnp.zeros_like(s_ref)

    hw = DIFF_VDIM
    streams = [(hd, s) for hd in range(ATTN_HEADS_PER_STEP) for s in range(2)]
    q = q_ref[...]
    lane = lax.broadcasted_iota(jnp.int32, (tq, hw), 1)
    q_half = []
    for hd, s in streams:
        qh = q[:, hd * hw:(hd + 1) * hw]
        keep = (lane < DIFF_HALF) if s == 0 else (lane >= DIFF_HALF)
        q_half.append(jnp.where(keep, qh, jnp.zeros_like(qh)))
    ones_rows = jnp.ones((acc_ref.shape[1] - hw, tq), BF16)

    m_ref[...] = jnp.full_like(m_ref, -jnp.inf)
    acc_ref[...] = jnp.zeros_like(acc_ref)

    def scores(j, slot):
        start = pl.multiple_of(j * tq, tq)
        kb = k_ref[pl.ds(start, tq), :]
        for i, (hd, s) in enumerate(streams):
            st = _nt(kb[:, hd * hw:(hd + 1) * hw], q_half[i])
            st_ref[slot, i] = st
            bmax_ref[slot, i] = jnp.max(st, axis=0, keepdims=True)

    def accumulate(j, slot, diagonal):
        vt = vt_ref[j]
        vtb = [jnp.concatenate([vt[hd * hw:(hd + 1) * hw], ones_rows], axis=0)
               for hd in range(ATTN_HEADS_PER_STEP)]
        st = [st_ref[slot, i] for i in range(len(streams))]
        if diagonal:
            ck = lax.broadcasted_iota(jnp.int32, st[0].shape, 0) // ATTN_CHUNK
            cq = lax.broadcasted_iota(jnp.int32, st[0].shape, 1) // ATTN_CHUNK
            st = [jnp.where(ck <= cq, x, -jnp.inf) for x in st]
            bmax = [jnp.max(x, axis=0, keepdims=True) for x in st]
        else:
            bmax = [bmax_ref[slot, i] for i in range(len(streams))]
        m_prev = [m_ref[i] for i in range(len(streams))]
        m_new = [jnp.maximum(a, b) for a, b in zip(m_prev, bmax)]
        p = [jnp.exp2(x - mx).astype(BF16) for x, mx in zip(st, m_new)]
        alpha = [jnp.exp2(a - b) for a, b in zip(m_prev, m_new)]
        for i, (hd, s) in enumerate(streams):
            acc_ref[i] = alpha[i] * acc_ref[i] + _mm(vtb[hd], p[i])
            m_ref[i] = m_new[i]

    _wkv_chunk_setup(rt_ref, kt_ref, bt_ref, at_ref, v_ref, xs_ref, ar_ref, nc=nc)

    scores(0, 0)

    def block_pair(pair, carry):
        t = 2 * pair
        scores(t + 1, 1)
        accumulate(t, 0, False)
        scores(t + 2, 0)
        accumulate(t + 1, 1, False)
        return carry

    lax.fori_loop(0, qi // 2, block_pair, 0)

    @pl.when(qi % 2 == 1)
    def _():
        scores(qi, 1)
        accumulate(qi - 1, 0, False)

    _wkv_chunk_chain(rt_ref, kt_ref, bt_ref, v_ref, gam_ref, xs_ref, ar_ref, s_ref, y_ref, nc=nc)
    accumulate(qi, qi % 2, True)

    lam = (jnp.exp(jnp.sum(lq1_ref[...] * lk1_ref[...], axis=-1, keepdims=True))
           - jnp.exp(jnp.sum(lq2_ref[...] * lk2_ref[...], axis=-1, keepdims=True)) + lambda_init)
    for hd in range(ATTN_HEADS_PER_STEP):
        num = [acc_ref[2 * hd + s, :hw, :] for s in range(2)]
        den = [acc_ref[2 * hd + s, hw:hw + 1, :] for s in range(2)]
        out_t = num[0] / den[0] - lam * (num[1] / den[1])
        out = _rms(out_t.T, sw_ref[...], SUBLN_EPS) * (1.0 - lambda_init)
        o_ref[:, hd * hw:(hd + 1) * hw] = out.astype(o_ref.dtype)


def _seq_mix(qk, vt, lq1, lk1, lq2, lk2, subln_w, rt, kt, bt, at, v, gam,
             *, batch, seq, lambda_init):
    m = qk.shape[0]
    tq = vt.shape[2]
    nq = seq // tq
    hs = ATTN_HEADS_PER_STEP
    groups = DIFF_HEADS // hs
    wide = hs * DIFF_VDIM
    n_streams = 2 * hs
    steps = batch * groups * nq
    heads, _, n = rt.shape
    tc = seq // steps
    assert tc % WKV_CHUNK == 0 and tc * steps == seq
    nc = tc // WKV_CHUNK
    shape4 = (heads, batch, seq, n)

    b_of = lambda s: s // (groups * nq)
    g_of = lambda s: (s // nq) % groups
    i_of = lambda s: s % nq
    const = dict(pipeline_mode=pl.Buffered(1))
    vec = lambda k: pl.BlockSpec((1, k), lambda s: (0, 0), **const)
    row = lambda a: a.reshape(1, -1)
    wkv_spec = pl.BlockSpec((heads, batch, tc, n), lambda s: (0, 0, s, 0))
    yb, yw = pl.pallas_call(
        functools.partial(_seq_mix_kernel, tq=tq, nq=nq, nc=nc, lambda_init=lambda_init),
        grid=(steps,),
        in_specs=[
            pl.BlockSpec((tq, wide), lambda s: (b_of(s) * nq + i_of(s), g_of(s))),
            pl.BlockSpec((seq, wide), lambda s: (b_of(s), groups + g_of(s))),
            pl.BlockSpec((nq, wide, tq), lambda s: (b_of(s), g_of(s), 0)),
            vec(DIFF_HALF), vec(DIFF_HALF), vec(DIFF_HALF), vec(DIFF_HALF), vec(DIFF_VDIM),
        ] + [wkv_spec] * 6,
        out_specs=[
            pl.BlockSpec((tq, wide), lambda s: (b_of(s) * nq + i_of(s), g_of(s))),
            pl.BlockSpec((batch, tc, heads * n), lambda s: (0, s, 0)),
        ],
        out_shape=[
            jax.ShapeDtypeStruct((m, DIFF_WIDTH), BF16),
            jax.ShapeDtypeStruct((batch, seq, heads * n), F32),
        ],
        scratch_shapes=[
            pltpu.VMEM((n_streams, 1, tq), F32),
            pltpu.VMEM((n_streams, DIFF_VDIM + BF16_SUBLANES, tq), F32),
            pltpu.VMEM((2, n_streams, tq, tq), F32),
            pltpu.VMEM((2, n_streams, 1, tq), F32),
            pltpu.VMEM((heads * batch, n, n), F32),
            pltpu.VMEM((nc * heads * batch, WKV_CHUNK, 2 * WKV_CHUNK), BF16),
            pltpu.VMEM((nc * heads * batch, WKV_CHUNK, 2 * WKV_CHUNK), BF16),
        ],
        compiler_params=pltpu.CompilerParams(
            dimension_semantics=("arbitrary",), vmem_limit_bytes=VMEM_LIMIT_BYTES),
        name="seq_mix",
    )(qk, qk, vt, row(lq1), row(lk1), row(lq2), row(lk2), row(subln_w),
      *[x.reshape(shape4) for x in (rt, kt, bt, at, v, gam)])
    return yb, yw.reshape(m, heads * n)


def _post_kernel(x_ref, yw_ref, bonus_ref, g_ref, yb_ref, lnw_ref, lnb_ref, wo_ref,
                 nffn_ref, wg_ref, wu_ref, wd_ref, nfin_ref, o_ref, *, tff, final_norm):
    ya = (yw_ref[...] * lnw_ref[...] + lnb_ref[...] + bonus_ref[...]) * g_ref[...]
    mixed = jnp.concatenate([ya.astype(BF16), yb_ref[...]], axis=-1)
    h1 = x_ref[...] + _mm(mixed, wo_ref[...])
    u = _rms(h1, nffn_ref[...], NORM_EPS).astype(BF16)
    acc = jnp.zeros_like(h1)
    for c0 in range(0, wg_ref.shape[1], tff):
        gt = _mm(u, wg_ref[:, c0:c0 + tff])
        up = _mm(u, wu_ref[:, c0:c0 + tff])
        act = (gt * _sigmoid(gt) * up).astype(BF16)
        acc = acc + _mm(act, wd_ref[c0:c0 + tff, :])
    h2 = h1 + acc
    if final_norm:
        h2 = _rms(h2, nfin_ref[...], NORM_EPS)
    o_ref[...] = h2


def _post(x2, yw, bonus, gate, yb, ln_w, ln_b, w_out_bf16, norm_ffn_w, wg, wu, wd, norm_final_w,
          *, final_norm, tm=512, tff=256):
    m, d = x2.shape
    const = dict(pipeline_mode=pl.Buffered(1))
    whole = lambda a: pl.BlockSpec(a.shape, lambda i: (0,) * a.ndim, **const)
    rows = lambda n: pl.BlockSpec((tm, n), lambda i: (i, 0))
    row = lambda a: a.reshape(1, -1)
    consts = (row(ln_w), row(ln_b), w_out_bf16, row(norm_ffn_w), wg, wu, wd, row(norm_final_w))
    return pl.pallas_call(
        functools.partial(_post_kernel, tff=tff, final_norm=final_norm),
        grid=(m // tm,),
        in_specs=[rows(d), rows(RWKV_WIDTH), rows(RWKV_WIDTH), rows(RWKV_WIDTH), rows(DIFF_WIDTH)]
        + [whole(a) for a in consts],
        out_specs=rows(d),
        out_shape=jax.ShapeDtypeStruct((m, d), F32),
        compiler_params=pltpu.CompilerParams(
            dimension_semantics=("arbitrary",), vmem_limit_bytes=VMEM_LIMIT_BYTES),
        name="post",
    )(x2, yw, bonus, gate, yb, *consts)


def kernel(x, norm_mix_w, w_in, mu_shift, w0, w_lora_up, a0, a_lora_up, g_lora_up, k_k, k_a, r_k,
           ln_x_w, ln_x_b, lambda_q1, lambda_k1, lambda_q2, lambda_k2, subln_w, w_out,
           norm_ffn_w, w_gate, w_up, w_down, norm_final_w):
    batch, seq, d = x.shape
    depth = w_in.shape[0]
    h = x.reshape(batch * seq, d)
    for l in range(depth):
        lambda_init = 0.8 - 0.6 * math.exp(-0.3 * l)
        w2pad = jnp.concatenate(
            [w_lora_up[l], jnp.zeros((AAA_LORA, RWKV_WIDTH), F32)], axis=0).astype(BF16)
        a2pad = jnp.concatenate(
            [jnp.zeros((DECAY_LORA, RWKV_WIDTH), F32), a_lora_up[l]], axis=0).astype(BF16)

        qk, vt, rt, kt, bt, at, v, gam, bonus, gate = _mix_in(
            h, norm_mix_w[l], w_in[l].astype(BF16), mu_shift[l], w0[l], w2pad, a0[l], a2pad,
            g_lora_up[l].astype(BF16), k_k[l], k_a[l], r_k[l].reshape(-1), seq=seq, tm=ATTN_TILE)
        yb, yw = _seq_mix(qk, vt, lambda_q1[l], lambda_k1[l], lambda_q2[l], lambda_k2[l], subln_w[l],
                          rt, kt, bt, at, v, gam, batch=batch, seq=seq, lambda_init=lambda_init)
        h = _post(h, yw, bonus, gate, yb, ln_x_w[l], ln_x_b[l], w_out[l].astype(BF16),
                  norm_ffn_w[l], w_gate[l].astype(BF16), w_up[l].astype(BF16),
                  w_down[l].astype(BF16), norm_final_w, final_norm=(l == depth - 1))
    return h.reshape(batch, seq, d)
```

```python
import functools
import math

import jax
import jax.numpy as jnp
from jax import lax
from jax.experimental import pallas as pl
from jax.experimental.pallas import tpu as pltpu

F32 = jnp.float32
BF16 = jnp.bfloat16

ATTN_CHUNK = 64
RWKV_WIDTH = 512
RWKV_HEAD = 64
RWKV_HEADS = RWKV_WIDTH // RWKV_HEAD
DECAY_LORA = 64
AAA_LORA = 64
GATE_LORA = 128
DIFF_WIDTH = 512
DIFF_HALF = 64
DIFF_VDIM = 2 * DIFF_HALF
DIFF_HEADS = DIFF_WIDTH // DIFF_VDIM
NORM_EPS = 1e-6
GN_EPS = 1e-5 * RWKV_HEAD
SUBLN_EPS = 1e-5
RWKV_COLS = 3 * RWKV_WIDTH + DECAY_LORA + AAA_LORA + GATE_LORA

WKV_CHUNK = 64
ATTN_TILE = 512
BF16_SUBLANES = 16
ATTN_HEADS_PER_STEP = 2

VMEM_LIMIT_BYTES = 56 * 1024 * 1024


def _nt(a, b):
    return lax.dot_general(a, b, (((1,), (1,)), ((), ())), preferred_element_type=F32)


def _tn(a, b):
    return lax.dot_general(a, b, (((0,), (0,)), ((), ())), preferred_element_type=F32)


def _mm(a, b):
    return jnp.dot(a, b, preferred_element_type=F32)


def _rms(x, w, eps):
    return x * lax.rsqrt(jnp.mean(x * x, axis=-1, keepdims=True) + eps) * w


def _sigmoid(x):
    return 1.0 / (1.0 + jnp.exp(-x))


def _split_dot_right(x, mat_bf16, terms):
    acc = None
    rem = x
    for _ in range(terms):
        part = rem.astype(BF16)
        d = _mm(part, mat_bf16)
        acc = d if acc is None else acc + d
        rem = rem - part.astype(F32)
    return acc


def _mix_in_kernel(x_ref, nw_ref, w_ref, mu_ref, w0_ref, w2_ref, a0_ref, a2_ref, g2_ref,
                   kk_ref, ka_ref, rk_ref,
                   qk_ref, vt_ref, rt_ref, kt_ref, bt_ref, at_ref, v_ref, gam_ref, bonus_ref, g_ref,
                   carry_ref, *, tm, tiles_per_seq):
    i = pl.program_id(0)
    seq_start = (i % tiles_per_seq) == 0
    ub = _rms(x_ref[...], nw_ref[...], NORM_EPS).astype(BF16)
    w = RWKV_WIDTH

    def shifted(c0, c1):
        p = _mm(ub, w_ref[:, c0:c1])
        carry = jnp.where(seq_start, 0.0, carry_ref[:, c0:c1])
        carry_ref[:, c0:c1] = p[tm - 1:tm, :]
        row = lax.broadcasted_iota(jnp.int32, p.shape, 0)
        p_prev = jnp.where(row == 0, carry, pltpu.roll(p, 1, axis=0))
        return p + (p_prev - p) * mu_ref[:, c0:c1]

    p_wa = shifted(3 * w, 3 * w + DECAY_LORA + AAA_LORA)
    p_g = shifted(3 * w + DECAY_LORA + AAA_LORA, RWKV_COLS)
    tanh_wa = jnp.tanh(p_wa).astype(BF16)
    wa_b = p_wa.astype(BF16)
    sig_g = _sigmoid(p_g).astype(BF16)

    hw = w // 2
    r_i = lax.broadcasted_iota(jnp.int32, (hw, hw), 0) // RWKV_HEAD
    c_i = lax.broadcasted_iota(jnp.int32, (hw, hw), 1) // RWKV_HEAD
    head_ones = (r_i == c_i).astype(BF16)

    for half in range(2):
        c0 = half * hw
        cs = slice(c0, c0 + hw)
        p_r = shifted(c0, c0 + hw)
        p_k = shifted(w + c0, w + c0 + hw)
        p_v = shifted(2 * w + c0, 2 * w + c0 + hw)

        z = w0_ref[:, cs] + _mm(tanh_wa, w2_ref[:, cs])
        softplus_neg = jnp.maximum(-z, 0.0) + jnp.log(1.0 + jnp.exp(-jnp.abs(z)))
        log_decay = -jnp.exp(-softplus_neg - 0.5)
        a_sig = _sigmoid(a0_ref[:, cs] + _mm(wa_b, a2_ref[:, cs]))
        g_ref[:, cs] = _mm(sig_g, g2_ref[:, cs])

        kk = p_k * kk_ref[:, cs]
        kk = kk * lax.rsqrt(jnp.maximum(_split_dot_right(kk * kk, head_ones, 1), 1e-24))
        k_fin = p_k * (1.0 + (a_sig - 1.0) * ka_ref[:, cs])
        bonus_ref[:, cs] = _split_dot_right(p_r * k_fin * rk_ref[:, cs], head_ones, 2) * p_v

        pos = lax.broadcasted_iota(jnp.int32, log_decay.shape, 0) % WKV_CHUNK
        lg = log_decay
        step = 1
        while step < WKV_CHUNK:
            lg = lg + jnp.where(pos >= step, pltpu.roll(lg, step, axis=0), 0.0)
            step *= 2

        gam = jnp.exp(lg)
        gam_inv = jnp.exp(-lg)
        gam_prev = jnp.exp(lg - log_decay)
        outs = ((rt_ref, p_r * gam), (kt_ref, k_fin * gam_inv), (bt_ref, kk * a_sig * gam_inv),
                (at_ref, -kk * gam_prev), (v_ref, p_v), (gam_ref, gam))
        for ref, val in outs:
            for hh in range(hw // RWKV_HEAD):
                h = half * (hw // RWKV_HEAD) + hh
                ref[h] = val[:, hh * RWKV_HEAD:(hh + 1) * RWKV_HEAD].astype(ref.dtype)

    q = _mm(ub, w_ref[:, RWKV_COLS:RWKV_COLS + DIFF_WIDTH])
    qk_ref[:, :DIFF_WIDTH] = (q * (DIFF_HALF ** -0.5 * math.log2(math.e))).astype(BF16)
    k0 = RWKV_COLS + DIFF_WIDTH
    qk_ref[:, DIFF_WIDTH:] = _mm(ub, w_ref[:, k0:k0 + DIFF_WIDTH]).astype(BF16)
    vt_ref[0] = _mm(ub, w_ref[:, k0 + DIFF_WIDTH:]).T.astype(BF16)


def _mix_in(x2, norm_w, w_in_bf16, mu, w0, w2pad, a0, a2pad, g2, k_k, k_a, r_k, *, seq, tm):
    m, d = x2.shape
    w = RWKV_WIDTH
    const = dict(pipeline_mode=pl.Buffered(1))
    row = lambda a: a.reshape(1, -1)
    whole = lambda a: pl.BlockSpec(a.shape, lambda i: (0,) * a.ndim, **const)
    consts = (row(norm_w), w_in_bf16, row(mu), row(w0), w2pad, row(a0), a2pad, g2,
              row(k_k), row(k_a), row(r_k))
    head_major = lambda dt: jax.ShapeDtypeStruct((RWKV_HEADS, m, RWKV_HEAD), dt)
    hm_spec = pl.BlockSpec((RWKV_HEADS, tm, RWKV_HEAD), lambda i: (0, i, 0))
    rows = lambda n: pl.BlockSpec((tm, n), lambda i: (i, 0))
    return pl.pallas_call(
        functools.partial(_mix_in_kernel, tm=tm, tiles_per_seq=seq // tm),
        grid=(m // tm,),
        in_specs=[rows(d)] + [whole(a) for a in consts],
        out_specs=[rows(2 * DIFF_WIDTH), pl.BlockSpec((1, DIFF_WIDTH, tm), lambda i: (i, 0, 0))]
        + [hm_spec] * 6 + [rows(w)] * 2,
        out_shape=[
            jax.ShapeDtypeStruct((m, 2 * DIFF_WIDTH), BF16),
            jax.ShapeDtypeStruct((m // tm, DIFF_WIDTH, tm), BF16),
        ] + [head_major(BF16)] * 5 + [head_major(F32)] + [jax.ShapeDtypeStruct((m, w), F32)] * 2,
        scratch_shapes=[pltpu.VMEM((1, RWKV_COLS), F32)],
        compiler_params=pltpu.CompilerParams(
            dimension_semantics=("arbitrary",), vmem_limit_bytes=VMEM_LIMIT_BYTES),
        name="mix_in",
    )(x2, *consts)


def _wkv_lanes(ref):
    heads, batch = ref.shape[0], ref.shape[1]
    return [(h, b) for b in range(batch) for h in range(heads)]


def _widen(x):
    return jnp.concatenate([x, jnp.zeros_like(x)], axis=1)


def _wkv_chunk_setup(rt_ref, kt_ref, bt_ref, at_ref, v_ref, xs_ref, ar_ref, *, nc):
    lanes = _wkv_lanes(rt_ref)
    c = WKV_CHUNK
    row = lax.broadcasted_iota(jnp.int32, (c, 2 * c), 0)
    col = lax.broadcasted_iota(jnp.int32, (c, 2 * c), 1)
    strict2 = row > col % c
    incl2 = row >= col % c
    zeros_b = jnp.zeros((c, c), BF16)
    bf = lambda xs: [x.astype(BF16) for x in xs]
    items = [(slice(ci * c, (ci + 1) * c), h, b) for ci in range(nc) for h, b in lanes]
    m = len(items)
    a = [at_ref[h, b, sl, :] for sl, h, b in items]
    r = [rt_ref[h, b, sl, :] for sl, h, b in items]
    v = [v_ref[h, b, sl, :] for sl, h, b in items]
    bk = [jnp.concatenate([bt_ref[h, b, sl, :], kt_ref[h, b, sl, :]], axis=0) for sl, h, b in items]
    prod = [_nt(jnp.concatenate([a[j], r[j]], axis=0), bk[j]) for j in range(m)]
    a_a = [jnp.where(strict2, prod[j][:c], 0.0) for j in range(m)]
    for j in range(m):
        ar_ref[j] = jnp.where(incl2, prod[j][c:], 0.0).astype(BF16)
    zv = [jnp.concatenate([zeros_b, v[j]], axis=0) for j in range(m)]
    av = [_mm(a_a[j].astype(BF16), _widen(zv[j])) for j in range(m)]
    x = [_widen(a[j]).astype(F32) + pltpu.roll(av[j], c, axis=1) for j in range(m)]
    power = bf([mat[:, :c] for mat in a_a])
    levels = int(math.log2(c))
    for lv in range(levels):
        if lv + 1 < levels:
            rhs = [jnp.concatenate([x[j].astype(BF16), _widen(power[j])], axis=1)
                   for j in range(m)]
            res = [_mm(power[j], rhs[j]) for j in range(m)]
            x = [x[j] + res[j][:, :2 * c] for j in range(m)]
            power = bf([res[j][:, 2 * c:3 * c] for j in range(m)])
        else:
            x = [x[j] + _mm(power[j], x[j].astype(BF16)) for j in range(m)]
    for j in range(m):
        xs_ref[j] = x[j].astype(BF16)


def _wkv_chunk_chain(rt_ref, kt_ref, bt_ref, v_ref, gam_ref, xs_ref, ar_ref, s_ref, y_ref, *, nc):
    lanes = _wkv_lanes(rt_ref)
    n = len(lanes)
    c = WKV_CHUNK
    row = lax.broadcasted_iota(jnp.int32, (c, 2 * c), 0)
    col = lax.broadcasted_iota(jnp.int32, (c, 2 * c), 1)
    eye_hi = (col == row + c).astype(BF16)
    bf = lambda xs: [x.astype(BF16) for x in xs]
    s = [s_ref[i] for i in range(n)]
    for ci in range(nc):
        sl = slice(ci * c, (ci + 1) * c)
        r = [rt_ref[h, b, sl, :] for h, b in lanes]
        v = [v_ref[h, b, sl, :] for h, b in lanes]
        bk = [jnp.concatenate([bt_ref[h, b, sl, :], kt_ref[h, b, sl, :]], axis=0) for h, b in lanes]
        s_b = bf(s)
        s_i = [_widen(s_b[i]) + eye_hi for i in range(n)]
        u = bf([_nt(xs_ref[ci * n + i], s_i[i]) for i in range(n)])
        uv = [jnp.concatenate([u[i], v[i]], axis=0) for i in range(n)]
        s_new = [s[i] + _tn(uv[i], bk[i]) for i in range(n)]
        y = [_nt(r[i], s_b[i]) + _mm(ar_ref[ci * n + i], uv[i]) for i in range(n)]
        for i, (h, b) in enumerate(lanes):
            g_last = gam_ref[h, b, ci * c + c - 1:(ci + 1) * c, :]
            s[i] = s_new[i] * g_last
            mean = jnp.mean(y[i], axis=-1, keepdims=True)
            yc = y[i] - mean
            var = jnp.mean(yc * yc, axis=-1, keepdims=True)
            y_ref[b, sl, h * RWKV_HEAD:(h + 1) * RWKV_HEAD] = yc * lax.rsqrt(var + GN_EPS)
    for i in range(n):
        s_ref[i] = s[i]


def _seq_mix_kernel(q_ref, k_ref, vt_ref, lq1_ref, lk1_ref, lq2_ref, lk2_ref, sw_ref,
                    rt_ref, kt_ref, bt_ref, at_ref, v_ref, gam_ref,
                    o_ref, y_ref,
                    m_ref, acc_ref, st_ref, bmax_ref, s_ref, xs_ref, ar_ref,
                    *, tq, nq, nc, lambda_init):
    step = pl.program_id(0)
    qi = step % nq

    @pl.when(step == 0)
    def _():
        s_ref[...] = jnp.zeros_like(s_ref)

    hw = DIFF_VDIM
    streams = [(hd, s) for hd in range(ATTN_HEADS_PER_STEP) for s in range(2)]
    q = q_ref[...]
    lane = lax.broadcasted_iota(jnp.int32, (tq, hw), 1)
    q_half = []
    for hd, s in streams:
        qh = q[:, hd * hw:(hd + 1) * hw]
        keep = (lane < DIFF_HALF) if s == 0 else (lane >= DIFF_HALF)
        q_half.append(jnp.where(keep, qh, jnp.zeros_like(qh)))
    ones_rows = jnp.ones((acc_ref.shape[1] - hw, tq), BF16)

    m_ref[...] = jnp.full_like(m_ref, -jnp.inf)
    acc_ref[...] = jnp.zeros_like(acc_ref)

    def scores(j, slot):
        start = pl.multiple_of(j * tq, tq)
        kb = k_ref[pl.ds(start, tq), :]
        for i, (hd, s) in enumerate(streams):
            st = _nt(kb[:, hd * hw:(hd + 1) * hw], q_half[i])
            st_ref[slot, i] = st
            bmax_ref[slot, i] = jnp.max(st, axis=0, keepdims=True)

    def accumulate(j, slot, diagonal):
        vt = vt_ref[j]
        vtb = [jnp.concatenate([vt[hd * hw:(hd + 1) * hw], ones_rows], axis=0)
               for hd in range(ATTN_HEADS_PER_STEP)]
        st = [st_ref[slot, i] for i in range(len(streams))]
        if diagonal:
            ck = lax.broadcasted_iota(jnp.int32, st[0].shape, 0) // ATTN_CHUNK
            cq = lax.broadcasted_iota(jnp.int32, st[0].shape, 1) // ATTN_CHUNK
            st = [jnp.where(ck <= cq, x, -jnp.inf) for x in st]
            bmax = [jnp.max(x, axis=0, keepdims=True) for x in st]
        else:
            bmax = [bmax_ref[slot, i] for i in range(len(streams))]
        m_prev = [m_ref[i] for i in range(len(streams))]
        m_new = [jnp.maximum(a, b) for a, b in zip(m_prev, bmax)]
        p = [jnp.exp2(x - mx).astype(BF16) for x, mx in zip(st, m_new)]
        alpha = [jnp.exp2(a - b) for a, b in zip(m_prev, m_new)]
        for i, (hd, s) in enumerate(streams):
            acc_ref[i] = alpha[i] * acc_ref[i] + _mm(vtb[hd], p[i])
            m_ref[i] = m_new[i]

    _wkv_chunk_setup(rt_ref, kt_ref, bt_ref, at_ref, v_ref, xs_ref, ar_ref, nc=nc)

    scores(0, 0)

    def block_pair(pair, carry):
        t = 2 * pair
        scores(t + 1, 1)
        accumulate(t, 0, False)
        scores(t + 2, 0)
        accumulate(t + 1, 1, False)
        return carry

    lax.fori_loop(0, qi // 2, block_pair, 0)

    @pl.when(qi % 2 == 1)
    def _():
        scores(qi, 1)
        accumulate(qi - 1, 0, False)

    _wkv_chunk_chain(rt_ref, kt_ref, bt_ref, v_ref, gam_ref, xs_ref, ar_ref, s_ref, y_ref, nc=nc)
    accumulate(qi, qi % 2, True)

    lam = (jnp.exp(jnp.sum(lq1_ref[...] * lk1_ref[...], axis=-1, keepdims=True))
           - jnp.exp(jnp.sum(lq2_ref[...] * lk2_ref[...], axis=-1, keepdims=True)) + lambda_init)
    for hd in range(ATTN_HEADS_PER_STEP):
        num = [acc_ref[2 * hd + s, :hw, :] for s in range(2)]
        den = [acc_ref[2 * hd + s, hw:hw + 1, :] for s in range(2)]
        out_t = num[0] / den[0] - lam * (num[1] / den[1])
        out = _rms(out_t.T, sw_ref[...], SUBLN_EPS) * (1.0 - lambda_init)
        o_ref[:, hd * hw:(hd + 1) * hw] = out.astype(o_ref.dtype)


def _seq_mix(qk, vt, lq1, lk1, lq2, lk2, subln_w, rt, kt, bt, at, v, gam,
             *, batch, seq, lambda_init):
    m = qk.shape[0]
    tq = vt.shape[2]
    nq = seq // tq
    hs = ATTN_HEADS_PER_STEP
    groups = DIFF_HEADS // hs
    wide = hs * DIFF_VDIM
    n_streams = 2 * hs
    steps = batch * groups * nq
    heads, _, n = rt.shape
    tc = seq // steps
    assert tc % WKV_CHUNK == 0 and tc * steps == seq
    nc = tc // WKV_CHUNK
    shape4 = (heads, batch, seq, n)

    b_of = lambda s: s // (groups * nq)
    g_of = lambda s: (s // nq) % groups
    i_of = lambda s: s % nq
    const = dict(pipeline_mode=pl.Buffered(1))
    vec = lambda k: pl.BlockSpec((1, k), lambda s: (0, 0), **const)
    row = lambda a: a.reshape(1, -1)
    wkv_spec = pl.BlockSpec((heads, batch, tc, n), lambda s: (0, 0, s, 0))
    yb, yw = pl.pallas_call(
        functools.partial(_seq_mix_kernel, tq=tq, nq=nq, nc=nc, lambda_init=lambda_init),
        grid=(steps,),
        in_specs=[
            pl.BlockSpec((tq, wide), lambda s: (b_of(s) * nq + i_of(s), g_of(s))),
            pl.BlockSpec((seq, wide), lambda s: (b_of(s), groups + g_of(s))),
            pl.BlockSpec((nq, wide, tq), lambda s: (b_of(s), g_of(s), 0)),
            vec(DIFF_HALF), vec(DIFF_HALF), vec(DIFF_HALF), vec(DIFF_HALF), vec(DIFF_VDIM),
        ] + [wkv_spec] * 6,
        out_specs=[
            pl.BlockSpec((tq, wide), lambda s: (b_of(s) * nq + i_of(s), g_of(s))),
            pl.BlockSpec((batch, tc, heads * n), lambda s: (0, s, 0)),
        ],
        out_shape=[
            jax.ShapeDtypeStruct((m, DIFF_WIDTH), BF16),
            jax.ShapeDtypeStruct((batch, seq, heads * n), F32),
        ],
        scratch_shapes=[
            pltpu.VMEM((n_streams, 1, tq), F32),
            pltpu.VMEM((n_streams, DIFF_VDIM + BF16_SUBLANES, tq), F32),
            pltpu.VMEM((2, n_streams, tq, tq), F32),
            pltpu.VMEM((2, n_streams, 1, tq), F32),
            pltpu.VMEM((heads * batch, n, n), F32),
            pltpu.VMEM((nc * heads * batch, WKV_CHUNK, 2 * WKV_CHUNK), BF16),
            pltpu.VMEM((nc * heads * batch, WKV_CHUNK, 2 * WKV_CHUNK), BF16),
        ],
        compiler_params=pltpu.CompilerParams(
            dimension_semantics=("arbitrary",), vmem_limit_bytes=VMEM_LIMIT_BYTES),
        name="seq_mix",
    )(qk, qk, vt, row(lq1), row(lk1), row(lq2), row(lk2), row(subln_w),
      *[x.reshape(shape4) for x in (rt, kt, bt, at, v, gam)])
    return yb, yw.reshape(m, heads * n)


def _post_kernel(x_ref, yw_ref, bonus_ref, g_ref, yb_ref, lnw_ref, lnb_ref, wo_ref,
                 nffn_ref, wg_ref, wu_ref, wd_ref, nfin_ref, o_ref, *, tff, final_norm):
    ya = (yw_ref[...] * lnw_ref[...] + lnb_ref[...] + bonus_ref[...]) * g_ref[...]
    mixed = jnp.concatenate([ya.astype(BF16), yb_ref[...]], axis=-1)
    h1 = x_ref[...] + _mm(mixed, wo_ref[...])
    u = _rms(h1, nffn_ref[...], NORM_EPS).astype(BF16)
    acc = jnp.zeros_like(h1)
    for c0 in range(0, wg_ref.shape[1], tff):
        gt = _mm(u, wg_ref[:, c0:c0 + tff])
        up = _mm(u, wu_ref[:, c0:c0 + tff])
        act = (gt * _sigmoid(gt) * up).astype(BF16)
        acc = acc + _mm(act, wd_ref[c0:c0 + tff, :])
    h2 = h1 + acc
    if final_norm:
        h2 = _rms(h2, nfin_ref[...], NORM_EPS)
    o_ref[...] = h2


def _post(x2, yw, bonus, gate, yb, ln_w, ln_b, w_out_bf16, norm_ffn_w, wg, wu, wd, norm_final_w,
          *, final_norm, tm=512, tff=256):
    m, d = x2.shape
    const = dict(pipeline_mode=pl.Buffered(1))
    whole = lambda a: pl.BlockSpec(a.shape, lambda i: (0,) * a.ndim, **const)
    rows = lambda n: pl.BlockSpec((tm, n), lambda i: (i, 0))
    row = lambda a: a.reshape(1, -1)
    consts = (row(ln_w), row(ln_b), w_out_bf16, row(norm_ffn_w), wg, wu, wd, row(norm_final_w))
    return pl.pallas_call(
        functools.partial(_post_kernel, tff=tff, final_norm=final_norm),
        grid=(m // tm,),
        in_specs=[rows(d), rows(RWKV_WIDTH), rows(RWKV_WIDTH), rows(RWKV_WIDTH), rows(DIFF_WIDTH)]
        + [whole(a) for a in consts],
        out_specs=rows(d),
        out_shape=jax.ShapeDtypeStruct((m, d), F32),
        compiler_params=pltpu.CompilerParams(
            dimension_semantics=("arbitrary",), vmem_limit_bytes=VMEM_LIMIT_BYTES),
        name="post",
    )(x2, yw, bonus, gate, yb, *consts)


def kernel(x, norm_mix_w, w_in, mu_shift, w0, w_lora_up, a0, a_lora_up, g_lora_up, k_k, k_a, r_k,
           ln_x_w, ln_x_b, lambda_q1, lambda_k1, lambda_q2, lambda_k2, subln_w, w_out,
           norm_ffn_w, w_gate, w_up, w_down, norm_final_w):
    batch, seq, d = x.shape
    depth = w_in.shape[0]
    h = x.reshape(batch * seq, d)
    for l in range(depth):
        lambda_init = 0.8 - 0.6 * math.exp(-0.3 * l)
        w2pad = jnp.concatenate(
            [w_lora_up[l], jnp.zeros((AAA_LORA, RWKV_WIDTH), F32)], axis=0).astype(BF16)
        a2pad = jnp.concatenate(
            [jnp.zeros((DECAY_LORA, RWKV_WIDTH), F32), a_lora_up[l]], axis=0).astype(BF16)

        qk, vt, rt, kt, bt, at, v, gam, bonus, gate = _mix_in(
            h, norm_mix_w[l], w_in[l].astype(BF16), mu_shift[l], w0[l], w2pad, a0[l], a2pad,
            g_lora_up[l].astype(BF16), k_k[l], k_a[l], r_k[l].reshape(-1), seq=seq, tm=ATTN_TILE)
        yb, yw = _seq_mix(qk, vt, lambda_q1[l], lambda_k1[l], lambda_q2[l], lambda_k2[l], subln_w[l],
                          rt, kt, bt, at, v, gam, batch=batch, seq=seq, lambda_init=lambda_init)
        h = _post(h, yw, bonus, gate, yb, ln_x_w[l], ln_x_b[l], w_out[l].astype(BF16),
                  norm_ffn_w[l], w_gate[l].astype(BF16), w_up[l].astype(BF16),
                  w_down[l].astype(BF16), norm_final_w, final_norm=(l == depth - 1))
    return h.reshape(batch, seq, d)
```

```python
import functools
import math

import jax
import jax.numpy as jnp
from jax import lax
from jax.experimental import pallas as pl
from jax.experimental.pallas import tpu as pltpu

F32 = jnp.float32
BF16 = jnp.bfloat16

ATTN_CHUNK = 64
RWKV_WIDTH = 512
RWKV_HEAD = 64
RWKV_HEADS = RWKV_WIDTH // RWKV_HEAD
DECAY_LORA = 64
AAA_LORA = 64
GATE_LORA = 128
DIFF_WIDTH = 512
DIFF_HALF = 64
DIFF_VDIM = 2 * DIFF_HALF
DIFF_HEADS = DIFF_WIDTH // DIFF_VDIM
NORM_EPS = 1e-6
GN_EPS = 1e-5 * RWKV_HEAD
SUBLN_EPS = 1e-5
RWKV_COLS = 3 * RWKV_WIDTH + DECAY_LORA + AAA_LORA + GATE_LORA

WKV_CHUNK = 64
ATTN_TILE = 512
BF16_SUBLANES = 16
ATTN_HEADS_PER_STEP = 2

VMEM_LIMIT_BYTES = 56 * 1024 * 1024


def _nt(a, b):
    return lax.dot_general(a, b, (((1,), (1,)), ((), ())), preferred_element_type=F32)


def _tn(a, b):
    return lax.dot_general(a, b, (((0,), (0,)), ((), ())), preferred_element_type=F32)


def _mm(a, b):
    return jnp.dot(a, b, preferred_element_type=F32)


def _rms(x, w, eps):
    return x * lax.rsqrt(jnp.mean(x * x, axis=-1, keepdims=True) + eps) * w


def _sigmoid(x):
    return 1.0 / (1.0 + jnp.exp(-x))


def _split_dot_right(x, mat_bf16, terms):
    acc = None
    rem = x
    for _ in range(terms):
        part = rem.astype(BF16)
        d = _mm(part, mat_bf16)
        acc = d if acc is None else acc + d
        rem = rem - part.astype(F32)
    return acc


def _mix_in_kernel(x_ref, nw_ref, w_ref, mu_ref, w0_ref, w2_ref, a0_ref, a2_ref, g2_ref,
                   kk_ref, ka_ref, rk_ref,
                   qk_ref, vt_ref, rt_ref, kt_ref, bt_ref, at_ref, v_ref, gam_ref, bonus_ref, g_ref,
                   carry_ref, *, tm, tiles_per_seq):
    i = pl.program_id(0)
    seq_start = (i % tiles_per_seq) == 0
    ub = _rms(x_ref[...], nw_ref[...], NORM_EPS).astype(BF16)
    w = RWKV_WIDTH

    def shifted(c0, c1):
        p = _mm(ub, w_ref[:, c0:c1])
        carry = jnp.where(seq_start, 0.0, carry_ref[:, c0:c1])
        carry_ref[:, c0:c1] = p[tm - 1:tm, :]
        row = lax.broadcasted_iota(jnp.int32, p.shape, 0)
        p_prev = jnp.where(row == 0, carry, pltpu.roll(p, 1, axis=0))
        return p + (p_prev - p) * mu_ref[:, c0:c1]

    p_lora = shifted(3 * w, RWKV_COLS)
    p_wa = p_lora[:, :DECAY_LORA + AAA_LORA]
    p_g = p_lora[:, DECAY_LORA + AAA_LORA:]
    tanh_wa = jnp.tanh(p_wa).astype(BF16)
    wa_b = p_wa.astype(BF16)
    sig_g = _sigmoid(p_g).astype(BF16)

    hw = w // 2
    r_i = lax.broadcasted_iota(jnp.int32, (hw, hw), 0) // RWKV_HEAD
    c_i = lax.broadcasted_iota(jnp.int32, (hw, hw), 1) // RWKV_HEAD
    head_ones = (r_i == c_i).astype(BF16)

    for half in range(2):
        c0 = half * hw
        cs = slice(c0, c0 + hw)
        p_r = shifted(c0, c0 + hw)
        p_k = shifted(w + c0, w + c0 + hw)
        p_v = shifted(2 * w + c0, 2 * w + c0 + hw)

        z = w0_ref[:, cs] + _mm(tanh_wa, w2_ref[:, cs])
        softplus_neg = jnp.maximum(-z, 0.0) + jnp.log(1.0 + jnp.exp(-jnp.abs(z)))
        log_decay = -jnp.exp(-softplus_neg - 0.5)
        a_sig = _sigmoid(a0_ref[:, cs] + _mm(wa_b, a2_ref[:, cs]))
        g_ref[:, cs] = _mm(sig_g, g2_ref[:, cs])

        kk = p_k * kk_ref[:, cs]
        kk = kk * lax.rsqrt(jnp.maximum(_split_dot_right(kk * kk, head_ones, 1), 1e-24))
        k_fin = p_k * (1.0 + (a_sig - 1.0) * ka_ref[:, cs])
        bonus_ref[:, cs] = _split_dot_right(p_r * k_fin * rk_ref[:, cs], head_ones, 2) * p_v

        pos = lax.broadcasted_iota(jnp.int32, log_decay.shape, 0) % WKV_CHUNK
        lg = log_decay
        step = 1
        while step < WKV_CHUNK:
            lg = lg + jnp.where(pos >= step, pltpu.roll(lg, step, axis=0), 0.0)
            step *= 2

        gam = jnp.exp(lg)
        gam_inv = jnp.exp(-lg)
        gam_prev = jnp.exp(lg - log_decay)
        outs = ((rt_ref, p_r * gam), (kt_ref, k_fin * gam_inv), (bt_ref, kk * a_sig * gam_inv),
                (at_ref, -kk * gam_prev), (v_ref, p_v), (gam_ref, gam))
        for ref, val in outs:
            for hh in range(hw // RWKV_HEAD):
                h = half * (hw // RWKV_HEAD) + hh
                ref[h] = val[:, hh * RWKV_HEAD:(hh + 1) * RWKV_HEAD].astype(ref.dtype)

    q = _mm(ub, w_ref[:, RWKV_COLS:RWKV_COLS + DIFF_WIDTH])
    qk_ref[:, :DIFF_WIDTH] = (q * (DIFF_HALF ** -0.5 * math.log2(math.e))).astype(BF16)
    k0 = RWKV_COLS + DIFF_WIDTH
    qk_ref[:, DIFF_WIDTH:] = _mm(ub, w_ref[:, k0:k0 + DIFF_WIDTH]).astype(BF16)
    vt_ref[0] = _mm(ub, w_ref[:, k0 + DIFF_WIDTH:]).T.astype(BF16)


def _mix_in(x2, norm_w, w_in_bf16, mu, w0, w2pad, a0, a2pad, g2, k_k, k_a, r_k, *, seq, tm):
    m, d = x2.shape
    w = RWKV_WIDTH
    const = dict(pipeline_mode=pl.Buffered(1))
    row = lambda a: a.reshape(1, -1)
    whole = lambda a: pl.BlockSpec(a.shape, lambda i: (0,) * a.ndim, **const)
    consts = (row(norm_w), w_in_bf16, row(mu), row(w0), w2pad, row(a0), a2pad, g2,
              row(k_k), row(k_a), row(r_k))
    head_major = lambda dt: jax.ShapeDtypeStruct((RWKV_HEADS, m, RWKV_HEAD), dt)
    hm_spec = pl.BlockSpec((RWKV_HEADS, tm, RWKV_HEAD), lambda i: (0, i, 0))
    rows = lambda n: pl.BlockSpec((tm, n), lambda i: (i, 0))
    return pl.pallas_call(
        functools.partial(_mix_in_kernel, tm=tm, tiles_per_seq=seq // tm),
        grid=(m // tm,),
        in_specs=[rows(d)] + [whole(a) for a in consts],
        out_specs=[rows(2 * DIFF_WIDTH), pl.BlockSpec((1, DIFF_WIDTH, tm), lambda i: (i, 0, 0))]
        + [hm_spec] * 6 + [rows(w)] * 2,
        out_shape=[
            jax.ShapeDtypeStruct((m, 2 * DIFF_WIDTH), BF16),
            jax.ShapeDtypeStruct((m // tm, DIFF_WIDTH, tm), BF16),
        ] + [head_major(BF16)] * 5 + [head_major(F32)] + [jax.ShapeDtypeStruct((m, w), F32)] * 2,
        scratch_shapes=[pltpu.VMEM((1, RWKV_COLS), F32)],
        compiler_params=pltpu.CompilerParams(
            dimension_semantics=("arbitrary",), vmem_limit_bytes=VMEM_LIMIT_BYTES),
        name="mix_in",
    )(x2, *consts)


def _wkv_lanes(ref):
    heads, batch = ref.shape[0], ref.shape[1]
    return [(h, b) for b in range(batch) for h in range(heads)]


def _widen(x):
    return jnp.concatenate([x, jnp.zeros_like(x)], axis=1)


def _wkv_chunk_setup(rt_ref, kt_ref, bt_ref, at_ref, v_ref, xs_ref, ar_ref, *, nc):
    lanes = _wkv_lanes(rt_ref)
    c = WKV_CHUNK
    row = lax.broadcasted_iota(jnp.int32, (c, 2 * c), 0)
    col = lax.broadcasted_iota(jnp.int32, (c, 2 * c), 1)
    strict2 = row > col % c
    incl2 = row >= col % c
    zeros_b = jnp.zeros((c, c), BF16)
    bf = lambda xs: [x.astype(BF16) for x in xs]
    items = [(slice(ci * c, (ci + 1) * c), h, b) for ci in range(nc) for h, b in lanes]
    m = len(items)
    a = [at_ref[h, b, sl, :] for sl, h, b in items]
    r = [rt_ref[h, b, sl, :] for sl, h, b in items]
    v = [v_ref[h, b, sl, :] for sl, h, b in items]
    bk = [jnp.concatenate([bt_ref[h, b, sl, :], kt_ref[h, b, sl, :]], axis=0) for sl, h, b in items]
    prod = [_nt(jnp.concatenate([a[j], r[j]], axis=0), bk[j]) for j in range(m)]
    a_a = [jnp.where(strict2, prod[j][:c], 0.0) for j in range(m)]
    for j in range(m):
        ar_ref[j] = jnp.where(incl2, prod[j][c:], 0.0).astype(BF16)
    zv = [jnp.concatenate([zeros_b, v[j]], axis=0) for j in range(m)]
    av = [_mm(a_a[j].astype(BF16), _widen(zv[j])) for j in range(m)]
    x = [_widen(a[j]).astype(F32) + pltpu.roll(av[j], c, axis=1) for j in range(m)]
    power = bf([mat[:, :c] for mat in a_a])
    levels = int(math.log2(c))
    for lv in range(levels):
        if lv + 1 < levels:
            rhs = [jnp.concatenate([x[j].astype(BF16), _widen(power[j])], axis=1)
                   for j in range(m)]
            res = [_mm(power[j], rhs[j]) for j in range(m)]
            x = [x[j] + res[j][:, :2 * c] for j in range(m)]
            power = bf([res[j][:, 2 * c:3 * c] for j in range(m)])
        else:
            x = [x[j] + _mm(power[j], x[j].astype(BF16)) for j in range(m)]
    for j in range(m):
        xs_ref[j] = x[j].astype(BF16)


def _wkv_chunk_chain(rt_ref, kt_ref, bt_ref, v_ref, gam_ref, xs_ref, ar_ref, s_ref, y_ref, *, nc):
    lanes = _wkv_lanes(rt_ref)
    n = len(lanes)
    c = WKV_CHUNK
    row = lax.broadcasted_iota(jnp.int32, (c, 2 * c), 0)
    col = lax.broadcasted_iota(jnp.int32, (c, 2 * c), 1)
    eye_hi = (col == row + c).astype(BF16)
    bf = lambda xs: [x.astype(BF16) for x in xs]
    s = [s_ref[i] for i in range(n)]
    for ci in range(nc):
        sl = slice(ci * c, (ci + 1) * c)
        r = [rt_ref[h, b, sl, :] for h, b in lanes]
        v = [v_ref[h, b, sl, :] for h, b in lanes]
        bk = [jnp.concatenate([bt_ref[h, b, sl, :], kt_ref[h, b, sl, :]], axis=0) for h, b in lanes]
        s_b = bf(s)
        s_i = [_widen(s_b[i]) + eye_hi for i in range(n)]
        u = bf([_nt(xs_ref[ci * n + i], s_i[i]) for i in range(n)])
        uv = [jnp.concatenate([u[i], v[i]], axis=0) for i in range(n)]
        s_new = [s[i] + _tn(uv[i], bk[i]) for i in range(n)]
        y = [_nt(r[i], s_b[i]) + _mm(ar_ref[ci * n + i], uv[i]) for i in range(n)]
        for i, (h, b) in enumerate(lanes):
            g_last = gam_ref[h, b, ci * c + c - 1:(ci + 1) * c, :]
            s[i] = s_new[i] * g_last
            mean = jnp.mean(y[i], axis=-1, keepdims=True)
            yc = y[i] - mean
            var = jnp.mean(yc * yc, axis=-1, keepdims=True)
            y_ref[b, sl, h * RWKV_HEAD:(h + 1) * RWKV_HEAD] = yc * lax.rsqrt(var + GN_EPS)
    for i in range(n):
        s_ref[i] = s[i]


def _seq_mix_kernel(q_ref, k_ref, vt_ref, lq1_ref, lk1_ref, lq2_ref, lk2_ref, sw_ref,
                    rt_ref, kt_ref, bt_ref, at_ref, v_ref, gam_ref,
                    o_ref, y_ref,
                    m_ref, acc_ref, st_ref, bmax_ref, s_ref, xs_ref, ar_ref,
                    *, tq, nq, nc, lambda_init):
    step = pl.program_id(0)
    qi = step % nq

    @pl.when(step == 0)
    def _():
        s_ref[...] = jnp.zeros_like(s_ref)

    hw = DIFF_VDIM
    streams = [(hd, s) for hd in range(ATTN_HEADS_PER_STEP) for s in range(2)]
    q = q_ref[...]
    lane = lax.broadcasted_iota(jnp.int32, (tq, hw), 1)
    q_half = []
    for hd, s in streams:
        qh = q[:, hd * hw:(hd + 1) * hw]
        keep = (lane < DIFF_HALF) if s == 0 else (lane >= DIFF_HALF)
        q_half.append(jnp.where(keep, qh, jnp.zeros_like(qh)))
    ones_rows = jnp.ones((acc_ref.shape[1] - hw, tq), BF16)

    m_ref[...] = jnp.full_like(m_ref, -jnp.inf)
    acc_ref[...] = jnp.zeros_like(acc_ref)

    def scores(j, slot):
        start = pl.multiple_of(j * tq, tq)
        kb = k_ref[pl.ds(start, tq), :]
        for i, (hd, s) in enumerate(streams):
            st = _nt(kb[:, hd * hw:(hd + 1) * hw], q_half[i])
            st_ref[slot, i] = st
            bmax_ref[slot, i] = jnp.max(st, axis=0, keepdims=True)

    def accumulate(j, slot, diagonal):
        vt = vt_ref[j]
        vtb = [jnp.concatenate([vt[hd * hw:(hd + 1) * hw], ones_rows], axis=0)
               for hd in range(ATTN_HEADS_PER_STEP)]
        st = [st_ref[slot, i] for i in range(len(streams))]
        if diagonal:
            ck = lax.broadcasted_iota(jnp.int32, st[0].shape, 0) // ATTN_CHUNK
            cq = lax.broadcasted_iota(jnp.int32, st[0].shape, 1) // ATTN_CHUNK
            st = [jnp.where(ck <= cq, x, -jnp.inf) for x in st]
            bmax = [jnp.max(x, axis=0, keepdims=True) for x in st]
        else:
            bmax = [bmax_ref[slot, i] for i in range(len(streams))]
        m_prev = [m_ref[i] for i in range(len(streams))]
        m_new = [jnp.maximum(a, b) for a, b in zip(m_prev, bmax)]
        p = [jnp.exp2(x - mx).astype(BF16) for x, mx in zip(st, m_new)]
        alpha = [jnp.exp2(a - b) for a, b in zip(m_prev, m_new)]
        for i, (hd, s) in enumerate(streams):
            acc_ref[i] = alpha[i] * acc_ref[i] + _mm(vtb[hd], p[i])
            m_ref[i] = m_new[i]

    _wkv_chunk_setup(rt_ref, kt_ref, bt_ref, at_ref, v_ref, xs_ref, ar_ref, nc=nc)

    scores(0, 0)

    def block_pair(pair, carry):
        t = 2 * pair
        scores(t + 1, 1)
        accumulate(t, 0, False)
        scores(t + 2, 0)
        accumulate(t + 1, 1, False)
        return carry

    lax.fori_loop(0, qi // 2, block_pair, 0)

    @pl.when(qi % 2 == 1)
    def _():
        scores(qi, 1)
        accumulate(qi - 1, 0, False)

    _wkv_chunk_chain(rt_ref, kt_ref, bt_ref, v_ref, gam_ref, xs_ref, ar_ref, s_ref, y_ref, nc=nc)
    accumulate(qi, qi % 2, True)

    lam = (jnp.exp(jnp.sum(lq1_ref[...] * lk1_ref[...], axis=-1, keepdims=True))
           - jnp.exp(jnp.sum(lq2_ref[...] * lk2_ref[...], axis=-1, keepdims=True)) + lambda_init)
    for hd in range(ATTN_HEADS_PER_STEP):
        num = [acc_ref[2 * hd + s, :hw, :] for s in range(2)]
        den = [acc_ref[2 * hd + s, hw:hw + 1, :] for s in range(2)]
        out_t = num[0] / den[0] - lam * (num[1] / den[1])
        out = _rms(out_t.T, sw_ref[...], SUBLN_EPS) * (1.0 - lambda_init)
        o_ref[:, hd * hw:(hd + 1) * hw] = out.astype(o_ref.dtype)


def _seq_mix(qk, vt, lq1, lk1, lq2, lk2, subln_w, rt, kt, bt, at, v, gam,
             *, batch, seq, lambda_init):
    m = qk.shape[0]
    tq = vt.shape[2]
    nq = seq // tq
    hs = ATTN_HEADS_PER_STEP
    groups = DIFF_HEADS // hs
    wide = hs * DIFF_VDIM
    n_streams = 2 * hs
    steps = batch * groups * nq
    heads, _, n = rt.shape
    tc = seq // steps
    assert tc % WKV_CHUNK == 0 and tc * steps == seq
    nc = tc // WKV_CHUNK
    shape4 = (heads, batch, seq, n)

    b_of = lambda s: s // (groups * nq)
    g_of = lambda s: (s // nq) % groups
    i_of = lambda s: s % nq
    const = dict(pipeline_mode=pl.Buffered(1))
    vec = lambda k: pl.BlockSpec((1, k), lambda s: (0, 0), **const)
    row = lambda a: a.reshape(1, -1)
    wkv_spec = pl.BlockSpec((heads, batch, tc, n), lambda s: (0, 0, s, 0))
    yb, yw = pl.pallas_call(
        functools.partial(_seq_mix_kernel, tq=tq, nq=nq, nc=nc, lambda_init=lambda_init),
        grid=(steps,),
        in_specs=[
            pl.BlockSpec((tq, wide), lambda s: (b_of(s) * nq + i_of(s), g_of(s))),
            pl.BlockSpec((seq, wide), lambda s: (b_of(s), groups + g_of(s))),
            pl.BlockSpec((nq, wide, tq), lambda s: (b_of(s), g_of(s), 0)),
            vec(DIFF_HALF), vec(DIFF_HALF), vec(DIFF_HALF), vec(DIFF_HALF), vec(DIFF_VDIM),
        ] + [wkv_spec] * 6,
        out_specs=[
            pl.BlockSpec((tq, wide), lambda s: (b_of(s) * nq + i_of(s), g_of(s))),
            pl.BlockSpec((batch, tc, heads * n), lambda s: (0, s, 0)),
        ],
        out_shape=[
            jax.ShapeDtypeStruct((m, DIFF_WIDTH), BF16),
            jax.ShapeDtypeStruct((batch, seq, heads * n), F32),
        ],
        scratch_shapes=[
            pltpu.VMEM((n_streams, 1, tq), F32),
            pltpu.VMEM((n_streams, DIFF_VDIM + BF16_SUBLANES, tq), F32),
            pltpu.VMEM((2, n_streams, tq, tq), F32),
            pltpu.VMEM((2, n_streams, 1, tq), F32),
            pltpu.VMEM((heads * batch, n, n), F32),
            pltpu.VMEM((nc * heads * batch, WKV_CHUNK, 2 * WKV_CHUNK), BF16),
            pltpu.VMEM((nc * heads * batch, WKV_CHUNK, 2 * WKV_CHUNK), BF16),
        ],
        compiler_params=pltpu.CompilerParams(
            dimension_semantics=("arbitrary",), vmem_limit_bytes=VMEM_LIMIT_BYTES),
        name="seq_mix",
    )(qk, qk, vt, row(lq1), row(lk1), row(lq2), row(lk2), row(subln_w),
      *[x.reshape(shape4) for x in (rt, kt, bt, at, v, gam)])
    return yb, yw.reshape(m, heads * n)


def _post_kernel(x_ref, yw_ref, bonus_ref, g_ref, yb_ref, lnw_ref, lnb_ref, wo_ref,
                 nffn_ref, wg_ref, wu_ref, wd_ref, nfin_ref, o_ref, *, tff, final_norm):
    ya = (yw_ref[...] * lnw_ref[...] + lnb_ref[...] + bonus_ref[...]) * g_ref[...]
    mixed = jnp.concatenate([ya.astype(BF16), yb_ref[...]], axis=-1)
    h1 = x_ref[...] + _mm(mixed, wo_ref[...])
    u = _rms(h1, nffn_ref[...], NORM_EPS).astype(BF16)
    acc = jnp.zeros_like(h1)
    for c0 in range(0, wg_ref.shape[1], tff):
        gt = _mm(u, wg_ref[:, c0:c0 + tff])
        up = _mm(u, wu_ref[:, c0:c0 + tff])
        act = (gt * _sigmoid(gt) * up).astype(BF16)
        acc = acc + _mm(act, wd_ref[c0:c0 + tff, :])
    h2 = h1 + acc
    if final_norm:
        h2 = _rms(h2, nfin_ref[...], NORM_EPS)
    o_ref[...] = h2


def _post(x2, yw, bonus, gate, yb, ln_w, ln_b, w_out_bf16, norm_ffn_w, wg, wu, wd, norm_final_w,
          *, final_norm, tm=512, tff=256):
    m, d = x2.shape
    const = dict(pipeline_mode=pl.Buffered(1))
    whole = lambda a: pl.BlockSpec(a.shape, lambda i: (0,) * a.ndim, **const)
    rows = lambda n: pl.BlockSpec((tm, n), lambda i: (i, 0))
    row = lambda a: a.reshape(1, -1)
    consts = (row(ln_w), row(ln_b), w_out_bf16, row(norm_ffn_w), wg, wu, wd, row(norm_final_w))
    return pl.pallas_call(
        functools.partial(_post_kernel, tff=tff, final_norm=final_norm),
        grid=(m // tm,),
        in_specs=[rows(d), rows(RWKV_WIDTH), rows(RWKV_WIDTH), rows(RWKV_WIDTH), rows(DIFF_WIDTH)]
        + [whole(a) for a in consts],
        out_specs=rows(d),
        out_shape=jax.ShapeDtypeStruct((m, d), F32),
        compiler_params=pltpu.CompilerParams(
            dimension_semantics=("arbitrary",), vmem_limit_bytes=VMEM_LIMIT_BYTES),
        name="post",
    )(x2, yw, bonus, gate, yb, *consts)


def kernel(x, norm_mix_w, w_in, mu_shift, w0, w_lora_up, a0, a_lora_up, g_lora_up, k_k, k_a, r_k,
           ln_x_w, ln_x_b, lambda_q1, lambda_k1, lambda_q2, lambda_k2, subln_w, w_out,
           norm_ffn_w, w_gate, w_up, w_down, norm_final_w):
    batch, seq, d = x.shape
    depth = w_in.shape[0]
    h = x.reshape(batch * seq, d)
    for l in range(depth):
        lambda_init = 0.8 - 0.6 * math.exp(-0.3 * l)
        w2pad = jnp.concatenate(
            [w_lora_up[l], jnp.zeros((AAA_LORA, RWKV_WIDTH), F32)], axis=0).astype(BF16)
        a2pad = jnp.concatenate(
            [jnp.zeros((DECAY_LORA, RWKV_WIDTH), F32), a_lora_up[l]], axis=0).astype(BF16)

        qk, vt, rt, kt, bt, at, v, gam, bonus, gate = _mix_in(
            h, norm_mix_w[l], w_in[l].astype(BF16), mu_shift[l], w0[l], w2pad, a0[l], a2pad,
            g_lora_up[l].astype(BF16), k_k[l], k_a[l], r_k[l].reshape(-1), seq=seq, tm=ATTN_TILE)
        yb, yw = _seq_mix(qk, vt, lambda_q1[l], lambda_k1[l], lambda_q2[l], lambda_k2[l], subln_w[l],
                          rt, kt, bt, at, v, gam, batch=batch, seq=seq, lambda_init=lambda_init)
        h = _post(h, yw, bonus, gate, yb, ln_x_w[l], ln_x_b[l], w_out[l].astype(BF16),
                  norm_ffn_w[l], w_gate[l].astype(BF16), w_up[l].astype(BF16),
                  w_down[l].astype(BF16), norm_final_w, final_norm=(l == depth - 1))
    return h.reshape(batch, seq, d)
```

```python
import functools
import math

import jax
import jax.numpy as jnp
from jax import lax
from jax.experimental import pallas as pl
from jax.experimental.pallas import tpu as pltpu

F32 = jnp.float32
BF16 = jnp.bfloat16

ATTN_CHUNK = 64
RWKV_WIDTH = 512
RWKV_HEAD = 64
RWKV_HEADS = RWKV_WIDTH // RWKV_HEAD
DECAY_LORA = 64
AAA_LORA = 64
GATE_LORA = 128
DIFF_WIDTH = 512
DIFF_HALF = 64
DIFF_VDIM = 2 * DIFF_HALF
DIFF_HEADS = DIFF_WIDTH // DIFF_VDIM
NORM_EPS = 1e-6
GN_EPS = 1e-5 * RWKV_HEAD
SUBLN_EPS = 1e-5
RWKV_COLS = 3 * RWKV_WIDTH + DECAY_LORA + AAA_LORA + GATE_LORA

WKV_CHUNK = 64
ATTN_TILE = 512
BF16_SUBLANES = 16
ATTN_HEADS_PER_STEP = 2

VMEM_LIMIT_BYTES = 56 * 1024 * 1024


def _nt(a, b):
    return lax.dot_general(a, b, (((1,), (1,)), ((), ())), preferred_element_type=F32)


def _tn(a, b):
    return lax.dot_general(a, b, (((0,), (0,)), ((), ())), preferred_element_type=F32)


def _mm(a, b):
    return jnp.dot(a, b, preferred_element_type=F32)


def _rms(x, w, eps):
    return x * lax.rsqrt(jnp.mean(x * x, axis=-1, keepdims=True) + eps) * w


def _sigmoid(x):
    return 1.0 / (1.0 + jnp.exp(-x))


def _split_dot_right(x, mat_bf16, terms):
    acc = None
    rem = x
    for _ in range(terms):
        part = rem.astype(BF16)
        d = _mm(part, mat_bf16)
        acc = d if acc is None else acc + d
        rem = rem - part.astype(F32)
    return acc


def _mix_in_kernel(x_ref, nw_ref, w_ref, mu_ref, w0_ref, w2_ref, a0_ref, a2_ref, g2_ref,
                   kk_ref, ka_ref, rk_ref,
                   qk_ref, vt_ref, rt_ref, kt_ref, bt_ref, at_ref, v_ref, gam_ref, bonus_ref, g_ref,
                   carry_ref, *, tm, tiles_per_seq):
    i = pl.program_id(0)
    seq_start = (i % tiles_per_seq) == 0
    ub = _rms(x_ref[...], nw_ref[...], NORM_EPS).astype(BF16)
    w = RWKV_WIDTH

    def shifted(c0, c1):
        p = _mm(ub, w_ref[:, c0:c1])
        carry = jnp.where(seq_start, 0.0, carry_ref[:, c0:c1])
        carry_ref[:, c0:c1] = p[tm - 1:tm, :]
        row = lax.broadcasted_iota(jnp.int32, p.shape, 0)
        p_prev = jnp.where(row == 0, carry, pltpu.roll(p, 1, axis=0))
        return p + (p_prev - p) * mu_ref[:, c0:c1]

    p_lora = shifted(3 * w, RWKV_COLS)
    p_wa = p_lora[:, :DECAY_LORA + AAA_LORA]
    p_g = p_lora[:, DECAY_LORA + AAA_LORA:]
    tanh_wa = jnp.tanh(p_wa).astype(BF16)
    wa_b = p_wa.astype(BF16)
    sig_g = _sigmoid(p_g).astype(BF16)

    hw = w // 2
    r_i = lax.broadcasted_iota(jnp.int32, (hw, hw), 0) // RWKV_HEAD
    c_i = lax.broadcasted_iota(jnp.int32, (hw, hw), 1) // RWKV_HEAD
    head_ones = (r_i == c_i).astype(BF16)

    for half in range(2):
        c0 = half * hw
        cs = slice(c0, c0 + hw)
        p_r = shifted(c0, c0 + hw)
        p_k = shifted(w + c0, w + c0 + hw)
        p_v = shifted(2 * w + c0, 2 * w + c0 + hw)

        z = w0_ref[:, cs] + _mm(tanh_wa, w2_ref[:, cs])
        softplus_neg = jnp.maximum(-z, 0.0) + jnp.log(1.0 + jnp.exp(-jnp.abs(z)))
        log_decay = -jnp.exp(-softplus_neg - 0.5)
        a_sig = _sigmoid(a0_ref[:, cs] + _mm(wa_b, a2_ref[:, cs]))
        g_ref[:, cs] = _mm(sig_g, g2_ref[:, cs])

        kk = p_k * kk_ref[:, cs]
        kk = kk * lax.rsqrt(jnp.maximum(_split_dot_right(kk * kk, head_ones, 1), 1e-24))
        k_fin = p_k * (1.0 + (a_sig - 1.0) * ka_ref[:, cs])
        bonus_ref[:, cs] = _split_dot_right(p_r * k_fin * rk_ref[:, cs], head_ones, 2) * p_v

        pos = lax.broadcasted_iota(jnp.int32, log_decay.shape, 0) % WKV_CHUNK
        lg = log_decay
        step = 1
        while step < WKV_CHUNK:
            lg = lg + jnp.where(pos >= step, pltpu.roll(lg, step, axis=0), 0.0)
            step *= 2

        gam = jnp.exp(lg)
        gam_inv = jnp.exp(-lg)
        gam_prev = jnp.exp(lg - log_decay)
        outs = ((rt_ref, p_r * gam), (kt_ref, k_fin * gam_inv), (bt_ref, kk * a_sig * gam_inv),
                (at_ref, -kk * gam_prev), (v_ref, p_v), (gam_ref, gam))
        for ref, val in outs:
            for hh in range(hw // RWKV_HEAD):
                h = half * (hw // RWKV_HEAD) + hh
                ref[h] = val[:, hh * RWKV_HEAD:(hh + 1) * RWKV_HEAD].astype(ref.dtype)

    q = _mm(ub, w_ref[:, RWKV_COLS:RWKV_COLS + DIFF_WIDTH])
    qk_ref[:, :DIFF_WIDTH] = (q * (DIFF_HALF ** -0.5 * math.log2(math.e))).astype(BF16)
    k0 = RWKV_COLS + DIFF_WIDTH
    qk_ref[:, DIFF_WIDTH:] = _mm(ub, w_ref[:, k0:k0 + DIFF_WIDTH]).astype(BF16)
    vt_ref[0] = _mm(ub, w_ref[:, k0 + DIFF_WIDTH:]).T.astype(BF16)


def _mix_in(x2, norm_w, w_in_bf16, mu, w0, w2pad, a0, a2pad, g2, k_k, k_a, r_k, *, seq, tm):
    m, d = x2.shape
    w = RWKV_WIDTH
    const = dict(pipeline_mode=pl.Buffered(1))
    row = lambda a: a.reshape(1, -1)
    whole = lambda a: pl.BlockSpec(a.shape, lambda i: (0,) * a.ndim, **const)
    consts = (row(norm_w), w_in_bf16, row(mu), row(w0), w2pad, row(a0), a2pad, g2,
              row(k_k), row(k_a), row(r_k))
    head_major = lambda dt: jax.ShapeDtypeStruct((RWKV_HEADS, m, RWKV_HEAD), dt)
    hm_spec = pl.BlockSpec((RWKV_HEADS, tm, RWKV_HEAD), lambda i: (0, i, 0))
    rows = lambda n: pl.BlockSpec((tm, n), lambda i: (i, 0))
    return pl.pallas_call(
        functools.partial(_mix_in_kernel, tm=tm, tiles_per_seq=seq // tm),
        grid=(m // tm,),
        in_specs=[rows(d)] + [whole(a) for a in consts],
        out_specs=[rows(2 * DIFF_WIDTH), pl.BlockSpec((1, DIFF_WIDTH, tm), lambda i: (i, 0, 0))]
        + [hm_spec] * 6 + [rows(w)] * 2,
        out_shape=[
            jax.ShapeDtypeStruct((m, 2 * DIFF_WIDTH), BF16),
            jax.ShapeDtypeStruct((m // tm, DIFF_WIDTH, tm), BF16),
        ] + [head_major(BF16)] * 5 + [head_major(F32)] + [jax.ShapeDtypeStruct((m, w), F32)] * 2,
        scratch_shapes=[pltpu.VMEM((1, RWKV_COLS), F32)],
        compiler_params=pltpu.CompilerParams(
            dimension_semantics=("arbitrary",), vmem_limit_bytes=VMEM_LIMIT_BYTES),
        name="mix_in",
    )(x2, *consts)


def _wkv_lanes(ref):
    heads, batch = ref.shape[0], ref.shape[1]
    return [(h, b) for b in range(batch) for h in range(heads)]


def _widen(x):
    return jnp.concatenate([x, jnp.zeros_like(x)], axis=1)


def _wkv_chunk_setup(rt_ref, kt_ref, bt_ref, at_ref, v_ref, xs_ref, ar_ref, *, nc):
    lanes = _wkv_lanes(rt_ref)
    c = WKV_CHUNK
    row = lax.broadcasted_iota(jnp.int32, (c, 2 * c), 0)
    col = lax.broadcasted_iota(jnp.int32, (c, 2 * c), 1)
    strict2 = row > col % c
    incl2 = row >= col % c
    zeros_b = jnp.zeros((c, c), BF16)
    bf = lambda xs: [x.astype(BF16) for x in xs]
    items = [(slice(ci * c, (ci + 1) * c), h, b) for ci in range(nc) for h, b in lanes]
    m = len(items)
    a = [at_ref[h, b, sl, :] for sl, h, b in items]
    r = [rt_ref[h, b, sl, :] for sl, h, b in items]
    v = [v_ref[h, b, sl, :] for sl, h, b in items]
    bk = [jnp.concatenate([bt_ref[h, b, sl, :], kt_ref[h, b, sl, :]], axis=0) for sl, h, b in items]
    prod = [_nt(jnp.concatenate([a[j], r[j]], axis=0), bk[j]) for j in range(m)]
    a_a = [jnp.where(strict2, prod[j][:c], 0.0) for j in range(m)]
    for j in range(m):
        ar_ref[j] = jnp.where(incl2, prod[j][c:], 0.0).astype(BF16)
    zv = [jnp.concatenate([zeros_b, v[j]], axis=0) for j in range(m)]
    av = [_mm(a_a[j].astype(BF16), _widen(zv[j])) for j in range(m)]
    x = [_widen(a[j]).astype(F32) + pltpu.roll(av[j], c, axis=1) for j in range(m)]
    power = bf([mat[:, :c] for mat in a_a])
    levels = int(math.log2(c))
    for lv in range(levels):
        if lv + 1 < levels:
            rhs = [jnp.concatenate([x[j].astype(BF16), _widen(power[j])], axis=1)
                   for j in range(m)]
            res = [_mm(power[j], rhs[j]) for j in range(m)]
            x = [x[j] + res[j][:, :2 * c] for j in range(m)]
            power = bf([res[j][:, 2 * c:3 * c] for j in range(m)])
        else:
            x = [x[j] + _mm(power[j], x[j].astype(BF16)) for j in range(m)]
    for j in range(m):
        xs_ref[j] = x[j].astype(BF16)


def _wkv_chunk_chain(rt_ref, kt_ref, bt_ref, v_ref, gam_ref, xs_ref, ar_ref, s_ref, y_ref, *, nc):
    lanes = _wkv_lanes(rt_ref)
    n = len(lanes)
    c = WKV_CHUNK
    row = lax.broadcasted_iota(jnp.int32, (c, 2 * c), 0)
    col = lax.broadcasted_iota(jnp.int32, (c, 2 * c), 1)
    eye_hi = (col == row + c).astype(BF16)
    bf = lambda xs: [x.astype(BF16) for x in xs]
    s = [s_ref[i] for i in range(n)]
    for ci in range(nc):
        sl = slice(ci * c, (ci + 1) * c)
        r = [rt_ref[h, b, sl, :] for h, b in lanes]
        v = [v_ref[h, b, sl, :] for h, b in lanes]
        bk = [jnp.concatenate([bt_ref[h, b, sl, :], kt_ref[h, b, sl, :]], axis=0) for h, b in lanes]
        s_b = bf(s)
        s_i = [_widen(s_b[i]) + eye_hi for i in range(n)]
        u = bf([_nt(xs_ref[ci * n + i], s_i[i]) for i in range(n)])
        uv = [jnp.concatenate([u[i], v[i]], axis=0) for i in range(n)]
        s_new = [s[i] + _tn(uv[i], bk[i]) for i in range(n)]
        y = [_nt(r[i], s_b[i]) + _mm(ar_ref[ci * n + i], uv[i]) for i in range(n)]
        for i, (h, b) in enumerate(lanes):
            g_last = gam_ref[h, b, ci * c + c - 1:(ci + 1) * c, :]
            s[i] = s_new[i] * g_last
            mean = jnp.mean(y[i], axis=-1, keepdims=True)
            yc = y[i] - mean
            var = jnp.mean(yc * yc, axis=-1, keepdims=True)
            y_ref[b, sl, h * RWKV_HEAD:(h + 1) * RWKV_HEAD] = yc * lax.rsqrt(var + GN_EPS)
    for i in range(n):
        s_ref[i] = s[i]


def _seq_mix_kernel(q_ref, k_ref, vt_ref, lq1_ref, lk1_ref, lq2_ref, lk2_ref, sw_ref,
                    rt_ref, kt_ref, bt_ref, at_ref, v_ref, gam_ref,
                    o_ref, y_ref,
                    m_ref, acc_ref, st_ref, bmax_ref, s_ref, xs_ref, ar_ref,
                    *, tq, nq, nc, lambda_init):
    step = pl.program_id(0)
    qi = step % nq

    @pl.when(step == 0)
    def _():
        s_ref[...] = jnp.zeros_like(s_ref)

    hw = DIFF_VDIM
    streams = [(hd, s) for hd in range(ATTN_HEADS_PER_STEP) for s in range(2)]
    q = q_ref[...]
    lane = lax.broadcasted_iota(jnp.int32, (tq, hw), 1)
    q_half = []
    for hd, s in streams:
        qh = q[:, hd * hw:(hd + 1) * hw]
        keep = (lane < DIFF_HALF) if s == 0 else (lane >= DIFF_HALF)
        q_half.append(jnp.where(keep, qh, jnp.zeros_like(qh)))
    ones_rows = jnp.ones((acc_ref.shape[1] - hw, tq), BF16)

    m_ref[...] = jnp.full_like(m_ref, -jnp.inf)
    acc_ref[...] = jnp.zeros_like(acc_ref)

    def scores(j, slot):
        start = pl.multiple_of(j * tq, tq)
        kb = k_ref[pl.ds(start, tq), :]
        for i, (hd, s) in enumerate(streams):
            st = _nt(kb[:, hd * hw:(hd + 1) * hw], q_half[i])
            st_ref[slot, i] = st
            bmax_ref[slot, i] = jnp.max(st, axis=0, keepdims=True)

    def accumulate(j, slot, diagonal):
        vt = vt_ref[j]
        vtb = [jnp.concatenate([vt[hd * hw:(hd + 1) * hw], ones_rows], axis=0)
               for hd in range(ATTN_HEADS_PER_STEP)]
        st = [st_ref[slot, i] for i in range(len(streams))]
        if diagonal:
            ck = lax.broadcasted_iota(jnp.int32, st[0].shape, 0) // ATTN_CHUNK
            cq = lax.broadcasted_iota(jnp.int32, st[0].shape, 1) // ATTN_CHUNK
            st = [jnp.where(ck <= cq, x, -jnp.inf) for x in st]
            bmax = [jnp.max(x, axis=0, keepdims=True) for x in st]
        else:
            bmax = [bmax_ref[slot, i] for i in range(len(streams))]
        m_prev = [m_ref[i] for i in range(len(streams))]
        m_new = [jnp.maximum(a, b) for a, b in zip(m_prev, bmax)]
        p = [jnp.exp2(x - mx).astype(BF16) for x, mx in zip(st, m_new)]
        alpha = [jnp.exp2(a - b) for a, b in zip(m_prev, m_new)]
        for i, (hd, s) in enumerate(streams):
            acc_ref[i] = alpha[i] * acc_ref[i] + _mm(vtb[hd], p[i])
            m_ref[i] = m_new[i]

    _wkv_chunk_setup(rt_ref, kt_ref, bt_ref, at_ref, v_ref, xs_ref, ar_ref, nc=nc)

    scores(0, 0)

    def block_pair(t):
        scores(t + 1, 1)
        accumulate(t, 0, False)
        scores(t + 2, 0)
        accumulate(t + 1, 1, False)

    def block_quad(quad, carry):
        block_pair(4 * quad)
        block_pair(4 * quad + 2)
        return carry

    lax.fori_loop(0, qi // 4, block_quad, 0)

    @pl.when((qi // 2) % 2 == 1)
    def _():
        block_pair(4 * (qi // 4))

    @pl.when(qi % 2 == 1)
    def _():
        scores(qi, 1)
        accumulate(qi - 1, 0, False)

    _wkv_chunk_chain(rt_ref, kt_ref, bt_ref, v_ref, gam_ref, xs_ref, ar_ref, s_ref, y_ref, nc=nc)
    accumulate(qi, qi % 2, True)

    lam = (jnp.exp(jnp.sum(lq1_ref[...] * lk1_ref[...], axis=-1, keepdims=True))
           - jnp.exp(jnp.sum(lq2_ref[...] * lk2_ref[...], axis=-1, keepdims=True)) + lambda_init)
    for hd in range(ATTN_HEADS_PER_STEP):
        num = [acc_ref[2 * hd + s, :hw, :] for s in range(2)]
        den = [acc_ref[2 * hd + s, hw:hw + 1, :] for s in range(2)]
        out_t = num[0] / den[0] - lam * (num[1] / den[1])
        out = _rms(out_t.T, sw_ref[...], SUBLN_EPS) * (1.0 - lambda_init)
        o_ref[:, hd * hw:(hd + 1) * hw] = out.astype(o_ref.dtype)


def _seq_mix(qk, vt, lq1, lk1, lq2, lk2, subln_w, rt, kt, bt, at, v, gam,
             *, batch, seq, lambda_init):
    m = qk.shape[0]
    tq = vt.shape[2]
    nq = seq // tq
    hs = ATTN_HEADS_PER_STEP
    groups = DIFF_HEADS // hs
    wide = hs * DIFF_VDIM
    n_streams = 2 * hs
    steps = batch * groups * nq
    heads, _, n = rt.shape
    tc = seq // steps
    assert tc % WKV_CHUNK == 0 and tc * steps == seq
    nc = tc // WKV_CHUNK
    shape4 = (heads, batch, seq, n)

    b_of = lambda s: s // (groups * nq)
    g_of = lambda s: (s // nq) % groups
    i_of = lambda s: s % nq
    const = dict(pipeline_mode=pl.Buffered(1))
    vec = lambda k: pl.BlockSpec((1, k), lambda s: (0, 0), **const)
    row = lambda a: a.reshape(1, -1)
    wkv_spec = pl.BlockSpec((heads, batch, tc, n), lambda s: (0, 0, s, 0))
    yb, yw = pl.pallas_call(
        functools.partial(_seq_mix_kernel, tq=tq, nq=nq, nc=nc, lambda_init=lambda_init),
        grid=(steps,),
        in_specs=[
            pl.BlockSpec((tq, wide), lambda s: (b_of(s) * nq + i_of(s), g_of(s))),
            pl.BlockSpec((seq, wide), lambda s: (b_of(s), groups + g_of(s))),
            pl.BlockSpec((nq, wide, tq), lambda s: (b_of(s), g_of(s), 0)),
            vec(DIFF_HALF), vec(DIFF_HALF), vec(DIFF_HALF), vec(DIFF_HALF), vec(DIFF_VDIM),
        ] + [wkv_spec] * 6,
        out_specs=[
            pl.BlockSpec((tq, wide), lambda s: (b_of(s) * nq + i_of(s), g_of(s))),
            pl.BlockSpec((batch, tc, heads * n), lambda s: (0, s, 0)),
        ],
        out_shape=[
            jax.ShapeDtypeStruct((m, DIFF_WIDTH), BF16),
            jax.ShapeDtypeStruct((batch, seq, heads * n), F32),
        ],
        scratch_shapes=[
            pltpu.VMEM((n_streams, 1, tq), F32),
            pltpu.VMEM((n_streams, DIFF_VDIM + BF16_SUBLANES, tq), F32),
            pltpu.VMEM((2, n_streams, tq, tq), F32),
            pltpu.VMEM((2, n_streams, 1, tq), F32),
            pltpu.VMEM((heads * batch, n, n), F32),
            pltpu.VMEM((nc * heads * batch, WKV_CHUNK, 2 * WKV_CHUNK), BF16),
            pltpu.VMEM((nc * heads * batch, WKV_CHUNK, 2 * WKV_CHUNK), BF16),
        ],
        compiler_params=pltpu.CompilerParams(
            dimension_semantics=("arbitrary",), vmem_limit_bytes=VMEM_LIMIT_BYTES),
        name="seq_mix",
    )(qk, qk, vt, row(lq1), row(lk1), row(lq2), row(lk2), row(subln_w),
      *[x.reshape(shape4) for x in (rt, kt, bt, at, v, gam)])
    return yb, yw.reshape(m, heads * n)


def _post_kernel(x_ref, yw_ref, bonus_ref, g_ref, yb_ref, lnw_ref, lnb_ref, wo_ref,
                 nffn_ref, wg_ref, wu_ref, wd_ref, nfin_ref, o_ref, *, tff, final_norm):
    ya = (yw_ref[...] * lnw_ref[...] + lnb_ref[...] + bonus_ref[...]) * g_ref[...]
    mixed = jnp.concatenate([ya.astype(BF16), yb_ref[...]], axis=-1)
    h1 = x_ref[...] + _mm(mixed, wo_ref[...])
    u = _rms(h1, nffn_ref[...], NORM_EPS).astype(BF16)
    acc = jnp.zeros_like(h1)
    for c0 in range(0, wg_ref.shape[1], tff):
        gt = _mm(u, wg_ref[:, c0:c0 + tff])
        up = _mm(u, wu_ref[:, c0:c0 + tff])
        act = (gt * _sigmoid(gt) * up).astype(BF16)
        acc = acc + _mm(act, wd_ref[c0:c0 + tff, :])
    h2 = h1 + acc
    if final_norm:
        h2 = _rms(h2, nfin_ref[...], NORM_EPS)
    o_ref[...] = h2


def _post(x2, yw, bonus, gate, yb, ln_w, ln_b, w_out_bf16, norm_ffn_w, wg, wu, wd, norm_final_w,
          *, final_norm, tm=512, tff=256):
    m, d = x2.shape
    const = dict(pipeline_mode=pl.Buffered(1))
    whole = lambda a: pl.BlockSpec(a.shape, lambda i: (0,) * a.ndim, **const)
    rows = lambda n: pl.BlockSpec((tm, n), lambda i: (i, 0))
    row = lambda a: a.reshape(1, -1)
    consts = (row(ln_w), row(ln_b), w_out_bf16, row(norm_ffn_w), wg, wu, wd, row(norm_final_w))
    return pl.pallas_call(
        functools.partial(_post_kernel, tff=tff, final_norm=final_norm),
        grid=(m // tm,),
        in_specs=[rows(d), rows(RWKV_WIDTH), rows(RWKV_WIDTH), rows(RWKV_WIDTH), rows(DIFF_WIDTH)]
        + [whole(a) for a in consts],
        out_specs=rows(d),
        out_shape=jax.ShapeDtypeStruct((m, d), F32),
        compiler_params=pltpu.CompilerParams(
            dimension_semantics=("arbitrary",), vmem_limit_bytes=VMEM_LIMIT_BYTES),
        name="post",
    )(x2, yw, bonus, gate, yb, *consts)


def kernel(x, norm_mix_w, w_in, mu_shift, w0, w_lora_up, a0, a_lora_up, g_lora_up, k_k, k_a, r_k,
           ln_x_w, ln_x_b, lambda_q1, lambda_k1, lambda_q2, lambda_k2, subln_w, w_out,
           norm_ffn_w, w_gate, w_up, w_down, norm_final_w):
    batch, seq, d = x.shape
    depth = w_in.shape[0]
    h = x.reshape(batch * seq, d)
    for l in range(depth):
        lambda_init = 0.8 - 0.6 * math.exp(-0.3 * l)
        w2pad = jnp.concatenate(
            [w_lora_up[l], jnp.zeros((AAA_LORA, RWKV_WIDTH), F32)], axis=0).astype(BF16)
        a2pad = jnp.concatenate(
            [jnp.zeros((DECAY_LORA, RWKV_WIDTH), F32), a_lora_up[l]], axis=0).astype(BF16)

        qk, vt, rt, kt, bt, at, v, gam, bonus, gate = _mix_in(
            h, norm_mix_w[l], w_in[l].astype(BF16), mu_shift[l], w0[l], w2pad, a0[l], a2pad,
            g_lora_up[l].astype(BF16), k_k[l], k_a[l], r_k[l].reshape(-1), seq=seq, tm=ATTN_TILE)
        yb, yw = _seq_mix(qk, vt, lambda_q1[l], lambda_k1[l], lambda_q2[l], lambda_k2[l], subln_w[l],
                          rt, kt, bt, at, v, gam, batch=batch, seq=seq, lambda_init=lambda_init)
        h = _post(h, yw, bonus, gate, yb, ln_x_w[l], ln_x_b[l], w_out[l].astype(BF16),
                  norm_ffn_w[l], w_gate[l].astype(BF16), w_up[l].astype(BF16),
                  w_down[l].astype(BF16), norm_final_w, final_norm=(l == depth - 1))
    return h.reshape(batch, seq, d)
```

```python
import functools
import math

import jax
import jax.numpy as jnp
from jax import lax
from jax.experimental import pallas as pl
from jax.experimental.pallas import tpu as pltpu

F32 = jnp.float32
BF16 = jnp.bfloat16

ATTN_CHUNK = 64
RWKV_WIDTH = 512
RWKV_HEAD = 64
RWKV_HEADS = RWKV_WIDTH // RWKV_HEAD
DECAY_LORA = 64
AAA_LORA = 64
GATE_LORA = 128
DIFF_WIDTH = 512
DIFF_HALF = 64
DIFF_VDIM = 2 * DIFF_HALF
DIFF_HEADS = DIFF_WIDTH // DIFF_VDIM
NORM_EPS = 1e-6
GN_EPS = 1e-5 * RWKV_HEAD
SUBLN_EPS = 1e-5
RWKV_COLS = 3 * RWKV_WIDTH + DECAY_LORA + AAA_LORA + GATE_LORA

WKV_CHUNK = 64
ATTN_TILE = 512
BF16_SUBLANES = 16
ATTN_HEADS_PER_STEP = 2

VMEM_LIMIT_BYTES = 56 * 1024 * 1024


def _nt(a, b):
    return lax.dot_general(a, b, (((1,), (1,)), ((), ())), preferred_element_type=F32)


def _tn(a, b):
    return lax.dot_general(a, b, (((0,), (0,)), ((), ())), preferred_element_type=F32)


def _mm(a, b):
    return jnp.dot(a, b, preferred_element_type=F32)


def _rms(x, w, eps):
    return x * lax.rsqrt(jnp.mean(x * x, axis=-1, keepdims=True) + eps) * w


def _sigmoid(x):
    return 0.5 * jnp.tanh(0.5 * x) + 0.5


def _split_dot_right(x, mat_bf16, terms):
    acc = None
    rem = x
    for _ in range(terms):
        part = rem.astype(BF16)
        d = _mm(part, mat_bf16)
        acc = d if acc is None else acc + d
        rem = rem - part.astype(F32)
    return acc


def _mix_in_kernel(x_ref, nw_ref, w_ref, mu_ref, w0_ref, w2_ref, a0_ref, a2_ref, g2_ref,
                   kk_ref, ka_ref, rk_ref,
                   qk_ref, vt_ref, rt_ref, kt_ref, bt_ref, at_ref, v_ref, gam_ref, bonus_ref, g_ref,
                   carry_ref, *, tm, tiles_per_seq):
    i = pl.program_id(0)
    seq_start = (i % tiles_per_seq) == 0
    ub = _rms(x_ref[...], nw_ref[...], NORM_EPS).astype(BF16)
    w = RWKV_WIDTH

    def shifted(c0, c1):
        p = _mm(ub, w_ref[:, c0:c1])
        carry = jnp.where(seq_start, 0.0, carry_ref[:, c0:c1])
        carry_ref[:, c0:c1] = p[tm - 1:tm, :]
        row = lax.broadcasted_iota(jnp.int32, p.shape, 0)
        p_prev = jnp.where(row == 0, carry, pltpu.roll(p, 1, axis=0))
        return p + (p_prev - p) * mu_ref[:, c0:c1]

    p_lora = shifted(3 * w, RWKV_COLS)
    p_wa = p_lora[:, :DECAY_LORA + AAA_LORA]
    p_g = p_lora[:, DECAY_LORA + AAA_LORA:]
    tanh_wa = jnp.tanh(p_wa).astype(BF16)
    wa_b = p_wa.astype(BF16)
    sig_g = _sigmoid(p_g).astype(BF16)

    hw = w // 2
    r_i = lax.broadcasted_iota(jnp.int32, (hw, hw), 0) // RWKV_HEAD
    c_i = lax.broadcasted_iota(jnp.int32, (hw, hw), 1) // RWKV_HEAD
    head_ones = (r_i == c_i).astype(BF16)

    for half in range(2):
        c0 = half * hw
        cs = slice(c0, c0 + hw)
        p_r = shifted(c0, c0 + hw)
        p_k = shifted(w + c0, w + c0 + hw)
        p_v = shifted(2 * w + c0, 2 * w + c0 + hw)

        z = w0_ref[:, cs] + _mm(tanh_wa, w2_ref[:, cs])
        log_decay = -math.exp(-0.5) * _sigmoid(z)
        a_sig = _sigmoid(a0_ref[:, cs] + _mm(wa_b, a2_ref[:, cs]))
        g_ref[:, cs] = _mm(sig_g, g2_ref[:, cs])

        kk = p_k * kk_ref[:, cs]
        kk = kk * lax.rsqrt(jnp.maximum(_split_dot_right(kk * kk, head_ones, 1), 1e-24))
        k_fin = p_k * (1.0 + (a_sig - 1.0) * ka_ref[:, cs])
        bonus_ref[:, cs] = _split_dot_right(p_r * k_fin * rk_ref[:, cs], head_ones, 2) * p_v

        pos = lax.broadcasted_iota(jnp.int32, log_decay.shape, 0) % WKV_CHUNK
        lg = log_decay
        step = 1
        while step < WKV_CHUNK:
            lg = lg + jnp.where(pos >= step, pltpu.roll(lg, step, axis=0), 0.0)
            step *= 2

        gam = jnp.exp(lg)
        gam_inv = jnp.exp(-lg)
        gam_prev = jnp.exp(lg - log_decay)
        outs = ((rt_ref, p_r * gam), (kt_ref, k_fin * gam_inv), (bt_ref, kk * a_sig * gam_inv),
                (at_ref, -kk * gam_prev), (v_ref, p_v), (gam_ref, gam))
        for ref, val in outs:
            for hh in range(hw // RWKV_HEAD):
                h = half * (hw // RWKV_HEAD) + hh
                ref[h] = val[:, hh * RWKV_HEAD:(hh + 1) * RWKV_HEAD].astype(ref.dtype)

    q = _mm(ub, w_ref[:, RWKV_COLS:RWKV_COLS + DIFF_WIDTH])
    qk_ref[:, :DIFF_WIDTH] = (q * (DIFF_HALF ** -0.5 * math.log2(math.e))).astype(BF16)
    k0 = RWKV_COLS + DIFF_WIDTH
    qk_ref[:, DIFF_WIDTH:] = _mm(ub, w_ref[:, k0:k0 + DIFF_WIDTH]).astype(BF16)
    vt_ref[0] = _mm(ub, w_ref[:, k0 + DIFF_WIDTH:]).T.astype(BF16)


def _mix_in(x2, norm_w, w_in_bf16, mu, w0, w2pad, a0, a2pad, g2, k_k, k_a, r_k, *, seq, tm):
    m, d = x2.shape
    w = RWKV_WIDTH
    const = dict(pipeline_mode=pl.Buffered(1))
    row = lambda a: a.reshape(1, -1)
    whole = lambda a: pl.BlockSpec(a.shape, lambda i: (0,) * a.ndim, **const)
    consts = (row(norm_w), w_in_bf16, row(mu), row(w0), w2pad, row(a0), a2pad, g2,
              row(k_k), row(k_a), row(r_k))
    head_major = lambda dt: jax.ShapeDtypeStruct((RWKV_HEADS, m, RWKV_HEAD), dt)
    hm_spec = pl.BlockSpec((RWKV_HEADS, tm, RWKV_HEAD), lambda i: (0, i, 0))
    rows = lambda n: pl.BlockSpec((tm, n), lambda i: (i, 0))
    return pl.pallas_call(
        functools.partial(_mix_in_kernel, tm=tm, tiles_per_seq=seq // tm),
        grid=(m // tm,),
        in_specs=[rows(d)] + [whole(a) for a in consts],
        out_specs=[rows(2 * DIFF_WIDTH), pl.BlockSpec((1, DIFF_WIDTH, tm), lambda i: (i, 0, 0))]
        + [hm_spec] * 6 + [rows(w)] * 2,
        out_shape=[
            jax.ShapeDtypeStruct((m, 2 * DIFF_WIDTH), BF16),
            jax.ShapeDtypeStruct((m // tm, DIFF_WIDTH, tm), BF16),
        ] + [head_major(BF16)] * 5 + [head_major(F32)] + [jax.ShapeDtypeStruct((m, w), F32)] * 2,
        scratch_shapes=[pltpu.VMEM((1, RWKV_COLS), F32)],
        compiler_params=pltpu.CompilerParams(
            dimension_semantics=("arbitrary",), vmem_limit_bytes=VMEM_LIMIT_BYTES),
        name="mix_in",
    )(x2, *consts)


def _wkv_lanes(ref):
    heads, batch = ref.shape[0], ref.shape[1]
    return [(h, b) for b in range(batch) for h in range(heads)]


def _widen(x):
    return jnp.concatenate([x, jnp.zeros_like(x)], axis=1)


def _wkv_chunk_setup(rt_ref, kt_ref, bt_ref, at_ref, v_ref, xs_ref, ar_ref, *, nc):
    lanes = _wkv_lanes(rt_ref)
    c = WKV_CHUNK
    row = lax.broadcasted_iota(jnp.int32, (c, 2 * c), 0)
    col = lax.broadcasted_iota(jnp.int32, (c, 2 * c), 1)
    strict2 = row > col % c
    incl2 = row >= col % c
    zeros_b = jnp.zeros((c, c), BF16)
    bf = lambda xs: [x.astype(BF16) for x in xs]
    items = [(slice(ci * c, (ci + 1) * c), h, b) for ci in range(nc) for h, b in lanes]
    m = len(items)
    a = [at_ref[h, b, sl, :] for sl, h, b in items]
    r = [rt_ref[h, b, sl, :] for sl, h, b in items]
    v = [v_ref[h, b, sl, :] for sl, h, b in items]
    bk = [jnp.concatenate([bt_ref[h, b, sl, :], kt_ref[h, b, sl, :]], axis=0) for sl, h, b in items]
    prod = [_nt(jnp.concatenate([a[j], r[j]], axis=0), bk[j]) for j in range(m)]
    a_a = [jnp.where(strict2, prod[j][:c], 0.0) for j in range(m)]
    for j in range(m):
        ar_ref[j] = jnp.where(incl2, prod[j][c:], 0.0).astype(BF16)
    zv = [jnp.concatenate([zeros_b, v[j]], axis=0) for j in range(m)]
    av = [_mm(a_a[j].astype(BF16), _widen(zv[j])) for j in range(m)]
    x = [_widen(a[j]).astype(F32) + pltpu.roll(av[j], c, axis=1) for j in range(m)]
    power = bf([mat[:, :c] for mat in a_a])
    levels = int(math.log2(c))
    for lv in range(levels):
        if lv + 1 < levels:
            rhs = [jnp.concatenate([x[j].astype(BF16), _widen(power[j])], axis=1)
                   for j in range(m)]
            res = [_mm(power[j], rhs[j]) for j in range(m)]
            x = [x[j] + res[j][:, :2 * c] for j in range(m)]
            power = bf([res[j][:, 2 * c:3 * c] for j in range(m)])
        else:
            x = [x[j] + _mm(power[j], x[j].astype(BF16)) for j in range(m)]
    for j in range(m):
        xs_ref[j] = x[j].astype(BF16)


def _wkv_chunk_chain(rt_ref, kt_ref, bt_ref, v_ref, gam_ref, xs_ref, ar_ref, s_ref, y_ref, *, nc):
    lanes = _wkv_lanes(rt_ref)
    n = len(lanes)
    c = WKV_CHUNK
    row = lax.broadcasted_iota(jnp.int32, (c, 2 * c), 0)
    col = lax.broadcasted_iota(jnp.int32, (c, 2 * c), 1)
    eye_hi = (col == row + c).astype(BF16)
    bf = lambda xs: [x.astype(BF16) for x in xs]
    s = [s_ref[i] for i in range(n)]
    for ci in range(nc):
        sl = slice(ci * c, (ci + 1) * c)
        r = [rt_ref[h, b, sl, :] for h, b in lanes]
        v = [v_ref[h, b, sl, :] for h, b in lanes]
        bk = [jnp.concatenate([bt_ref[h, b, sl, :], kt_ref[h, b, sl, :]], axis=0) for h, b in lanes]
        s_b = bf(s)
        s_i = [_widen(s_b[i]) + eye_hi for i in range(n)]
        u = bf([_nt(xs_ref[ci * n + i], s_i[i]) for i in range(n)])
        uv = [jnp.concatenate([u[i], v[i]], axis=0) for i in range(n)]
        s_new = [s[i] + _tn(uv[i], bk[i]) for i in range(n)]
        y = [_nt(r[i], s_b[i]) + _mm(ar_ref[ci * n + i], uv[i]) for i in range(n)]
        for i, (h, b) in enumerate(lanes):
            g_last = gam_ref[h, b, ci * c + c - 1:(ci + 1) * c, :]
            s[i] = s_new[i] * g_last
            mean = jnp.mean(y[i], axis=-1, keepdims=True)
            yc = y[i] - mean
            var = jnp.mean(yc * yc, axis=-1, keepdims=True)
            y_ref[b, sl, h * RWKV_HEAD:(h + 1) * RWKV_HEAD] = yc * lax.rsqrt(var + GN_EPS)
    for i in range(n):
        s_ref[i] = s[i]


def _seq_mix_kernel(q_ref, k_ref, vt_ref, lq1_ref, lk1_ref, lq2_ref, lk2_ref, sw_ref,
                    rt_ref, kt_ref, bt_ref, at_ref, v_ref, gam_ref,
                    o_ref, y_ref,
                    m_ref, acc_ref, st_ref, bmax_ref, s_ref, xs_ref, ar_ref,
                    *, tq, nq, nc, lambda_init):
    step = pl.program_id(0)
    qi = step % nq

    @pl.when(step == 0)
    def _():
        s_ref[...] = jnp.zeros_like(s_ref)

    hw = DIFF_VDIM
    streams = [(hd, s) for hd in range(ATTN_HEADS_PER_STEP) for s in range(2)]
    q = q_ref[...]
    lane = lax.broadcasted_iota(jnp.int32, (tq, hw), 1)
    q_half = []
    for hd, s in streams:
        qh = q[:, hd * hw:(hd + 1) * hw]
        keep = (lane < DIFF_HALF) if s == 0 else (lane >= DIFF_HALF)
        q_half.append(jnp.where(keep, qh, jnp.zeros_like(qh)))
    ones_rows = jnp.ones((acc_ref.shape[1] - hw, tq), BF16)

    m_ref[...] = jnp.full_like(m_ref, -jnp.inf)
    acc_ref[...] = jnp.zeros_like(acc_ref)

    def scores(j, slot):
        start = pl.multiple_of(j * tq, tq)
        kb = k_ref[pl.ds(start, tq), :]
        for i, (hd, s) in enumerate(streams):
            st = _nt(kb[:, hd * hw:(hd + 1) * hw], q_half[i])
            st_ref[slot, i] = st
            bmax_ref[slot, i] = jnp.max(st, axis=0, keepdims=True)

    def accumulate(j, slot, diagonal):
        vt = vt_ref[j]
        vtb = [jnp.concatenate([vt[hd * hw:(hd + 1) * hw], ones_rows], axis=0)
               for hd in range(ATTN_HEADS_PER_STEP)]
        st = [st_ref[slot, i] for i in range(len(streams))]
        if diagonal:
            ck = lax.broadcasted_iota(jnp.int32, st[0].shape, 0) // ATTN_CHUNK
            cq = lax.broadcasted_iota(jnp.int32, st[0].shape, 1) // ATTN_CHUNK
            st = [jnp.where(ck <= cq, x, -jnp.inf) for x in st]
            bmax = [jnp.max(x, axis=0, keepdims=True) for x in st]
        else:
            bmax = [bmax_ref[slot, i] for i in range(len(streams))]
        m_prev = [m_ref[i] for i in range(len(streams))]
        m_new = [jnp.maximum(a, b) for a, b in zip(m_prev, bmax)]
        p = [jnp.exp2(x - mx).astype(BF16) for x, mx in zip(st, m_new)]
        alpha = [jnp.exp2(a - b) for a, b in zip(m_prev, m_new)]
        for i, (hd, s) in enumerate(streams):
            acc_ref[i] = alpha[i] * acc_ref[i] + _mm(vtb[hd], p[i])
            m_ref[i] = m_new[i]

    _wkv_chunk_setup(rt_ref, kt_ref, bt_ref, at_ref, v_ref, xs_ref, ar_ref, nc=nc)

    scores(0, 0)

    def block_pair(t):
        scores(t + 1, 1)
        accumulate(t, 0, False)
        scores(t + 2, 0)
        accumulate(t + 1, 1, False)

    def block_quad(quad, carry):
        block_pair(4 * quad)
        block_pair(4 * quad + 2)
        return carry

    lax.fori_loop(0, qi // 4, block_quad, 0)

    @pl.when((qi // 2) % 2 == 1)
    def _():
        block_pair(4 * (qi // 4))

    @pl.when(qi % 2 == 1)
    def _():
        scores(qi, 1)
        accumulate(qi - 1, 0, False)

    _wkv_chunk_chain(rt_ref, kt_ref, bt_ref, v_ref, gam_ref, xs_ref, ar_ref, s_ref, y_ref, nc=nc)
    accumulate(qi, qi % 2, True)

    lam = (jnp.exp(jnp.sum(lq1_ref[...] * lk1_ref[...], axis=-1, keepdims=True))
           - jnp.exp(jnp.sum(lq2_ref[...] * lk2_ref[...], axis=-1, keepdims=True)) + lambda_init)
    for hd in range(ATTN_HEADS_PER_STEP):
        num = [acc_ref[2 * hd + s, :hw, :] for s in range(2)]
        den = [acc_ref[2 * hd + s, hw:hw + 1, :] for s in range(2)]
        out_t = num[0] / den[0] - lam * (num[1] / den[1])
        out = _rms(out_t.T, sw_ref[...], SUBLN_EPS) * (1.0 - lambda_init)
        o_ref[:, hd * hw:(hd + 1) * hw] = out.astype(o_ref.dtype)


def _seq_mix(qk, vt, lq1, lk1, lq2, lk2, subln_w, rt, kt, bt, at, v, gam,
             *, batch, seq, lambda_init):
    m = qk.shape[0]
    tq = vt.shape[2]
    nq = seq // tq
    hs = ATTN_HEADS_PER_STEP
    groups = DIFF_HEADS // hs
    wide = hs * DIFF_VDIM
    n_streams = 2 * hs
    steps = batch * groups * nq
    heads, _, n = rt.shape
    tc = seq // steps
    assert tc % WKV_CHUNK == 0 and tc * steps == seq
    nc = tc // WKV_CHUNK
    shape4 = (heads, batch, seq, n)

    b_of = lambda s: s // (groups * nq)
    g_of = lambda s: (s // nq) % groups
    i_of = lambda s: s % nq
    const = dict(pipeline_mode=pl.Buffered(1))
    vec = lambda k: pl.BlockSpec((1, k), lambda s: (0, 0), **const)
    row = lambda a: a.reshape(1, -1)
    wkv_spec = pl.BlockSpec((heads, batch, tc, n), lambda s: (0, 0, s, 0))
    yb, yw = pl.pallas_call(
        functools.partial(_seq_mix_kernel, tq=tq, nq=nq, nc=nc, lambda_init=lambda_init),
        grid=(steps,),
        in_specs=[
            pl.BlockSpec((tq, wide), lambda s: (b_of(s) * nq + i_of(s), g_of(s))),
            pl.BlockSpec((seq, wide), lambda s: (b_of(s), groups + g_of(s))),
            pl.BlockSpec((nq, wide, tq), lambda s: (b_of(s), g_of(s), 0)),
            vec(DIFF_HALF), vec(DIFF_HALF), vec(DIFF_HALF), vec(DIFF_HALF), vec(DIFF_VDIM),
        ] + [wkv_spec] * 6,
        out_specs=[
            pl.BlockSpec((tq, wide), lambda s: (b_of(s) * nq + i_of(s), g_of(s))),
            pl.BlockSpec((batch, tc, heads * n), lambda s: (0, s, 0)),
        ],
        out_shape=[
            jax.ShapeDtypeStruct((m, DIFF_WIDTH), BF16),
            jax.ShapeDtypeStruct((batch, seq, heads * n), F32),
        ],
        scratch_shapes=[
            pltpu.VMEM((n_streams, 1, tq), F32),
            pltpu.VMEM((n_streams, DIFF_VDIM + BF16_SUBLANES, tq), F32),
            pltpu.VMEM((2, n_streams, tq, tq), F32),
            pltpu.VMEM((2, n_streams, 1, tq), F32),
            pltpu.VMEM((heads * batch, n, n), F32),
            pltpu.VMEM((nc * heads * batch, WKV_CHUNK, 2 * WKV_CHUNK), BF16),
            pltpu.VMEM((nc * heads * batch, WKV_CHUNK, 2 * WKV_CHUNK), BF16),
        ],
        compiler_params=pltpu.CompilerParams(
            dimension_semantics=("arbitrary",), vmem_limit_bytes=VMEM_LIMIT_BYTES),
        name="seq_mix",
    )(qk, qk, vt, row(lq1), row(lk1), row(lq2), row(lk2), row(subln_w),
      *[x.reshape(shape4) for x in (rt, kt, bt, at, v, gam)])
    return yb, yw.reshape(m, heads * n)


def _post_kernel(x_ref, yw_ref, bonus_ref, g_ref, yb_ref, lnw_ref, lnb_ref, wo_ref,
                 nffn_ref, wg_ref, wu_ref, wd_ref, nfin_ref, o_ref, *, tff, final_norm):
    ya = (yw_ref[...] * lnw_ref[...] + lnb_ref[...] + bonus_ref[...]) * g_ref[...]
    mixed = jnp.concatenate([ya.astype(BF16), yb_ref[...]], axis=-1)
    h1 = x_ref[...] + _mm(mixed, wo_ref[...])
    u = _rms(h1, nffn_ref[...], NORM_EPS).astype(BF16)
    acc = jnp.zeros_like(h1)
    for c0 in range(0, wg_ref.shape[1], tff):
        gt = _mm(u, wg_ref[:, c0:c0 + tff])
        up = _mm(u, wu_ref[:, c0:c0 + tff])
        act = (gt * _sigmoid(gt) * up).astype(BF16)
        acc = acc + _mm(act, wd_ref[c0:c0 + tff, :])
    h2 = h1 + acc
    if final_norm:
        h2 = _rms(h2, nfin_ref[...], NORM_EPS)
    o_ref[...] = h2


def _post(x2, yw, bonus, gate, yb, ln_w, ln_b, w_out_bf16, norm_ffn_w, wg, wu, wd, norm_final_w,
          *, final_norm, tm=512, tff=256):
    m, d = x2.shape
    const = dict(pipeline_mode=pl.Buffered(1))
    whole = lambda a: pl.BlockSpec(a.shape, lambda i: (0,) * a.ndim, **const)
    rows = lambda n: pl.BlockSpec((tm, n), lambda i: (i, 0))
    row = lambda a: a.reshape(1, -1)
    consts = (row(ln_w), row(ln_b), w_out_bf16, row(norm_ffn_w), wg, wu, wd, row(norm_final_w))
    return pl.pallas_call(
        functools.partial(_post_kernel, tff=tff, final_norm=final_norm),
        grid=(m // tm,),
        in_specs=[rows(d), rows(RWKV_WIDTH), rows(RWKV_WIDTH), rows(RWKV_WIDTH), rows(DIFF_WIDTH)]
        + [whole(a) for a in consts],
        out_specs=rows(d),
        out_shape=jax.ShapeDtypeStruct((m, d), F32),
        compiler_params=pltpu.CompilerParams(
            dimension_semantics=("arbitrary",), vmem_limit_bytes=VMEM_LIMIT_BYTES),
        name="post",
    )(x2, yw, bonus, gate, yb, *consts)


def kernel(x, norm_mix_w, w_in, mu_shift, w0, w_lora_up, a0, a_lora_up, g_lora_up, k_k, k_a, r_k,
           ln_x_w, ln_x_b, lambda_q1, lambda_k1, lambda_q2, lambda_k2, subln_w, w_out,
           norm_ffn_w, w_gate, w_up, w_down, norm_final_w):
    batch, seq, d = x.shape
    depth = w_in.shape[0]
    h = x.reshape(batch * seq, d)
    for l in range(depth):
        lambda_init = 0.8 - 0.6 * math.exp(-0.3 * l)
        w2pad = jnp.concatenate(
            [w_lora_up[l], jnp.zeros((AAA_LORA, RWKV_WIDTH), F32)], axis=0).astype(BF16)
        a2pad = jnp.concatenate(
            [jnp.zeros((DECAY_LORA, RWKV_WIDTH), F32), a_lora_up[l]], axis=0).astype(BF16)

        qk, vt, rt, kt, bt, at, v, gam, bonus, gate = _mix_in(
            h, norm_mix_w[l], w_in[l].astype(BF16), mu_shift[l], w0[l], w2pad, a0[l], a2pad,
            g_lora_up[l].astype(BF16), k_k[l], k_a[l], r_k[l].reshape(-1), seq=seq, tm=ATTN_TILE)
        yb, yw = _seq_mix(qk, vt, lambda_q1[l], lambda_k1[l], lambda_q2[l], lambda_k2[l], subln_w[l],
                          rt, kt, bt, at, v, gam, batch=batch, seq=seq, lambda_init=lambda_init)
        h = _post(h, yw, bonus, gate, yb, ln_x_w[l], ln_x_b[l], w_out[l].astype(BF16),
                  norm_ffn_w[l], w_gate[l].astype(BF16), w_up[l].astype(BF16),
                  w_down[l].astype(BF16), norm_final_w, final_norm=(l == depth - 1))
    return h.reshape(batch, seq, d)
```

```python
import functools
import math

import jax
import jax.numpy as jnp
from jax import lax
from jax.experimental import pallas as pl
from jax.experimental.pallas import tpu as pltpu

F32 = jnp.float32
BF16 = jnp.bfloat16

ATTN_CHUNK = 64
RWKV_WIDTH = 512
RWKV_HEAD = 64
RWKV_HEADS = RWKV_WIDTH // RWKV_HEAD
DECAY_LORA = 64
AAA_LORA = 64
GATE_LORA = 128
DIFF_WIDTH = 512
DIFF_HALF = 64
DIFF_VDIM = 2 * DIFF_HALF
DIFF_HEADS = DIFF_WIDTH // DIFF_VDIM
NORM_EPS = 1e-6
GN_EPS = 1e-5 * RWKV_HEAD
SUBLN_EPS = 1e-5
RWKV_COLS = 3 * RWKV_WIDTH + DECAY_LORA + AAA_LORA + GATE_LORA

WKV_CHUNK = 64
ATTN_TILE = 512
BF16_SUBLANES = 16
ATTN_HEADS_PER_STEP = 2

VMEM_LIMIT_BYTES = 56 * 1024 * 1024


def _nt(a, b):
    return lax.dot_general(a, b, (((1,), (1,)), ((), ())), preferred_element_type=F32)


def _mm(a, b):
    return jnp.dot(a, b, preferred_element_type=F32)


def _rms(x, w, eps):
    return x * lax.rsqrt(jnp.mean(x * x, axis=-1, keepdims=True) + eps) * w


def _sigmoid(x):
    return 0.5 * jnp.tanh(0.5 * x) + 0.5


def _split_dot_right(x, mat_bf16, terms):
    acc = None
    rem = x
    for _ in range(terms):
        part = rem.astype(BF16)
        d = _mm(part, mat_bf16)
        acc = d if acc is None else acc + d
        rem = rem - part.astype(F32)
    return acc


def _mix_in_kernel(x_ref, nw_ref, w_ref, mu_ref, w0_ref, w2_ref, a0_ref, a2_ref, g2_ref,
                   kk_ref, ka_ref, rk_ref,
                   qk_ref, vt_ref, rt_ref, kt_ref, bt_ref, at_ref, v_ref, gam_ref, bonus_ref, g_ref,
                   carry_ref, *, tm, tiles_per_seq):
    i = pl.program_id(0)
    seq_start = (i % tiles_per_seq) == 0
    ub = _rms(x_ref[...], nw_ref[...], NORM_EPS).astype(BF16)
    w = RWKV_WIDTH

    def shifted(c0, c1):
        p = _mm(ub, w_ref[:, c0:c1])
        carry = jnp.where(seq_start, 0.0, carry_ref[:, c0:c1])
        carry_ref[:, c0:c1] = p[tm - 1:tm, :]
        row = lax.broadcasted_iota(jnp.int32, p.shape, 0)
        p_prev = jnp.where(row == 0, carry, pltpu.roll(p, 1, axis=0))
        return p + (p_prev - p) * mu_ref[:, c0:c1]

    p_lora = shifted(3 * w, RWKV_COLS)
    p_wa = p_lora[:, :DECAY_LORA + AAA_LORA]
    p_g = p_lora[:, DECAY_LORA + AAA_LORA:]
    tanh_wa = jnp.tanh(p_wa).astype(BF16)
    wa_b = p_wa.astype(BF16)
    sig_g = _sigmoid(p_g).astype(BF16)

    hw = w // 2
    r_i = lax.broadcasted_iota(jnp.int32, (hw, hw), 0) // RWKV_HEAD
    c_i = lax.broadcasted_iota(jnp.int32, (hw, hw), 1) // RWKV_HEAD
    head_ones = (r_i == c_i).astype(BF16)

    n_chunks = tm // WKV_CHUNK

    for half in range(2):
        c0 = half * hw
        cs = slice(c0, c0 + hw)
        p_r = shifted(c0, c0 + hw)
        p_k = shifted(w + c0, w + c0 + hw)
        p_v = shifted(2 * w + c0, 2 * w + c0 + hw)

        z = w0_ref[:, cs] + _mm(tanh_wa, w2_ref[:, cs])
        log_decay = -math.exp(-0.5) * _sigmoid(z)
        a_sig = _sigmoid(a0_ref[:, cs] + _mm(wa_b, a2_ref[:, cs]))
        g_ref[:, cs] = _mm(sig_g, g2_ref[:, cs])

        kk = p_k * kk_ref[:, cs]
        kk = kk * lax.rsqrt(jnp.maximum(_split_dot_right(kk * kk, head_ones, 1), 1e-24))
        k_fin = p_k * (1.0 + (a_sig - 1.0) * ka_ref[:, cs])
        bonus_ref[:, cs] = _split_dot_right(p_r * k_fin * rk_ref[:, cs], head_ones, 2) * p_v

        pos = lax.broadcasted_iota(jnp.int32, log_decay.shape, 0) % WKV_CHUNK
        lg = log_decay
        step = 1
        while step < WKV_CHUNK:
            lg = lg + jnp.where(pos >= step, pltpu.roll(lg, step, axis=0), 0.0)
            step *= 2

        gam = jnp.exp(lg)
        gam_inv = jnp.exp(-lg)
        gam_prev = jnp.exp(lg - log_decay)
        outs = ((rt_ref, p_r * gam), (kt_ref, k_fin * gam_inv), (bt_ref, kk * a_sig * gam_inv),
                (at_ref, -kk * gam_prev), (v_ref, p_v),
                (gam_ref, gam.reshape(n_chunks, WKV_CHUNK, hw)[:, WKV_CHUNK - 1, :]))
        for ref, val in outs:
            for hh in range(hw // RWKV_HEAD):
                h = half * (hw // RWKV_HEAD) + hh
                ref[h] = val[:, hh * RWKV_HEAD:(hh + 1) * RWKV_HEAD].astype(ref.dtype)

    q = _mm(ub, w_ref[:, RWKV_COLS:RWKV_COLS + DIFF_WIDTH])
    qk_ref[:, :DIFF_WIDTH] = (q * (DIFF_HALF ** -0.5 * math.log2(math.e))).astype(BF16)
    k0 = RWKV_COLS + DIFF_WIDTH
    qk_ref[:, DIFF_WIDTH:] = _mm(ub, w_ref[:, k0:k0 + DIFF_WIDTH]).astype(BF16)
    vt_ref[0] = _mm(ub, w_ref[:, k0 + DIFF_WIDTH:]).T.astype(BF16)


def _mix_in(x2, norm_w, w_in_bf16, mu, w0, w2pad, a0, a2pad, g2, k_k, k_a, r_k, *, seq, tm):
    m, d = x2.shape
    w = RWKV_WIDTH
    const = dict(pipeline_mode=pl.Buffered(1))
    row = lambda a: a.reshape(1, -1)
    whole = lambda a: pl.BlockSpec(a.shape, lambda i: (0,) * a.ndim, **const)
    consts = (row(norm_w), w_in_bf16, row(mu), row(w0), w2pad, row(a0), a2pad, g2,
              row(k_k), row(k_a), row(r_k))
    head_major = lambda dt: jax.ShapeDtypeStruct((RWKV_HEADS, m, RWKV_HEAD), dt)
    hm_spec = pl.BlockSpec((RWKV_HEADS, tm, RWKV_HEAD), lambda i: (0, i, 0))
    rows = lambda n: pl.BlockSpec((tm, n), lambda i: (i, 0))
    return pl.pallas_call(
        functools.partial(_mix_in_kernel, tm=tm, tiles_per_seq=seq // tm),
        grid=(m // tm,),
        in_specs=[rows(d)] + [whole(a) for a in consts],
        out_specs=[rows(2 * DIFF_WIDTH), pl.BlockSpec((1, DIFF_WIDTH, tm), lambda i: (i, 0, 0))]
        + [hm_spec] * 5 + [pl.BlockSpec((RWKV_HEADS, tm // WKV_CHUNK, RWKV_HEAD), lambda i: (0, i, 0))]
        + [rows(w)] * 2,
        out_shape=[
            jax.ShapeDtypeStruct((m, 2 * DIFF_WIDTH), BF16),
            jax.ShapeDtypeStruct((m // tm, DIFF_WIDTH, tm), BF16),
        ] + [head_major(BF16)] * 5
        + [jax.ShapeDtypeStruct((RWKV_HEADS, m // WKV_CHUNK, RWKV_HEAD), F32)]
        + [jax.ShapeDtypeStruct((m, w), F32)] * 2,
        scratch_shapes=[pltpu.VMEM((1, RWKV_COLS), F32)],
        compiler_params=pltpu.CompilerParams(
            dimension_semantics=("arbitrary",), vmem_limit_bytes=VMEM_LIMIT_BYTES),
        name="mix_in",
    )(x2, *consts)


def _wkv_lanes(ref):
    heads, batch = ref.shape[0], ref.shape[1]
    return [(h, b) for b in range(batch) for h in range(heads)]


def _widen(x):
    return jnp.concatenate([x, jnp.zeros_like(x)], axis=1)


def _wkv_chunk_setup(rt_ref, kt_ref, bt_ref, at_ref, v_ref, gam_ref, xs_ref, ar_ref, bkt_ref, gcol_ref,
                     *, nc, chunk0):
    lanes = _wkv_lanes(rt_ref)
    c = WKV_CHUNK
    row = lax.broadcasted_iota(jnp.int32, (c, 2 * c), 0)
    col = lax.broadcasted_iota(jnp.int32, (c, 2 * c), 1)
    strict2 = row > col % c
    incl2 = row >= col % c
    zeros_b = jnp.zeros((c, c), BF16)
    bf = lambda xs: [x.astype(BF16) for x in xs]
    items = [(slice(ci * c, (ci + 1) * c), h, b) for ci in range(nc) for h, b in lanes]
    chunk_of = [ci for ci in range(nc) for _ in lanes]
    m = len(items)
    a = [at_ref[h, b, sl, :] for sl, h, b in items]
    r = [rt_ref[h, b, sl, :] for sl, h, b in items]
    v = [v_ref[h, b, sl, :] for sl, h, b in items]
    bk = [jnp.concatenate([bt_ref[h, b, sl, :], kt_ref[h, b, sl, :]], axis=0) for sl, h, b in items]
    prod = [_nt(jnp.concatenate([a[j], r[j]], axis=0), bk[j]) for j in range(m)]
    a_a = [jnp.where(strict2, prod[j][:c], 0.0) for j in range(m)]
    for j in range(m):
        ar_ref[j] = jnp.where(incl2, prod[j][c:], 0.0).astype(BF16)
    zv = [jnp.concatenate([zeros_b, v[j]], axis=0) for j in range(m)]
    av = [_mm(a_a[j].astype(BF16), _widen(zv[j])) for j in range(m)]
    x = [_widen(a[j]).astype(F32) + pltpu.roll(av[j], c, axis=1) for j in range(m)]
    power = bf([mat[:, :c] for mat in a_a])
    levels = int(math.log2(c))
    for lv in range(levels):
        if lv + 1 < levels:
            rhs = [jnp.concatenate([x[j].astype(BF16), _widen(power[j])], axis=1)
                   for j in range(m)]
            res = [_mm(power[j], rhs[j]) for j in range(m)]
            x = [x[j] + res[j][:, :2 * c] for j in range(m)]
            power = bf([res[j][:, 2 * c:3 * c] for j in range(m)])
        else:
            x = [x[j] + _mm(power[j], x[j].astype(BF16)) for j in range(m)]
    for j, (sl, h, b) in enumerate(items):
        xs_ref[j] = x[j].astype(BF16)
        bkt_ref[j] = bk[j].T
        g_last = gam_ref[h, b, pl.ds(chunk0 + chunk_of[j], 1), :]
        gcol_ref[j] = jnp.broadcast_to(g_last, (c, c)).T


def _wkv_chunk_chain(rt_ref, v_ref, xs_ref, ar_ref, bkt_ref, gcol_ref, s_ref, y_ref, *, nc):
    lanes = _wkv_lanes(rt_ref)
    n = len(lanes)
    c = WKV_CHUNK
    eye = (lax.broadcasted_iota(jnp.int32, (c, c), 0)
           == lax.broadcasted_iota(jnp.int32, (c, c), 1)).astype(BF16)
    bf = lambda xs: [x.astype(BF16) for x in xs]
    s = [s_ref[i] for i in range(n)]
    for ci in range(nc):
        sl = slice(ci * c, (ci + 1) * c)
        r = [rt_ref[h, b, sl, :] for h, b in lanes]
        v = [v_ref[h, b, sl, :] for h, b in lanes]
        s_b = bf(s)
        s_i = [jnp.concatenate([s_b[i], eye], axis=0) for i in range(n)]
        u = bf([_mm(xs_ref[ci * n + i], s_i[i]) for i in range(n)])
        uv = [jnp.concatenate([u[i], v[i]], axis=0) for i in range(n)]
        s_new = [s[i] + _mm(bkt_ref[ci * n + i], uv[i]) for i in range(n)]
        y = [_mm(r[i], s_b[i]) + _mm(ar_ref[ci * n + i], uv[i]) for i in range(n)]
        for i, (h, b) in enumerate(lanes):
            s[i] = s_new[i] * gcol_ref[ci * n + i]
            mean = jnp.mean(y[i], axis=-1, keepdims=True)
            yc = y[i] - mean
            var = jnp.mean(yc * yc, axis=-1, keepdims=True)
            y_ref[b, sl, h * RWKV_HEAD:(h + 1) * RWKV_HEAD] = yc * lax.rsqrt(var + GN_EPS)
    for i in range(n):
        s_ref[i] = s[i]


def _seq_mix_kernel(q_ref, k_ref, vt_ref, lq1_ref, lk1_ref, lq2_ref, lk2_ref, sw_ref,
                    rt_ref, kt_ref, bt_ref, at_ref, v_ref, gam_ref,
                    o_ref, y_ref,
                    m_ref, acc_ref, st_ref, bmax_ref, s_ref, xs_ref, ar_ref, bkt_ref, gcol_ref,
                    *, tq, nq, nc, lambda_init):
    step = pl.program_id(0)
    qi = step % nq

    @pl.when(step == 0)
    def _():
        s_ref[...] = jnp.zeros_like(s_ref)

    hw = DIFF_VDIM
    streams = [(hd, s) for hd in range(ATTN_HEADS_PER_STEP) for s in range(2)]
    q = q_ref[...]
    lane = lax.broadcasted_iota(jnp.int32, (tq, hw), 1)
    q_half = []
    for hd, s in streams:
        qh = q[:, hd * hw:(hd + 1) * hw]
        keep = (lane < DIFF_HALF) if s == 0 else (lane >= DIFF_HALF)
        q_half.append(jnp.where(keep, qh, jnp.zeros_like(qh)))
    ones_rows = jnp.ones((acc_ref.shape[1] - hw, tq), BF16)

    m_ref[...] = jnp.full_like(m_ref, -jnp.inf)
    acc_ref[...] = jnp.zeros_like(acc_ref)

    def scores(j, slot):
        start = pl.multiple_of(j * tq, tq)
        kb = k_ref[pl.ds(start, tq), :]
        for i, (hd, s) in enumerate(streams):
            st = _nt(kb[:, hd * hw:(hd + 1) * hw], q_half[i])
            st_ref[slot, i] = st
            bmax_ref[slot, i] = jnp.max(st, axis=0, keepdims=True)

    def accumulate(j, slot, diagonal):
        vt = vt_ref[j]
        vtb = [jnp.concatenate([vt[hd * hw:(hd + 1) * hw], ones_rows], axis=0)
               for hd in range(ATTN_HEADS_PER_STEP)]
        st = [st_ref[slot, i] for i in range(len(streams))]
        if diagonal:
            ck = lax.broadcasted_iota(jnp.int32, st[0].shape, 0) // ATTN_CHUNK
            cq = lax.broadcasted_iota(jnp.int32, st[0].shape, 1) // ATTN_CHUNK
            st = [jnp.where(ck <= cq, x, -jnp.inf) for x in st]
            bmax = [jnp.max(x, axis=0, keepdims=True) for x in st]
        else:
            bmax = [bmax_ref[slot, i] for i in range(len(streams))]
        m_prev = [m_ref[i] for i in range(len(streams))]
        m_new = [jnp.maximum(a, b) for a, b in zip(m_prev, bmax)]
        p = [jnp.exp2(x - mx).astype(BF16) for x, mx in zip(st, m_new)]
        alpha = [jnp.exp2(a - b) for a, b in zip(m_prev, m_new)]
        for i, (hd, s) in enumerate(streams):
            acc_ref[i] = alpha[i] * acc_ref[i] + _mm(vtb[hd], p[i])
            m_ref[i] = m_new[i]

    _wkv_chunk_setup(rt_ref, kt_ref, bt_ref, at_ref, v_ref, gam_ref, xs_ref, ar_ref, bkt_ref, gcol_ref,
                     nc=nc, chunk0=step * nc)

    scores(0, 0)

    def block_pair(t):
        scores(t + 1, 1)
        accumulate(t, 0, False)
        scores(t + 2, 0)
        accumulate(t + 1, 1, False)

    def block_quad(quad, carry):
        block_pair(4 * quad)
        block_pair(4 * quad + 2)
        return carry

    lax.fori_loop(0, qi // 4, block_quad, 0)

    @pl.when((qi // 2) % 2 == 1)
    def _():
        block_pair(4 * (qi // 4))

    @pl.when(qi % 2 == 1)
    def _():
        scores(qi, 1)
        accumulate(qi - 1, 0, False)

    _wkv_chunk_chain(rt_ref, v_ref, xs_ref, ar_ref, bkt_ref, gcol_ref, s_ref, y_ref, nc=nc)
    accumulate(qi, qi % 2, True)

    lam = (jnp.exp(jnp.sum(lq1_ref[...] * lk1_ref[...], axis=-1, keepdims=True))
           - jnp.exp(jnp.sum(lq2_ref[...] * lk2_ref[...], axis=-1, keepdims=True)) + lambda_init)
    for hd in range(ATTN_HEADS_PER_STEP):
        num = [acc_ref[2 * hd + s, :hw, :] for s in range(2)]
        den = [acc_ref[2 * hd + s, hw:hw + 1, :] for s in range(2)]
        out_t = num[0] / den[0] - lam * (num[1] / den[1])
        out = _rms(out_t.T, sw_ref[...], SUBLN_EPS) * (1.0 - lambda_init)
        o_ref[:, hd * hw:(hd + 1) * hw] = out.astype(o_ref.dtype)


def _seq_mix(qk, vt, lq1, lk1, lq2, lk2, subln_w, rt, kt, bt, at, v, gam,
             *, batch, seq, lambda_init):
    m = qk.shape[0]
    tq = vt.shape[2]
    nq = seq // tq
    hs = ATTN_HEADS_PER_STEP
    groups = DIFF_HEADS // hs
    wide = hs * DIFF_VDIM
    n_streams = 2 * hs
    steps = batch * groups * nq
    heads, _, n = rt.shape
    tc = seq // steps
    assert tc % WKV_CHUNK == 0 and tc * steps == seq
    nc = tc // WKV_CHUNK
    shape4 = (heads, batch, seq, n)

    b_of = lambda s: s // (groups * nq)
    g_of = lambda s: (s // nq) % groups
    i_of = lambda s: s % nq
    const = dict(pipeline_mode=pl.Buffered(1))
    vec = lambda k: pl.BlockSpec((1, k), lambda s: (0, 0), **const)
    row = lambda a: a.reshape(1, -1)
    wkv_spec = pl.BlockSpec((heads, batch, tc, n), lambda s: (0, 0, s, 0))
    yb, yw = pl.pallas_call(
        functools.partial(_seq_mix_kernel, tq=tq, nq=nq, nc=nc, lambda_init=lambda_init),
        grid=(steps,),
        in_specs=[
            pl.BlockSpec((tq, wide), lambda s: (b_of(s) * nq + i_of(s), g_of(s))),
            pl.BlockSpec((seq, wide), lambda s: (b_of(s), groups + g_of(s))),
            pl.BlockSpec((nq, wide, tq), lambda s: (b_of(s), g_of(s), 0)),
            vec(DIFF_HALF), vec(DIFF_HALF), vec(DIFF_HALF), vec(DIFF_HALF), vec(DIFF_VDIM),
        ] + [wkv_spec] * 5
        + [pl.BlockSpec((heads, batch, seq // WKV_CHUNK, n), lambda s: (0, 0, 0, 0), **const)],
        out_specs=[
            pl.BlockSpec((tq, wide), lambda s: (b_of(s) * nq + i_of(s), g_of(s))),
            pl.BlockSpec((batch, tc, heads * n), lambda s: (0, s, 0)),
        ],
        out_shape=[
            jax.ShapeDtypeStruct((m, DIFF_WIDTH), BF16),
            jax.ShapeDtypeStruct((batch, seq, heads * n), F32),
        ],
        scratch_shapes=[
            pltpu.VMEM((n_streams, 1, tq), F32),
            pltpu.VMEM((n_streams, DIFF_VDIM + BF16_SUBLANES, tq), F32),
            pltpu.VMEM((2, n_streams, tq, tq), F32),
            pltpu.VMEM((2, n_streams, 1, tq), F32),
            pltpu.VMEM((heads * batch, n, n), F32),
            pltpu.VMEM((nc * heads * batch, WKV_CHUNK, 2 * WKV_CHUNK), BF16),
            pltpu.VMEM((nc * heads * batch, WKV_CHUNK, 2 * WKV_CHUNK), BF16),
            pltpu.VMEM((nc * heads * batch, WKV_CHUNK, 2 * WKV_CHUNK), BF16),
            pltpu.VMEM((nc * heads * batch, WKV_CHUNK, WKV_CHUNK), F32),
        ],
        compiler_params=pltpu.CompilerParams(
            dimension_semantics=("arbitrary",), vmem_limit_bytes=VMEM_LIMIT_BYTES),
        name="seq_mix",
    )(qk, qk, vt, row(lq1), row(lk1), row(lq2), row(lk2), row(subln_w),
      *[x.reshape(shape4) for x in (rt, kt, bt, at, v)],
      gam.reshape(heads, batch, seq // WKV_CHUNK, n))
    return yb, yw.reshape(m, heads * n)


def _post_kernel(x_ref, yw_ref, bonus_ref, g_ref, yb_ref, lnw_ref, lnb_ref, wo_ref,
                 nffn_ref, wg_ref, wu_ref, wd_ref, nfin_ref, o_ref, *, tff, final_norm):
    ya = (yw_ref[...] * lnw_ref[...] + lnb_ref[...] + bonus_ref[...]) * g_ref[...]
    mixed = jnp.concatenate([ya.astype(BF16), yb_ref[...]], axis=-1)
    h1 = x_ref[...] + _mm(mixed, wo_ref[...])
    u = _rms(h1, nffn_ref[...], NORM_EPS).astype(BF16)
    acc = jnp.zeros_like(h1)
    for c0 in range(0, wg_ref.shape[1], tff):
        gt = _mm(u, wg_ref[:, c0:c0 + tff])
        up = _mm(u, wu_ref[:, c0:c0 + tff])
        act = (gt * _sigmoid(gt) * up).astype(BF16)
        acc = acc + _mm(act, wd_ref[c0:c0 + tff, :])
    h2 = h1 + acc
    if final_norm:
        h2 = _rms(h2, nfin_ref[...], NORM_EPS)
    o_ref[...] = h2


def _post(x2, yw, bonus, gate, yb, ln_w, ln_b, w_out_bf16, norm_ffn_w, wg, wu, wd, norm_final_w,
          *, final_norm, tm=512, tff=256):
    m, d = x2.shape
    const = dict(pipeline_mode=pl.Buffered(1))
    whole = lambda a: pl.BlockSpec(a.shape, lambda i: (0,) * a.ndim, **const)
    rows = lambda n: pl.BlockSpec((tm, n), lambda i: (i, 0))
    row = lambda a: a.reshape(1, -1)
    consts = (row(ln_w), row(ln_b), w_out_bf16, row(norm_ffn_w), wg, wu, wd, row(norm_final_w))
    return pl.pallas_call(
        functools.partial(_post_kernel, tff=tff, final_norm=final_norm),
        grid=(m // tm,),
        in_specs=[rows(d), rows(RWKV_WIDTH), rows(RWKV_WIDTH), rows(RWKV_WIDTH), rows(DIFF_WIDTH)]
        + [whole(a) for a in consts],
        out_specs=rows(d),
        out_shape=jax.ShapeDtypeStruct((m, d), F32),
        compiler_params=pltpu.CompilerParams(
            dimension_semantics=("arbitrary",), vmem_limit_bytes=VMEM_LIMIT_BYTES),
        name="post",
    )(x2, yw, bonus, gate, yb, *consts)


def kernel(x, norm_mix_w, w_in, mu_shift, w0, w_lora_up, a0, a_lora_up, g_lora_up, k_k, k_a, r_k,
           ln_x_w, ln_x_b, lambda_q1, lambda_k1, lambda_q2, lambda_k2, subln_w, w_out,
           norm_ffn_w, w_gate, w_up, w_down, norm_final_w):
    batch, seq, d = x.shape
    depth = w_in.shape[0]
    h = x.reshape(batch * seq, d)
    for l in range(depth):
        lambda_init = 0.8 - 0.6 * math.exp(-0.3 * l)
        w2pad = jnp.concatenate(
            [w_lora_up[l], jnp.zeros((AAA_LORA, RWKV_WIDTH), F32)], axis=0).astype(BF16)
        a2pad = jnp.concatenate(
            [jnp.zeros((DECAY_LORA, RWKV_WIDTH), F32), a_lora_up[l]], axis=0).astype(BF16)

        qk, vt, rt, kt, bt, at, v, gam, bonus, gate = _mix_in(
            h, norm_mix_w[l], w_in[l].astype(BF16), mu_shift[l], w0[l], w2pad, a0[l], a2pad,
            g_lora_up[l].astype(BF16), k_k[l], k_a[l], r_k[l].reshape(-1), seq=seq, tm=ATTN_TILE)
        yb, yw = _seq_mix(qk, vt, lambda_q1[l], lambda_k1[l], lambda_q2[l], lambda_k2[l], subln_w[l],
                          rt, kt, bt, at, v, gam, batch=batch, seq=seq, lambda_init=lambda_init)
        h = _post(h, yw, bonus, gate, yb, ln_x_w[l], ln_x_b[l], w_out[l].astype(BF16),
                  norm_ffn_w[l], w_gate[l].astype(BF16), w_up[l].astype(BF16),
                  w_down[l].astype(BF16), norm_final_w, final_norm=(l == depth - 1))
    return h.reshape(batch, seq, d)
```
